```python
import jax, jax.numpy as jnp
from jax import lax
import numpy as np

D_MODEL = 2048
BATCH = 8
SEQ = 2048
DEPTH = 1
DEC_BATCH = 128
DEC_SEQ = 1
PAST_LEN = 8192
PAGE_SIZE = 128

N_META = 16
MIX_DIM = D_MODEL
ATTN_DIM = MIX_DIM // 2
CONV_DIM = MIX_DIM - ATTN_DIM
HEAD_DIM = 64
N_HEADS = ATTN_DIM // HEAD_DIM
N_KV_HEADS = 4
GQA_GROUP = N_HEADS // N_KV_HEADS
KV_DIM = N_KV_HEADS * HEAD_DIM
WINDOW = 128
BLOCK = WINDOW
CONV_WIDTH = 3
D_FF = 5632
IN_DIM = ATTN_DIM + 2 * KV_DIM + 3 * CONV_DIM
SPLITS = [ATTN_DIM, ATTN_DIM + KV_DIM, ATTN_DIM + 2 * KV_DIM,
          ATTN_DIM + 2 * KV_DIM + CONV_DIM, ATTN_DIM + 2 * KV_DIM + 2 * CONV_DIM]
EPS = 1e-5
NEG = -1e30

kernel_name = 'hymba_swa_sink_shortconv_macaron_step'


def rmsnorm(x, g):
    xf = x.astype(jnp.float32)
    y = xf * lax.rsqrt(jnp.mean(xf * xf, axis=-1, keepdims=True) + EPS)
    return (y * g.astype(jnp.float32)).astype(x.dtype)


def swiglu(x, w_gate, w_up, w_down):
    return (jax.nn.silu(x @ w_gate) * (x @ w_up)) @ w_down


def mixer_inputs(xn, w_in):
    z = xn @ w_in
    q, k, v, gate_b, gate_c, h = jnp.split(z, SPLITS, axis=-1)
    bsz, s = xn.shape[:2]
    q = q.reshape(bsz, s, N_KV_HEADS, GQA_GROUP, HEAD_DIM)
    k = k.reshape(bsz, s, N_KV_HEADS, HEAD_DIM)
    v = v.reshape(bsz, s, N_KV_HEADS, HEAD_DIM)
    return q, k, v, gate_c * h, gate_b


def attend_with_sinks(q, k, v, mask, sink):
    s = jnp.einsum('...qkgd,...skd->...kgqs', q, k).astype(jnp.float32) * (HEAD_DIM ** -0.5)
    s = jnp.where(mask, s, NEG)
    sink_col = jnp.broadcast_to(sink.astype(jnp.float32).reshape(N_KV_HEADS, GQA_GROUP, 1, 1),
                                s.shape[:-1] + (1,))
    p = jax.nn.softmax(jnp.concatenate([s, sink_col], axis=-1), axis=-1)[..., :-1]
    return jnp.einsum('...kgqs,...skd->...qkgd', p.astype(v.dtype), v)


def swa_prompt(q, k, v, sink):
    bsz, L = q.shape[:2]
    pad_front = (-N_META) % BLOCK
    pad_end = (-(pad_front + L)) % BLOCK
    T = pad_front + L + pad_end
    nb = T // BLOCK
    qb = jnp.pad(q, ((0, 0), (pad_front, pad_end), (0, 0), (0, 0), (0, 0))).reshape(
        bsz, nb, BLOCK, N_KV_HEADS, GQA_GROUP, HEAD_DIM)

    def band(a):
        ab = jnp.pad(a, ((0, 0), (pad_front, pad_end), (0, 0), (0, 0))).reshape(
            bsz, nb, BLOCK, N_KV_HEADS, HEAD_DIM)
        prev = jnp.pad(ab, ((0, 0), (1, 0), (0, 0), (0, 0), (0, 0)))[:, :-1]
        return jnp.concatenate([prev, ab], axis=2)

    qpos = (jnp.arange(T) - pad_front).reshape(nb, BLOCK)
    kpos = jnp.concatenate([qpos - BLOCK, qpos], axis=1)
    diff = qpos[:, :, None] - kpos[:, None, :]
    mask = (kpos[:, None, :] >= 0) & (diff >= 0) & (diff <= WINDOW)
    out = attend_with_sinks(qb, band(k), band(v), mask[None, :, None, None], sink)
    return out.reshape(bsz, T, ATTN_DIM)[:, pad_front:pad_front + L]


def swa_sample(q, k, v, cache_k, cache_v, sink):
    bsz, s = q.shape[:2]
    w = cache_k.shape[1]
    k_all = jnp.concatenate([cache_k.astype(k.dtype), k], axis=1)
    v_all = jnp.concatenate([cache_v.astype(v.dtype), v], axis=1)
    diff = (jnp.arange(s) + w)[:, None] - jnp.arange(w + s)[None, :]
    mask = (diff >= 0) & (diff <= WINDOW)
    out = attend_with_sinks(q, k_all, v_all, mask, sink)
    return out.reshape(bsz, s, ATTN_DIM), k_all[:, -w:], v_all[:, -w:]


def short_conv(hist, u, gate_b, w_conv):
    s = u.shape[1]
    u_ext = jnp.concatenate([hist.astype(u.dtype), u], axis=1)
    y = u_ext[:, 0:s] * w_conv[0]
    for j in range(1, CONV_WIDTH):
        y = y + u_ext[:, j:j + s] * w_conv[j]
    return gate_b * y, u_ext[:, -(CONV_WIDTH - 1):]


def merge(attn_out, conv_out, g_a, g_c, w_out):
    return jnp.concatenate([rmsnorm(attn_out, g_a), rmsnorm(conv_out, g_c)], axis=-1) @ w_out


def setup_inputs(seed: int = 0) -> dict:
    key = jax.random.key(seed)
    ks = jax.random.split(key, 24)
    f = jnp.float32
    nrm = lambda k, shape, scale: jax.random.normal(k, shape, f) * scale
    gain = lambda k, shape: 1.0 + 0.05 * jax.random.normal(k, shape, f)
    w_buf = min(WINDOW, PAST_LEN)
    return {
        'x_prompt': nrm(ks[0], (BATCH, SEQ, D_MODEL), 1.0),
        'x_sample': nrm(ks[1], (DEC_BATCH, DEC_SEQ, D_MODEL), 1.0),
        'cache_swa_k': nrm(ks[2], (DEPTH, DEC_BATCH, w_buf, N_KV_HEADS, HEAD_DIM), 1.0),
        'cache_swa_v': nrm(ks[3], (DEPTH, DEC_BATCH, w_buf, N_KV_HEADS, HEAD_DIM), 1.0),
        'state_conv': nrm(ks[4], (DEPTH, DEC_BATCH, CONV_WIDTH - 1, CONV_DIM), 1.0),
        'meta_tokens': nrm(ks[5], (N_META, D_MODEL), 1.0),
        'g_ffn1': gain(ks[6], (DEPTH, D_MODEL)),
        'w1_gate': nrm(ks[7], (DEPTH, D_MODEL, D_FF), D_MODEL ** -0.5),
        'w1_up': nrm(ks[8], (DEPTH, D_MODEL, D_FF), D_MODEL ** -0.5),
        'w1_down': nrm(ks[9], (DEPTH, D_FF, D_MODEL), D_FF ** -0.5),
        'g_mix': gain(ks[10], (DEPTH, D_MODEL)),
        'w_in': nrm(ks[11], (DEPTH, D_MODEL, IN_DIM), D_MODEL ** -0.5),
        'attn_sinks': nrm(ks[12], (DEPTH, N_HEADS), 1.0),
        'w_conv': nrm(ks[13], (DEPTH, CONV_WIDTH, CONV_DIM), CONV_WIDTH ** -0.5),
        'g_attn_out': gain(ks[14], (DEPTH, ATTN_DIM)),
        'g_conv_out': gain(ks[15], (DEPTH, CONV_DIM)),
        'w_out': nrm(ks[16], (DEPTH, MIX_DIM, D_MODEL), MIX_DIM ** -0.5),
        'g_ffn2': gain(ks[17], (DEPTH, D_MODEL)),
        'w2_gate': nrm(ks[18], (DEPTH, D_MODEL, D_FF), D_MODEL ** -0.5),
        'w2_up': nrm(ks[19], (DEPTH, D_MODEL, D_FF), D_MODEL ** -0.5),
        'w2_down': nrm(ks[20], (DEPTH, D_FF, D_MODEL), D_FF ** -0.5),
        'g_final': gain(ks[21], (D_MODEL,)),
    }


def reference(x_prompt, x_sample, cache_swa_k, cache_swa_v, state_conv, meta_tokens,
              g_ffn1, w1_gate, w1_up, w1_down, g_mix, w_in, attn_sinks, w_conv,
              g_attn_out, g_conv_out, w_out, g_ffn2, w2_gate, w2_up, w2_down, g_final):
    bsz = x_prompt.shape[0]
    meta = jnp.broadcast_to(meta_tokens.astype(x_prompt.dtype)[None], (bsz, N_META, D_MODEL))
    xp = jnp.concatenate([meta, x_prompt], axis=1)
    xs = x_sample
    kp_l, vp_l, cp_l, ks_l, vs_l, cs_l = [], [], [], [], [], []
    for l in range(DEPTH):
        xp = xp + 0.5 * swiglu(rmsnorm(xp, g_ffn1[l]), w1_gate[l], w1_up[l], w1_down[l])
        xs = xs + 0.5 * swiglu(rmsnorm(xs, g_ffn1[l]), w1_gate[l], w1_up[l], w1_down[l])

        qp, kp, vp, up_, bp = mixer_inputs(rmsnorm(xp, g_mix[l]), w_in[l])
        ap = swa_prompt(qp, kp, vp, attn_sinks[l])
        hist0 = jnp.zeros((bsz, CONV_WIDTH - 1, CONV_DIM), up_.dtype)
        cp_out, cp_state = short_conv(hist0, up_, bp, w_conv[l])
        xp = xp + merge(ap, cp_out, g_attn_out[l], g_conv_out[l], w_out[l])
        kp_l.append(kp[:, -WINDOW:])
        vp_l.append(vp[:, -WINDOW:])
        cp_l.append(cp_state)

        qs, kss, vss, us_, bs_ = mixer_inputs(rmsnorm(xs, g_mix[l]), w_in[l])
        as_, ks_new, vs_new = swa_sample(qs, kss, vss, cache_swa_k[l], cache_swa_v[l], attn_sinks[l])
        cs_out, cs_state = short_conv(state_conv[l], us_, bs_, w_conv[l])
        xs = xs + merge(as_, cs_out, g_attn_out[l], g_conv_out[l], w_out[l])
        ks_l.append(ks_new)
        vs_l.append(vs_new)
        cs_l.append(cs_state)

        xp = xp + 0.5 * swiglu(rmsnorm(xp, g_ffn2[l]), w2_gate[l], w2_up[l], w2_down[l])
        xs = xs + 0.5 * swiglu(rmsnorm(xs, g_ffn2[l]), w2_gate[l], w2_up[l], w2_down[l])

    y_prompt = rmsnorm(xp, g_final)[:, N_META:]
    y_sample = rmsnorm(xs, g_final)
    new_swa_k_prompt = jnp.stack(kp_l, axis=0)
    new_swa_v_prompt = jnp.stack(vp_l, axis=0)
    new_conv_prompt = jnp.stack(cp_l, axis=0)
    new_swa_k_sample = jnp.stack(ks_l, axis=0)
    new_swa_v_sample = jnp.stack(vs_l, axis=0)
    new_conv_sample = jnp.stack(cs_l, axis=0)
    return (y_prompt, y_sample, new_swa_k_prompt, new_swa_v_prompt, new_conv_prompt,
            new_swa_k_sample, new_swa_v_sample, new_conv_sample)
```

```python
import functools

import jax
import jax.numpy as jnp
from jax import lax
from jax.experimental import pallas as pl
from jax.experimental.pallas import tpu as pltpu

D_MODEL = 2048
D_FF = 5632
N_META = 16
ATTN_DIM = 1024
CONV_DIM = 1024
HEAD_DIM = 64
N_HEADS = 16
N_KV_HEADS = 4
GQA_GROUP = 4
KV_DIM = 256
WINDOW = 128
IN_DIM = ATTN_DIM + 2 * KV_DIM + 3 * CONV_DIM
EPS = 1e-5
NEG = -1e30

LANES = 128
HALF = LANES // 2
VMEM_LIMIT = 56 * 1024 * 1024

F32 = jnp.float32
BF16 = jnp.bfloat16


def _rms(x, g):
    return x * lax.rsqrt(jnp.mean(x * x, axis=-1, keepdims=True) + EPS) * g


def _dot(a, b):
    return jnp.dot(a, b, preferred_element_type=F32)


def _dot_t(a, b):
    return lax.dot_general(a, b, (((1,), (1,)), ((), ())), preferred_element_type=F32)


def _resident(shape):
    return pl.BlockSpec(shape, lambda *_: (0,) * len(shape), pipeline_mode=pl.Buffered(1))


def _ffn_kernel(*refs, n_ff, has_mix, has_final):
    refs = list(refs)
    x_ref = refs.pop(0)
    if has_mix:
        a_ref, c_ref, woa_ref, woc_ref = refs[:4]
        refs = refs[4:]
    g_ref, wg_ref, wu_ref, wd_ref = refs[:4]
    refs = refs[4:]
    if has_final:
        gf_ref = refs.pop(0)
    out_ref, xn_ref = refs
    j = pl.program_id(1)

    @pl.when(j == 0)
    def _():
        x = x_ref[...]
        if has_mix:
            x = x + (_dot(a_ref[...].astype(BF16), woa_ref[...]) +
                     _dot(c_ref[...].astype(BF16), woc_ref[...]))
        out_ref[...] = x
        xn_ref[...] = _rms(x, g_ref[...]).astype(BF16)

    xn = xn_ref[...]
    gate = _dot(xn, wg_ref[...])
    up = _dot(xn, wu_ref[...])
    h = (gate * jax.nn.sigmoid(gate) * up * 0.5).astype(BF16)
    out_ref[...] += _dot(h, wd_ref[...])

    if has_final:
        @pl.when(j == n_ff - 1)
        def _():
            out_ref[...] = _rms(out_ref[...], gf_ref[...])


def _ffn(x, g, wg, wu, wd, *, tm, tf, mix=None, g_final=None, name):
    m = x.shape[0]
    assert m % tm == 0 and D_FF % tf == 0
    n_ff = D_FF // tf
    row = lambda i, j: (i, 0)
    in_specs = [pl.BlockSpec((tm, D_MODEL), row)]
    args = [x]
    if mix is not None:
        a, c, woa, woc = mix
        in_specs += [pl.BlockSpec((tm, ATTN_DIM), row), pl.BlockSpec((tm, CONV_DIM), row),
                     _resident((ATTN_DIM, D_MODEL)), _resident((CONV_DIM, D_MODEL))]
        args += [a, c, woa, woc]
    in_specs += [_resident((1, D_MODEL)),
                 pl.BlockSpec((D_MODEL, tf), lambda i, j: (0, j)),
                 pl.BlockSpec((D_MODEL, tf), lambda i, j: (0, j)),
                 pl.BlockSpec((tf, D_MODEL), lambda i, j: (j, 0))]
    args += [g, wg, wu, wd]
    if g_final is not None:
        in_specs.append(_resident((1, D_MODEL)))
        args.append(g_final)
    return pl.pallas_call(
        functools.partial(_ffn_kernel, n_ff=n_ff, has_mix=mix is not None,
                          has_final=g_final is not None),
        grid=(m // tm, n_ff),
        in_specs=in_specs,
        out_specs=pl.BlockSpec((tm, D_MODEL), row),
        out_shape=jax.ShapeDtypeStruct((m, D_MODEL), F32),
        scratch_shapes=[pltpu.VMEM((tm, D_MODEL), BF16)],
        compiler_params=pltpu.CompilerParams(
            dimension_semantics=("parallel", "arbitrary"), vmem_limit_bytes=VMEM_LIMIT),
        name=name,
    )(*args)


Q0, K0, V0, B0, C0, H0 = 0, 1024, 1280, 1536, 2560, 3584


def _project(x_ref, g_ref, w_ref):
    xn = _rms(x_ref[...], g_ref[...]).astype(BF16)
    mm = lambda lo, hi: _dot(xn, w_ref[:, lo:hi])
    q = mm(Q0, K0) * (HEAD_DIM ** -0.5)
    k = mm(K0, V0)
    v = mm(V0, B0)
    gate_b = mm(B0, C0)
    u = mm(C0, H0) * mm(H0, IN_DIM)
    return q, k, v, gate_b, u


def _proj_prompt_kernel(x_ref, g_ref, w_ref, wc_ref, gc_ref, umeta_ref,
                        q_ref, k_ref, v_ref, cn_ref, cst_ref, us_ref, *, tm, tiles_per_seq):
    t = pl.program_id(0) % tiles_per_seq
    q, k, v, gate_b, u = _project(x_ref, g_ref, w_ref)
    q_ref[...] = q.astype(BF16)
    k_ref[...] = k
    v_ref[...] = v

    @pl.when(t == 0)
    def _():
        us_ref[0:8, :] = umeta_ref[...]

    @pl.when(t != 0)
    def _():
        us_ref[0:8, :] = us_ref[tm:tm + 8, :]

    us_ref[8:8 + tm, :] = u
    y = (wc_ref[0:1, :] * us_ref[6:6 + tm, :] + wc_ref[1:2, :] * us_ref[7:7 + tm, :]
         + wc_ref[2:3, :] * u)
    cn_ref[...] = _rms(gate_b * y, gc_ref[...]).astype(BF16)
    cst_ref[0] = us_ref[tm + 6:tm + 8, :]


def _proj_prompt(x, g, w, wc, gc, umeta, *, tm, seq):
    m = x.shape[0]
    tiles_per_seq = seq // tm
    row = lambda i: (i, 0)
    return pl.pallas_call(
        functools.partial(_proj_prompt_kernel, tm=tm, tiles_per_seq=tiles_per_seq),
        grid=(m // tm,),
        in_specs=[pl.BlockSpec((tm, D_MODEL), row), _resident((1, D_MODEL)),
                  _resident((D_MODEL, IN_DIM)), _resident((3, CONV_DIM)),
                  _resident((1, CONV_DIM)), _resident((8, CONV_DIM))],
        out_specs=[pl.BlockSpec((tm, ATTN_DIM), row), pl.BlockSpec((tm, KV_DIM), row),
                   pl.BlockSpec((tm, KV_DIM), row), pl.BlockSpec((tm, CONV_DIM), row),
                   pl.BlockSpec((1, 2, CONV_DIM), lambda i: (i // tiles_per_seq, 0, 0))],
        out_shape=[jax.ShapeDtypeStruct((m, ATTN_DIM), BF16),
                   jax.ShapeDtypeStruct((m, KV_DIM), F32),
                   jax.ShapeDtypeStruct((m, KV_DIM), F32),
                   jax.ShapeDtypeStruct((m, CONV_DIM), BF16),
                   jax.ShapeDtypeStruct((m // seq, 2, CONV_DIM), F32)],
        scratch_shapes=[pltpu.VMEM((tm + 8, CONV_DIM), F32)],
        compiler_params=pltpu.CompilerParams(
            dimension_semantics=("arbitrary",), vmem_limit_bytes=VMEM_LIMIT),
        name="proj_prompt",
    )(x, g, w, wc, gc, umeta)


def _proj_tail_kernel(x_ref, g_ref, w_ref, wc_ref, gc_ref, h0_ref, h1_ref,
                      q_ref, k_ref, v_ref, u_ref, cn_ref):
    q, k, v, gate_b, u = _project(x_ref, g_ref, w_ref)
    q_ref[...] = q
    k_ref[...] = k
    v_ref[...] = v
    u_ref[...] = u
    y = wc_ref[0:1, :] * h0_ref[...] + wc_ref[1:2, :] * h1_ref[...] + wc_ref[2:3, :] * u
    cn_ref[...] = _rms(gate_b * y, gc_ref[...]).astype(BF16)


def _proj_tail(x, g, w, wc, gc, h0, h1):
    m = x.shape[0]
    full = lambda shape: pl.BlockSpec(shape, lambda i: (0,) * len(shape))
    return pl.pallas_call(
        _proj_tail_kernel,
        grid=(1,),
        in_specs=[full((m, D_MODEL)), full((1, D_MODEL)), _resident((D_MODEL, IN_DIM)),
                  full((3, CONV_DIM)), full((1, CONV_DIM)), full((m, CONV_DIM)),
                  full((m, CONV_DIM))],
        out_specs=[full((m, ATTN_DIM)), full((m, KV_DIM)), full((m, KV_DIM)),
                   full((m, CONV_DIM)), full((m, CONV_DIM))],
        out_shape=[jax.ShapeDtypeStruct((m, ATTN_DIM), F32),
                   jax.ShapeDtypeStruct((m, KV_DIM), F32),
                   jax.ShapeDtypeStruct((m, KV_DIM), F32),
                   jax.ShapeDtypeStruct((m, CONV_DIM), F32),
                   jax.ShapeDtypeStruct((m, CONV_DIM), BF16)],
        compiler_params=pltpu.CompilerParams(
            dimension_semantics=("arbitrary",), vmem_limit_bytes=VMEM_LIMIT),
        name="proj_tail",
    )(x, g, w, wc, gc, h0, h1)


def _softmax_with_sink(s, sink):
    mx = jnp.maximum(jnp.max(s, axis=-1, keepdims=True), sink)
    e = jnp.exp(s - mx)
    den = jnp.sum(e, axis=-1, keepdims=True) + jnp.exp(sink - mx)
    return e * (1.0 / den)


def _attn_prompt_kernel(sink_ref, q_ref, kp_ref, kc_ref, vp_ref, vc_ref, km_ref, vm_ref, ga_ref,
                        o_ref):
    i = pl.program_id(1)
    first = i == 0
    kk = jnp.concatenate([jnp.where(first, km_ref[...], kp_ref[...]), kc_ref[...]], axis=0)
    vv = jnp.concatenate([jnp.where(first, vm_ref[...], vp_ref[...]), vc_ref[...]], axis=0)
    r = lax.broadcasted_iota(jnp.int32, (WINDOW, 2 * WINDOW), 0)
    c = lax.broadcasted_iota(jnp.int32, (WINDOW, 2 * WINDOW), 1)
    valid = (c >= r) & (c <= r + WINDOW) & ((c >= WINDOW - N_META) | jnp.logical_not(first))
    low = lax.broadcasted_iota(jnp.int32, (2 * WINDOW, LANES), 1) < HALF

    outs = [None] * (ATTN_DIM // LANES)
    ssq = jnp.zeros((WINDOW, 1), F32)
    for pair in range(N_KV_HEADS // 2):
        kcol = kk[:, pair * LANES:(pair + 1) * LANES]
        vcol = vv[:, pair * LANES:(pair + 1) * LANES]
        k_half = (jnp.where(low, kcol, 0.0).astype(BF16), jnp.where(low, 0.0, kcol).astype(BF16))
        v_half = (jnp.where(low, vcol, 0.0).astype(BF16), jnp.where(low, 0.0, vcol).astype(BF16))
        for grp in range(GQA_GROUP):
            col = grp * (N_KV_HEADS // 2) + pair
            qc = q_ref[:, col * LANES:(col + 1) * LANES]
            o = None
            for half in range(2):
                head = (2 * pair + half) * GQA_GROUP + grp
                s = jnp.where(valid, _dot_t(qc, k_half[half]), NEG)
                p = _softmax_with_sink(s, sink_ref[head]).astype(BF16)
                pv = _dot(p, v_half[half])
                o = pv if o is None else o + pv
            outs[col] = o
            ssq = ssq + jnp.sum(o * o, axis=-1, keepdims=True)
    inv = lax.rsqrt(ssq * (1.0 / ATTN_DIM) + EPS)
    for col, o in enumerate(outs):
        sl = slice(col * LANES, (col + 1) * LANES)
        o_ref[:, sl] = (o * inv * ga_ref[:, sl]).astype(BF16)


def _attn_prompt(sinks, q, k, v, kmeta, vmeta, ga, *, batch, seq):
    nb = seq // WINDOW
    cur = lambda b, i: (b * nb + i, 0)
    prev = lambda b, i: (b * nb + jnp.maximum(i - 1, 0), 0)
    const = lambda shape: pl.BlockSpec(shape, lambda b, i: (0,) * len(shape))
    return pl.pallas_call(
        _attn_prompt_kernel,
        grid=(batch, nb),
        in_specs=[pl.BlockSpec(memory_space=pltpu.SMEM),
                  pl.BlockSpec((WINDOW, ATTN_DIM), cur),
                  pl.BlockSpec((WINDOW, KV_DIM), prev), pl.BlockSpec((WINDOW, KV_DIM), cur),
                  pl.BlockSpec((WINDOW, KV_DIM), prev), pl.BlockSpec((WINDOW, KV_DIM), cur),
                  const((WINDOW, KV_DIM)), const((WINDOW, KV_DIM)), const((1, ATTN_DIM))],
        out_specs=pl.BlockSpec((WINDOW, ATTN_DIM), cur),
        out_shape=jax.ShapeDtypeStruct((batch * seq, ATTN_DIM), BF16),
        compiler_params=pltpu.CompilerParams(
            dimension_semantics=("parallel", "arbitrary"), vmem_limit_bytes=VMEM_LIMIT),
        name="attn_prompt",
    )(sinks, q, k, k, v, v, kmeta, vmeta, ga)


def _attn_sample_kernel(sink_ref, q_ref, kn_ref, vn_ref, ck_ref, cv_ref, ga_ref,
                        ok_ref, ov_ref, an_ref, *, bs):
    ri = lax.broadcasted_iota(jnp.int32, (8, LANES), 0)
    li = lax.broadcasted_iota(jnp.int32, (8, LANES), 1)
    own_half = ((ri & 1) == 0) == (li < HALF)
    last_row = lax.broadcasted_iota(jnp.int32, (WINDOW, KV_DIM), 0) == WINDOW - 1

    def one_sequence(b, carry):
        kc = ck_ref[b]
        vc = cv_ref[b]
        k_new = jnp.where(last_row, kn_ref[pl.ds(b, 1), :], pltpu.roll(kc, WINDOW - 1, 0))
        v_new = jnp.where(last_row, vn_ref[pl.ds(b, 1), :], pltpu.roll(vc, WINDOW - 1, 0))
        ok_ref[b] = k_new
        ov_ref[b] = v_new
        qrow = q_ref[pl.ds(b, 1), :]
        pieces = [None] * (ATTN_DIM // LANES)
        for pair in range(N_KV_HEADS // 2):
            sl = slice(pair * LANES, (pair + 1) * LANES)
            lq = jnp.zeros((8, LANES), F32)
            sk = jnp.zeros((8, 1), F32)
            for grp in range(GQA_GROUP):
                col = grp * (N_KV_HEADS // 2) + pair
                qpiece = qrow[:, col * LANES:(col + 1) * LANES]
                lq = jnp.where((ri >> 1) == grp, qpiece, lq)
                for half in range(2):
                    head = (2 * pair + half) * GQA_GROUP + grp
                    sk = jnp.where(ri[:, 0:1] == 2 * grp + half, sink_ref[head], sk)
            lq = jnp.where(own_half, lq, 0.0).astype(BF16)
            s = _dot_t(lq, k_new[:, sl].astype(BF16))
            s0 = _dot_t(lq, kc[0:8, sl].astype(BF16))[:, 0:1]
            mx = jnp.maximum(jnp.maximum(jnp.max(s, axis=-1, keepdims=True), s0), sk)
            e = jnp.exp(s - mx)
            e0 = jnp.exp(s0 - mx)
            rden = 1.0 / (jnp.sum(e, axis=-1, keepdims=True) + e0 + jnp.exp(sk - mx))
            p = (e * rden).astype(BF16)
            p0 = (e0 * rden).astype(BF16).astype(F32)
            o = _dot(p, v_new[:, sl].astype(BF16)) + p0 * vc[0:1, sl].astype(BF16).astype(F32)
            o = jnp.where(li < HALF, o, pltpu.roll(o, 7, 0))
            for grp in range(GQA_GROUP):
                pieces[grp * (N_KV_HEADS // 2) + pair] = o[2 * grp:2 * grp + 1, :]
        orow = jnp.concatenate(pieces, axis=1)
        an_ref[pl.ds(b, 1), :] = _rms(orow, ga_ref[...])
        return carry

    lax.fori_loop(0, bs, one_sequence, 0)


def _attn_sample(sinks, q, kn, vn, ck, cv, ga, *, bs):
    nseq = q.shape[0]
    row = lambda i: (i, 0)
    blk3 = pl.BlockSpec((bs, WINDOW, KV_DIM), lambda i: (i, 0, 0))
    return pl.pallas_call(
        functools.partial(_attn_sample_kernel, bs=bs),
        grid=(nseq // bs,),
        in_specs=[pl.BlockSpec(memory_space=pltpu.SMEM),
                  pl.BlockSpec((bs, ATTN_DIM), row), pl.BlockSpec((bs, KV_DIM), row),
                  pl.BlockSpec((bs, KV_DIM), row), blk3, blk3,
                  pl.BlockSpec((1, ATTN_DIM), lambda i: (0, 0))],
        out_specs=[blk3, blk3, pl.BlockSpec((bs, ATTN_DIM), row)],
        out_shape=[jax.ShapeDtypeStruct(ck.shape, F32), jax.ShapeDtypeStruct(cv.shape, F32),
                   jax.ShapeDtypeStruct((nseq, ATTN_DIM), F32)],
        compiler_params=pltpu.CompilerParams(
            dimension_semantics=("parallel",), vmem_limit_bytes=VMEM_LIMIT),
        name="attn_sample",
    )(sinks, q, kn, vn, ck, cv, ga)


def _group_major(w, axis):
    shape = w.shape
    w = w.reshape(shape[:axis] + (N_KV_HEADS, GQA_GROUP, HEAD_DIM) + shape[axis + 1:])
    return jnp.swapaxes(w, axis, axis + 1).reshape(shape)


def kernel(x_prompt, x_sample, cache_swa_k, cache_swa_v, state_conv, meta_tokens, g_ffn1, w1_gate, w1_up, w1_down, g_mix, w_in, attn_sinks, w_conv, g_attn_out, g_conv_out, w_out, g_ffn2, w2_gate, w2_up, w2_down, g_final):
    assert g_ffn1.shape[0] == 1, "one layer"
    batch, seq, _ = x_prompt.shape
    n_dec = x_sample.shape[0]
    assert x_sample.shape[1] == 1 and cache_swa_k.shape[2] == WINDOW

    row2d = lambda a: a.reshape(1, -1).astype(F32)
    w1g, w1u, w1d = w1_gate[0].astype(BF16), w1_up[0].astype(BF16), w1_down[0].astype(BF16)
    w2g, w2u, w2d = w2_gate[0].astype(BF16), w2_up[0].astype(BF16), w2_down[0].astype(BF16)
    w_in_b = jnp.concatenate([_group_major(w_in[0, :, :ATTN_DIM], 1), w_in[0, :, ATTN_DIM:]],
                             axis=1).astype(BF16)
    wo_a = _group_major(w_out[0, :ATTN_DIM], 0).astype(BF16)
    wo_c = w_out[0, ATTN_DIM:].astype(BF16)
    ga = row2d(_group_major(g_attn_out[0], 0))
    gc = row2d(g_conv_out[0])
    sinks = attn_sinks[0].astype(F32)
    wc = w_conv[0].astype(F32)

    xp = x_prompt.reshape(batch * seq, D_MODEL)
    xt = jnp.concatenate([x_sample[:, 0, :], meta_tokens.astype(x_prompt.dtype)], axis=0)
    n_tail = xt.shape[0]

    xp1 = _ffn(xp, row2d(g_ffn1[0]), w1g, w1u, w1d, tm=512, tf=512, name="ffn1_prompt")
    xt1 = _ffn(xt, row2d(g_ffn1[0]), w1g, w1u, w1d, tm=n_tail, tf=512, name="ffn1_tail")

    st = state_conv[0]
    zpad = jnp.zeros((N_META, CONV_DIM), F32)
    h0 = jnp.concatenate([st[:, 0, :], zpad], axis=0)
    h1 = jnp.concatenate([st[:, 1, :], zpad], axis=0)
    qt, kt, vt, ut, cnt = _proj_tail(xt1, row2d(g_mix[0]), w_in_b, wc, gc, h0, h1)
    kmeta = jnp.pad(kt[n_dec:], ((WINDOW - N_META, 0), (0, 0)))
    vmeta = jnp.pad(vt[n_dec:], ((WINDOW - N_META, 0), (0, 0)))
    umeta = ut[n_tail - 8:]

    qp, kp, vp, cnp, cstate = _proj_prompt(xp1, row2d(g_mix[0]), w_in_b, wc, gc, umeta,
                                           tm=512, seq=seq)
    anp = _attn_prompt(sinks, qp, kp, vp, kmeta, vmeta, ga, batch=batch, seq=seq)

    ck = cache_swa_k[0].reshape(n_dec, WINDOW, KV_DIM)
    cv = cache_swa_v[0].reshape(n_dec, WINDOW, KV_DIM)
    nk, nv, ans = _attn_sample(sinks, qt[:n_dec], kt[:n_dec], vt[:n_dec], ck, cv, ga, bs=16)

    yp = _ffn(xp1, row2d(g_ffn2[0]), w2g, w2u, w2d, tm=512, tf=512,
              mix=(anp, cnp, wo_a, wo_c), g_final=row2d(g_final), name="ffn2_prompt")
    ys = _ffn(xt1[:n_dec], row2d(g_ffn2[0]), w2g, w2u, w2d, tm=n_dec, tf=512,
              mix=(ans, cnt[:n_dec], wo_a, wo_c), g_final=row2d(g_final), name="ffn2_tail")

    kv_shape = (1, batch, WINDOW, N_KV_HEADS, HEAD_DIM)
    new_k_prompt = kp.reshape(batch, seq, KV_DIM)[:, seq - WINDOW:].reshape(kv_shape)
    new_v_prompt = vp.reshape(batch, seq, KV_DIM)[:, seq - WINDOW:].reshape(kv_shape)
    new_conv_sample = jnp.stack([st[:, 1, :], ut[:n_dec]], axis=1)[None]
    return (yp.reshape(batch, seq, D_MODEL), ys.reshape(n_dec, 1, D_MODEL),
            new_k_prompt, new_v_prompt, cstate[None],
            nk.reshape(cache_swa_k.shape), nv.reshape(cache_swa_v.shape), new_conv_sample)
```

```python
import functools

import jax
import jax.numpy as jnp
from jax import lax
from jax.experimental import pallas as pl
from jax.experimental.pallas import tpu as pltpu

D_MODEL = 2048
D_FF = 5632
N_META = 16
ATTN_DIM = 1024
CONV_DIM = 1024
HEAD_DIM = 64
N_HEADS = 16
N_KV_HEADS = 4
GQA_GROUP = 4
KV_DIM = 256
WINDOW = 128
IN_DIM = ATTN_DIM + 2 * KV_DIM + 3 * CONV_DIM
EPS = 1e-5
NEG = -1e30

LANES = 128
HALF = LANES // 2
VMEM_LIMIT = 60 * 1024 * 1024
FFN_TM, FFN_TF = 1024, 512
FFN_MIX_TM = 512
PROJ_TM = 512
ATTN_NSUB = 2
SAMPLE_BS = 16

F32 = jnp.float32
BF16 = jnp.bfloat16


def _rms(x, g):
    return x * lax.rsqrt(jnp.mean(x * x, axis=-1, keepdims=True) + EPS) * g


def _dot(a, b):
    return jnp.dot(a, b, preferred_element_type=F32)


def _dot_t(a, b):
    return lax.dot_general(a, b, (((1,), (1,)), ((), ())), preferred_element_type=F32)


def _resident(shape):
    return pl.BlockSpec(shape, lambda *_: (0,) * len(shape), pipeline_mode=pl.Buffered(1))


def _ffn_kernel(*refs, n_ff, has_mix, has_final):
    refs = list(refs)
    x_ref = refs.pop(0)
    if has_mix:
        a_ref, c_ref, woa_ref, woc_ref = refs[:4]
        refs = refs[4:]
    g_ref, wg_ref, wu_ref, wd_ref = refs[:4]
    refs = refs[4:]
    if has_final:
        gf_ref = refs.pop(0)
    out_ref, xn_ref = refs
    j = pl.program_id(1)

    @pl.when(j == 0)
    def _():
        x = x_ref[...]
        if has_mix:
            x = x + (_dot(a_ref[...].astype(BF16), woa_ref[...]) +
                     _dot(c_ref[...].astype(BF16), woc_ref[...]))
        out_ref[...] = x
        xn_ref[...] = _rms(x, g_ref[...]).astype(BF16)

    xn = xn_ref[...]
    gate = _dot(xn, wg_ref[...])
    up = _dot(xn, wu_ref[...])
    h = (gate * jax.nn.sigmoid(gate) * up * 0.5).astype(BF16)
    out_ref[...] += _dot(h, wd_ref[...])

    if has_final:
        @pl.when(j == n_ff - 1)
        def _():
            out_ref[...] = _rms(out_ref[...], gf_ref[...])


def _ffn(x, g, wg, wu, wd, *, tm, tf, mix=None, g_final=None, name):
    m = x.shape[0]
    assert m % tm == 0 and D_FF % tf == 0
    n_ff = D_FF // tf
    row = lambda i, j: (i, 0)
    in_specs = [pl.BlockSpec((tm, D_MODEL), row)]
    args = [x]
    if mix is not None:
        a, c, woa, woc = mix
        in_specs += [pl.BlockSpec((tm, ATTN_DIM), row), pl.BlockSpec((tm, CONV_DIM), row),
                     _resident((ATTN_DIM, D_MODEL)), _resident((CONV_DIM, D_MODEL))]
        args += [a, c, woa, woc]
    in_specs += [_resident((1, D_MODEL)),
                 pl.BlockSpec((D_MODEL, tf), lambda i, j: (0, j)),
                 pl.BlockSpec((D_MODEL, tf), lambda i, j: (0, j)),
                 pl.BlockSpec((tf, D_MODEL), lambda i, j: (j, 0))]
    args += [g, wg, wu, wd]
    if g_final is not None:
        in_specs.append(_resident((1, D_MODEL)))
        args.append(g_final)
    return pl.pallas_call(
        functools.partial(_ffn_kernel, n_ff=n_ff, has_mix=mix is not None,
                          has_final=g_final is not None),
        grid=(m // tm, n_ff),
        in_specs=in_specs,
        out_specs=pl.BlockSpec((tm, D_MODEL), row),
        out_shape=jax.ShapeDtypeStruct((m, D_MODEL), F32),
        scratch_shapes=[pltpu.VMEM((tm, D_MODEL), BF16)],
        compiler_params=pltpu.CompilerParams(
            dimension_semantics=("parallel", "arbitrary"), vmem_limit_bytes=VMEM_LIMIT),
        name=name,
    )(*args)


Q0, K0, V0, B0, C0, H0 = 0, 1024, 1280, 1536, 2560, 3584


def _project_conv(x_ref, g_ref, w_ref):
    xn = _rms(x_ref[...], g_ref[...]).astype(BF16)
    gate_b = _dot(xn, w_ref[:, B0:C0])
    u = _dot(xn, w_ref[:, C0:H0]) * _dot(xn, w_ref[:, H0:IN_DIM])
    return xn, gate_b, u


def _project_qkv(xn, w_ref):
    q = _dot(xn, w_ref[:, Q0:K0]) * (HEAD_DIM ** -0.5)
    return q, _dot(xn, w_ref[:, K0:V0]), _dot(xn, w_ref[:, V0:B0])


def _proj_prompt_kernel(x_ref, g_ref, w_ref, wc_ref, gc_ref, umeta_ref,
                        q_ref, k_ref, v_ref, cn_ref, cst_ref, us_ref, *, tm, tiles_per_seq):
    t = pl.program_id(0) % tiles_per_seq
    xn, gate_b, u = _project_conv(x_ref, g_ref, w_ref)

    @pl.when(t == 0)
    def _():
        us_ref[0:8, :] = umeta_ref[...]

    @pl.when(t != 0)
    def _():
        us_ref[0:8, :] = us_ref[tm:tm + 8, :]

    us_ref[8:8 + tm, :] = u
    y = (wc_ref[0:1, :] * us_ref[6:6 + tm, :] + wc_ref[1:2, :] * us_ref[7:7 + tm, :]
         + wc_ref[2:3, :] * u)
    cn_ref[...] = _rms(gate_b * y, gc_ref[...]).astype(BF16)
    cst_ref[0] = us_ref[tm + 6:tm + 8, :]
    q, k, v = _project_qkv(xn, w_ref)
    q_ref[...] = q.astype(BF16)
    k_ref[...] = k
    v_ref[...] = v


def _proj_prompt(x, g, w, wc, gc, umeta, *, tm, seq):
    m = x.shape[0]
    tiles_per_seq = seq // tm
    row = lambda i: (i, 0)
    return pl.pallas_call(
        functools.partial(_proj_prompt_kernel, tm=tm, tiles_per_seq=tiles_per_seq),
        grid=(m // tm,),
        in_specs=[pl.BlockSpec((tm, D_MODEL), row), _resident((1, D_MODEL)),
                  _resident((D_MODEL, IN_DIM)), _resident((3, CONV_DIM)),
                  _resident((1, CONV_DIM)), _resident((8, CONV_DIM))],
        out_specs=[pl.BlockSpec((tm, ATTN_DIM), row), pl.BlockSpec((tm, KV_DIM), row),
                   pl.BlockSpec((tm, KV_DIM), row), pl.BlockSpec((tm, CONV_DIM), row),
                   pl.BlockSpec((1, 2, CONV_DIM), lambda i: (i // tiles_per_seq, 0, 0))],
        out_shape=[jax.ShapeDtypeStruct((m, ATTN_DIM), BF16),
                   jax.ShapeDtypeStruct((m, KV_DIM), F32),
                   jax.ShapeDtypeStruct((m, KV_DIM), F32),
                   jax.ShapeDtypeStruct((m, CONV_DIM), BF16),
                   jax.ShapeDtypeStruct((m // seq, 2, CONV_DIM), F32)],
        scratch_shapes=[pltpu.VMEM((tm + 8, CONV_DIM), F32)],
        compiler_params=pltpu.CompilerParams(
            dimension_semantics=("arbitrary",), vmem_limit_bytes=VMEM_LIMIT),
        name="proj_prompt",
    )(x, g, w, wc, gc, umeta)


def _proj_tail_kernel(x_ref, g_ref, w_ref, wc_ref, gc_ref, h0_ref, h1_ref,
                      q_ref, k_ref, v_ref, u_ref, cn_ref):
    xn, gate_b, u = _project_conv(x_ref, g_ref, w_ref)
    q, k, v = _project_qkv(xn, w_ref)
    q_ref[...] = q
    k_ref[...] = k
    v_ref[...] = v
    u_ref[...] = u
    y = wc_ref[0:1, :] * h0_ref[...] + wc_ref[1:2, :] * h1_ref[...] + wc_ref[2:3, :] * u
    cn_ref[...] = _rms(gate_b * y, gc_ref[...]).astype(BF16)


def _proj_tail(x, g, w, wc, gc, h0, h1):
    m = x.shape[0]
    full = lambda shape: pl.BlockSpec(shape, lambda i: (0,) * len(shape))
    return pl.pallas_call(
        _proj_tail_kernel,
        grid=(1,),
        in_specs=[full((m, D_MODEL)), full((1, D_MODEL)), _resident((D_MODEL, IN_DIM)),
                  full((3, CONV_DIM)), full((1, CONV_DIM)), full((m, CONV_DIM)),
                  full((m, CONV_DIM))],
        out_specs=[full((m, ATTN_DIM)), full((m, KV_DIM)), full((m, KV_DIM)),
                   full((m, CONV_DIM)), full((m, CONV_DIM))],
        out_shape=[jax.ShapeDtypeStruct((m, ATTN_DIM), F32),
                   jax.ShapeDtypeStruct((m, KV_DIM), F32),
                   jax.ShapeDtypeStruct((m, KV_DIM), F32),
                   jax.ShapeDtypeStruct((m, CONV_DIM), F32),
                   jax.ShapeDtypeStruct((m, CONV_DIM), BF16)],
        compiler_params=pltpu.CompilerParams(
            dimension_semantics=("arbitrary",), vmem_limit_bytes=VMEM_LIMIT),
        name="proj_tail",
    )(x, g, w, wc, gc, h0, h1)


def _attn_prompt_kernel(sink_ref, q_ref, kp_ref, kc_ref, vp_ref, vc_ref, km_ref, vm_ref, ga_ref,
                        o_ref, *, nsub):
    first = pl.program_id(1) == 0
    kall = jnp.concatenate([jnp.where(first, km_ref[...], kp_ref[...]), kc_ref[...]], axis=0)
    vall = jnp.concatenate([jnp.where(first, vm_ref[...], vp_ref[...]), vc_ref[...]], axis=0)
    c = lax.broadcasted_iota(jnp.int32, (2 * WINDOW, WINDOW), 0)
    r = lax.broadcasted_iota(jnp.int32, (2 * WINDOW, WINDOW), 1)
    band = (c >= r) & (c <= r + WINDOW)
    band_first = band & ((c >= WINDOW - N_META) | jnp.logical_not(first))
    low = lax.broadcasted_iota(jnp.int32, kall.shape[:1] + (LANES,), 1) < HALF

    n_pairs = N_KV_HEADS // 2
    k_half, vt_half = [], []
    for pair in range(n_pairs):
        kcol = kall[:, pair * LANES:(pair + 1) * LANES]
        vcol = vall[:, pair * LANES:(pair + 1) * LANES]
        k_half.append((jnp.where(low, kcol, 0.0).astype(BF16),
                       jnp.where(low, 0.0, kcol).astype(BF16)))
        vt_half.append((jnp.where(low, vcol, 0.0).T.astype(BF16),
                        jnp.where(low, 0.0, vcol).T.astype(BF16)))

    for sb in range(nsub):
        rows = slice(sb * WINDOW, (sb + 1) * WINDOW)
        keys = slice(sb * WINDOW, (sb + 2) * WINDOW)
        valid = band_first if sb == 0 else band
        outs = [None] * (ATTN_DIM // LANES)
        ssq = jnp.zeros((1, WINDOW), F32)
        for pair in range(n_pairs):
            kst = jnp.concatenate([k_half[pair][0][keys], k_half[pair][1][keys]], axis=0)
            vst = jnp.concatenate([vt_half[pair][0][:, keys], vt_half[pair][1][:, keys]], axis=1)
            qst = jnp.concatenate(
                [q_ref[rows, (grp * n_pairs + pair) * LANES:(grp * n_pairs + pair + 1) * LANES]
                 for grp in range(GQA_GROUP)], axis=0)
            st_all = _dot_t(kst, qst)
            pt_rows = []
            for half in range(2):
                pt_cols = []
                for grp in range(GQA_GROUP):
                    sink = sink_ref[(2 * pair + half) * GQA_GROUP + grp]
                    st = st_all[half * 2 * WINDOW:(half + 1) * 2 * WINDOW,
                                grp * WINDOW:(grp + 1) * WINDOW]
                    st = jnp.where(valid, st, NEG)
                    mx = jnp.maximum(jnp.max(st, axis=0, keepdims=True), sink)
                    e = jnp.exp(st - mx)
                    den = jnp.sum(e, axis=0, keepdims=True) + jnp.exp(sink - mx)
                    pt_cols.append((e * (1.0 / den)).astype(BF16))
                pt_rows.append(jnp.concatenate(pt_cols, axis=1))
            ot_all = _dot(vst, jnp.concatenate(pt_rows, axis=0))
            for grp in range(GQA_GROUP):
                ot = ot_all[:, grp * WINDOW:(grp + 1) * WINDOW]
                outs[grp * n_pairs + pair] = ot
                ssq = ssq + jnp.sum(ot * ot, axis=0, keepdims=True)
        inv = lax.rsqrt(ssq * (1.0 / ATTN_DIM) + EPS)
        for col, ot in enumerate(outs):
            sl = slice(col * LANES, (col + 1) * LANES)
            o_ref[rows, sl] = ((ot * inv).T * ga_ref[:, sl]).astype(BF16)


def _attn_prompt(sinks, q, k, v, kmeta, vmeta, ga, *, batch, seq, nsub):
    nb = seq // (nsub * WINDOW)
    cur = lambda b, i: (b * nb + i, 0)
    prev = lambda b, i: ((b * nb + i) * nsub - jnp.minimum(i, 1), 0)
    const = lambda shape: pl.BlockSpec(shape, lambda b, i: (0,) * len(shape))
    return pl.pallas_call(
        functools.partial(_attn_prompt_kernel, nsub=nsub),
        grid=(batch, nb),
        in_specs=[pl.BlockSpec(memory_space=pltpu.SMEM),
                  pl.BlockSpec((nsub * WINDOW, ATTN_DIM), cur),
                  pl.BlockSpec((WINDOW, KV_DIM), prev), pl.BlockSpec((nsub * WINDOW, KV_DIM), cur),
                  pl.BlockSpec((WINDOW, KV_DIM), prev), pl.BlockSpec((nsub * WINDOW, KV_DIM), cur),
                  const((WINDOW, KV_DIM)), const((WINDOW, KV_DIM)), const((1, ATTN_DIM))],
        out_specs=pl.BlockSpec((nsub * WINDOW, ATTN_DIM), cur),
        out_shape=jax.ShapeDtypeStruct((batch * seq, ATTN_DIM), BF16),
        compiler_params=pltpu.CompilerParams(
            dimension_semantics=("parallel", "arbitrary"), vmem_limit_bytes=VMEM_LIMIT),
        name="attn_prompt",
    )(sinks, q, k, k, v, v, kmeta, vmeta, ga)


def _attn_sample_kernel(sink_ref, q_ref, kn_ref, vn_ref, ck_ref, cv_ref, ga_ref,
                        ok_ref, ov_ref, an_ref, *, bs):
    n_pairs = N_KV_HEADS // 2
    heads_per_col = 2 * GQA_GROUP
    last_row = lax.broadcasted_iota(jnp.int32, (WINDOW, KV_DIM), 0) == WINDOW - 1
    for s in range(bs):
        ok_ref[s] = jnp.where(last_row, kn_ref[s:s + 1, :], pltpu.roll(ck_ref[s], WINDOW - 1, 0))
        ov_ref[s] = jnp.where(last_row, vn_ref[s:s + 1, :], pltpu.roll(cv_ref[s], WINDOW - 1, 0))

    kstack = ck_ref[...].reshape(bs * WINDOW, KV_DIM).astype(BF16)
    vstack = cv_ref[...].reshape(bs * WINDOW, KV_DIM).astype(BF16)
    n_rows = heads_per_col * bs
    row = lax.broadcasted_iota(jnp.int32, (n_rows, bs * WINDOW), 0)
    lane = lax.broadcasted_iota(jnp.int32, (n_rows, bs * WINDOW), 1)
    log2 = lambda n: n.bit_length() - 1
    assert bs == 1 << log2(bs) and WINDOW == 1 << log2(WINDOW)
    same_seq = (row & (bs - 1)) == (lane >> log2(WINDOW))
    head_row = lax.broadcasted_iota(jnp.int32, (n_rows, 1), 0) >> log2(bs)
    low = lax.broadcasted_iota(jnp.int32, (bs, LANES), 1) < HALF

    cols = [None] * (ATTN_DIM // LANES)
    for pair in range(n_pairs):
        sl = slice(pair * LANES, (pair + 1) * LANES)
        pieces = []
        sk = jnp.zeros((n_rows, 1), F32)
        for grp in range(GQA_GROUP):
            col = grp * n_pairs + pair
            qcol = q_ref[:, col * LANES:(col + 1) * LANES]
            pieces += [jnp.where(low, qcol, 0.0), jnp.where(low, 0.0, qcol)]
            for half in range(2):
                sink = sink_ref[(2 * pair + half) * GQA_GROUP + grp]
                sk = jnp.where(head_row == 2 * grp + half, sink, sk)
        lq = jnp.concatenate(pieces, axis=0).astype(BF16)
        s = jnp.where(same_seq, _dot_t(lq, kstack[:, sl]), NEG)
        k_new = jnp.concatenate([kn_ref[:, sl]] * heads_per_col, axis=0).astype(BF16)
        v_new = jnp.concatenate([vn_ref[:, sl]] * heads_per_col, axis=0).astype(BF16)
        s_new = jnp.sum(lq.astype(F32) * k_new.astype(F32), axis=-1, keepdims=True)
        mx = jnp.maximum(jnp.maximum(jnp.max(s, axis=-1, keepdims=True), s_new), sk)
        e = jnp.exp(s - mx)
        e_new = jnp.exp(s_new - mx)
        rden = 1.0 / (jnp.sum(e, axis=-1, keepdims=True) + e_new + jnp.exp(sk - mx))
        p = (e * rden).astype(BF16)
        p_new = (e_new * rden).astype(BF16).astype(F32)
        o = _dot(p, vstack[:, sl]) + p_new * v_new.astype(F32)
        for grp in range(GQA_GROUP):
            lo = o[(2 * grp) * bs:(2 * grp + 1) * bs]
            hi = o[(2 * grp + 1) * bs:(2 * grp + 2) * bs]
            cols[grp * n_pairs + pair] = jnp.where(low, lo, hi)
    an_ref[...] = _rms(jnp.concatenate(cols, axis=1), ga_ref[...])


def _attn_sample(sinks, q, kn, vn, ck, cv, ga, *, bs):
    nseq = q.shape[0]
    row = lambda i: (i, 0)
    blk3 = pl.BlockSpec((bs, WINDOW, KV_DIM), lambda i: (i, 0, 0))
    return pl.pallas_call(
        functools.partial(_attn_sample_kernel, bs=bs),
        grid=(nseq // bs,),
        in_specs=[pl.BlockSpec(memory_space=pltpu.SMEM),
                  pl.BlockSpec((bs, ATTN_DIM), row), pl.BlockSpec((bs, KV_DIM), row),
                  pl.BlockSpec((bs, KV_DIM), row), blk3, blk3,
                  pl.BlockSpec((1, ATTN_DIM), lambda i: (0, 0))],
        out_specs=[blk3, blk3, pl.BlockSpec((bs, ATTN_DIM), row)],
        out_shape=[jax.ShapeDtypeStruct(ck.shape, F32), jax.ShapeDtypeStruct(cv.shape, F32),
                   jax.ShapeDtypeStruct((nseq, ATTN_DIM), F32)],
        compiler_params=pltpu.CompilerParams(
            dimension_semantics=("parallel",), vmem_limit_bytes=VMEM_LIMIT),
        name="attn_sample",
    )(sinks, q, kn, vn, ck, cv, ga)


def _group_major(w, axis):
    shape = w.shape
    w = w.reshape(shape[:axis] + (N_KV_HEADS, GQA_GROUP, HEAD_DIM) + shape[axis + 1:])
    return jnp.swapaxes(w, axis, axis + 1).reshape(shape)


def kernel(x_prompt, x_sample, cache_swa_k, cache_swa_v, state_conv, meta_tokens, g_ffn1, w1_gate, w1_up, w1_down, g_mix, w_in, attn_sinks, w_conv, g_attn_out, g_conv_out, w_out, g_ffn2, w2_gate, w2_up, w2_down, g_final):
    assert g_ffn1.shape[0] == 1, "one layer"
    batch, seq, _ = x_prompt.shape
    n_dec = x_sample.shape[0]
    assert x_sample.shape[1] == 1 and cache_swa_k.shape[2] == WINDOW

    row2d = lambda a: a.reshape(1, -1).astype(F32)
    w1g, w1u, w1d = w1_gate[0].astype(BF16), w1_up[0].astype(BF16), w1_down[0].astype(BF16)
    w2g, w2u, w2d = w2_gate[0].astype(BF16), w2_up[0].astype(BF16), w2_down[0].astype(BF16)
    w_in_b = jnp.concatenate([_group_major(w_in[0, :, :ATTN_DIM], 1), w_in[0, :, ATTN_DIM:]],
                             axis=1).astype(BF16)
    wo_a = _group_major(w_out[0, :ATTN_DIM], 0).astype(BF16)
    wo_c = w_out[0, ATTN_DIM:].astype(BF16)
    ga = row2d(_group_major(g_attn_out[0], 0))
    gc = row2d(g_conv_out[0])
    sinks = attn_sinks[0].astype(F32)
    wc = w_conv[0].astype(F32)

    xp = x_prompt.reshape(batch * seq, D_MODEL)
    xt = jnp.concatenate([x_sample[:, 0, :], meta_tokens.astype(x_prompt.dtype)], axis=0)
    n_tail = xt.shape[0]

    xp1 = _ffn(xp, row2d(g_ffn1[0]), w1g, w1u, w1d, tm=FFN_TM, tf=FFN_TF, name="ffn1_prompt")
    xt1 = _ffn(xt, row2d(g_ffn1[0]), w1g, w1u, w1d, tm=n_tail, tf=FFN_TF, name="ffn1_tail")

    st = state_conv[0]
    zpad = jnp.zeros((N_META, CONV_DIM), F32)
    h0 = jnp.concatenate([st[:, 0, :], zpad], axis=0)
    h1 = jnp.concatenate([st[:, 1, :], zpad], axis=0)
    qt, kt, vt, ut, cnt = _proj_tail(xt1, row2d(g_mix[0]), w_in_b, wc, gc, h0, h1)
    kmeta = jnp.pad(kt[n_dec:], ((WINDOW - N_META, 0), (0, 0)))
    vmeta = jnp.pad(vt[n_dec:], ((WINDOW - N_META, 0), (0, 0)))
    umeta = ut[n_tail - 8:]

    qp, kp, vp, cnp, cstate = _proj_prompt(xp1, row2d(g_mix[0]), w_in_b, wc, gc, umeta,
                                           tm=PROJ_TM, seq=seq)
    anp = _attn_prompt(sinks, qp, kp, vp, kmeta, vmeta, ga, batch=batch, seq=seq,
                       nsub=ATTN_NSUB)

    ck = cache_swa_k[0].reshape(n_dec, WINDOW, KV_DIM)
    cv = cache_swa_v[0].reshape(n_dec, WINDOW, KV_DIM)
    nk, nv, ans = _attn_sample(sinks, qt[:n_dec], kt[:n_dec], vt[:n_dec], ck, cv, ga,
                               bs=SAMPLE_BS)

    yp = _ffn(xp1, row2d(g_ffn2[0]), w2g, w2u, w2d, tm=FFN_MIX_TM, tf=FFN_TF,
              mix=(anp, cnp, wo_a, wo_c), g_final=row2d(g_final), name="ffn2_prompt")
    ys = _ffn(xt1[:n_dec], row2d(g_ffn2[0]), w2g, w2u, w2d, tm=n_dec, tf=FFN_TF,
              mix=(ans, cnt[:n_dec], wo_a, wo_c), g_final=row2d(g_final), name="ffn2_tail")

    kv_shape = (1, batch, WINDOW, N_KV_HEADS, HEAD_DIM)
    new_k_prompt = kp.reshape(batch, seq, KV_DIM)[:, seq - WINDOW:].reshape(kv_shape)
    new_v_prompt = vp.reshape(batch, seq, KV_DIM)[:, seq - WINDOW:].reshape(kv_shape)
    new_conv_sample = jnp.stack([st[:, 1, :], ut[:n_dec]], axis=1)[None]
    return (yp.reshape(batch, seq, D_MODEL), ys.reshape(n_dec, 1, D_MODEL),
            new_k_prompt, new_v_prompt, cstate[None],
            nk.reshape(cache_swa_k.shape), nv.reshape(cache_swa_v.shape), new_conv_sample)
```

```python
import functools

import jax
import jax.numpy as jnp
from jax import lax
from jax.experimental import pallas as pl
from jax.experimental.pallas import tpu as pltpu

D_MODEL = 2048
D_FF = 5632
N_META = 16
ATTN_DIM = 1024
CONV_DIM = 1024
HEAD_DIM = 64
N_HEADS = 16
N_KV_HEADS = 4
GQA_GROUP = 4
KV_DIM = 256
WINDOW = 128
IN_DIM = ATTN_DIM + 2 * KV_DIM + 3 * CONV_DIM
EPS = 1e-5
NEG = -1e30

LANES = 128
HALF = LANES // 2
VMEM_LIMIT = 60 * 1024 * 1024
FFN_TM, FFN_TF = 1024, 512
MERGE_TM = 512
PROJ_TM = 512
PROJ_ROW_CHUNKS = 2
ATTN_NSUB = 2
SAMPLE_BS = 16

F32 = jnp.float32
BF16 = jnp.bfloat16


def _rms(x, g):
    return x * lax.rsqrt(jnp.mean(x * x, axis=-1, keepdims=True) + EPS) * g


def _dot(a, b):
    return jnp.dot(a, b, preferred_element_type=F32)


def _dot_t(a, b):
    return lax.dot_general(a, b, (((1,), (1,)), ((), ())), preferred_element_type=F32)


def _resident(shape):
    return pl.BlockSpec(shape, lambda *_: (0,) * len(shape), pipeline_mode=pl.Buffered(1))


def _ffn_kernel(*refs, n_ff, normed_input, has_final):
    refs = list(refs)
    x_ref, norm_ref, wg_ref, wu_ref, wd_ref = refs[:5]
    refs = refs[5:]
    if has_final:
        gf_ref = refs.pop(0)
    out_ref = refs.pop(0)
    xn_ref = norm_ref if normed_input else refs.pop(0)
    j = pl.program_id(1)

    @pl.when(j == 0)
    def _():
        x = x_ref[...]
        out_ref[...] = x
        if not normed_input:
            xn_ref[...] = _rms(x, norm_ref[...]).astype(BF16)

    xn = xn_ref[...]
    gate = _dot(xn, wg_ref[...])
    up = _dot(xn, wu_ref[...])
    h = (gate * jax.nn.sigmoid(gate) * up * 0.5).astype(BF16)
    out_ref[...] += _dot(h, wd_ref[...])

    if has_final:
        @pl.when(j == n_ff - 1)
        def _():
            out_ref[...] = _rms(out_ref[...], gf_ref[...])


def _ffn(x, norm, wg, wu, wd, *, tm, tf, g_final=None, name):
    m = x.shape[0]
    assert m % tm == 0 and D_FF % tf == 0
    n_ff = D_FF // tf
    normed_input = norm.shape[0] == m
    row = lambda i, j: (i, 0)
    in_specs = [pl.BlockSpec((tm, D_MODEL), row),
                pl.BlockSpec((tm, D_MODEL), row) if normed_input else _resident((1, D_MODEL)),
                pl.BlockSpec((D_MODEL, tf), lambda i, j: (0, j)),
                pl.BlockSpec((D_MODEL, tf), lambda i, j: (0, j)),
                pl.BlockSpec((tf, D_MODEL), lambda i, j: (j, 0))]
    args = [x, norm, wg, wu, wd]
    if g_final is not None:
        in_specs.append(_resident((1, D_MODEL)))
        args.append(g_final)
    return pl.pallas_call(
        functools.partial(_ffn_kernel, n_ff=n_ff, normed_input=normed_input,
                          has_final=g_final is not None),
        grid=(m // tm, n_ff),
        in_specs=in_specs,
        out_specs=pl.BlockSpec((tm, D_MODEL), row),
        out_shape=jax.ShapeDtypeStruct((m, D_MODEL), F32),
        scratch_shapes=[] if normed_input else [pltpu.VMEM((tm, D_MODEL), BF16)],
        compiler_params=pltpu.CompilerParams(
            dimension_semantics=("parallel", "arbitrary"), vmem_limit_bytes=VMEM_LIMIT),
        name=name,
    )(*args)


def _merge_kernel(x_ref, a_ref, c_ref, woa_ref, woc_ref, g_ref, x2_ref, xn_ref):
    x = x_ref[...] + (_dot(a_ref[...].astype(BF16), woa_ref[...]) +
                      _dot(c_ref[...].astype(BF16), woc_ref[...]))
    x2_ref[...] = x
    xn_ref[...] = _rms(x, g_ref[...]).astype(BF16)


def _merge(x, a, c, woa, woc, g, *, tm, name):
    m = x.shape[0]
    assert m % tm == 0
    row = lambda i: (i, 0)
    return pl.pallas_call(
        _merge_kernel,
        grid=(m // tm,),
        in_specs=[pl.BlockSpec((tm, D_MODEL), row), pl.BlockSpec((tm, ATTN_DIM), row),
                  pl.BlockSpec((tm, CONV_DIM), row), _resident((ATTN_DIM, D_MODEL)),
                  _resident((CONV_DIM, D_MODEL)), _resident((1, D_MODEL))],
        out_specs=[pl.BlockSpec((tm, D_MODEL), row), pl.BlockSpec((tm, D_MODEL), row)],
        out_shape=[jax.ShapeDtypeStruct((m, D_MODEL), F32),
                   jax.ShapeDtypeStruct((m, D_MODEL), BF16)],
        compiler_params=pltpu.CompilerParams(
            dimension_semantics=("parallel",), vmem_limit_bytes=VMEM_LIMIT),
        name=name,
    )(x, a, c, woa, woc, g)


Q0, K0, V0, B0, C0, H0 = 0, 1024, 1280, 1536, 2560, 3584


def _project_conv(x_ref, g_ref, w_ref):
    xn = _rms(x_ref[...], g_ref[...]).astype(BF16)
    gate_b = _dot(xn, w_ref[:, B0:C0])
    u = _dot(xn, w_ref[:, C0:H0]) * _dot(xn, w_ref[:, H0:IN_DIM])
    return xn, gate_b, u


def _project_qkv(xn, w_ref):
    q = _dot(xn, w_ref[:, Q0:K0]) * (HEAD_DIM ** -0.5)
    return q, _dot(xn, w_ref[:, K0:V0]), _dot(xn, w_ref[:, V0:B0])


def _proj_prompt_kernel(x_ref, g_ref, w_ref, wc_ref, gc_ref, umeta_ref,
                        q_ref, k_ref, v_ref, cn_ref, cst_ref, us_ref, *, tm, tiles_per_seq):
    t = pl.program_id(0) % tiles_per_seq

    @pl.when(t == 0)
    def _():
        us_ref[0:8, :] = umeta_ref[...]

    @pl.when(t != 0)
    def _():
        us_ref[0:8, :] = us_ref[tm:tm + 8, :]

    hm = tm // PROJ_ROW_CHUNKS
    for h in range(PROJ_ROW_CHUNKS):
        rows = slice(h * hm, (h + 1) * hm)
        xn, gate_b, u = _project_conv(x_ref.at[rows], g_ref, w_ref)
        us_ref[8 + h * hm:8 + (h + 1) * hm, :] = u
        y = (wc_ref[0:1, :] * us_ref[6 + h * hm:6 + (h + 1) * hm, :]
             + wc_ref[1:2, :] * us_ref[7 + h * hm:7 + (h + 1) * hm, :] + wc_ref[2:3, :] * u)
        cn_ref[rows, :] = _rms(gate_b * y, gc_ref[...]).astype(BF16)
        q, k, v = _project_qkv(xn, w_ref)
        q_ref[rows, :] = q.astype(BF16)
        k_ref[rows, :] = k
        v_ref[rows, :] = v
    cst_ref[0] = us_ref[tm + 6:tm + 8, :]


def _proj_prompt(x, g, w, wc, gc, umeta, *, tm, seq):
    m = x.shape[0]
    tiles_per_seq = seq // tm
    row = lambda i: (i, 0)
    return pl.pallas_call(
        functools.partial(_proj_prompt_kernel, tm=tm, tiles_per_seq=tiles_per_seq),
        grid=(m // tm,),
        in_specs=[pl.BlockSpec((tm, D_MODEL), row), _resident((1, D_MODEL)),
                  _resident((D_MODEL, IN_DIM)), _resident((3, CONV_DIM)),
                  _resident((1, CONV_DIM)), _resident((8, CONV_DIM))],
        out_specs=[pl.BlockSpec((tm, ATTN_DIM), row), pl.BlockSpec((tm, KV_DIM), row),
                   pl.BlockSpec((tm, KV_DIM), row), pl.BlockSpec((tm, CONV_DIM), row),
                   pl.BlockSpec((1, 2, CONV_DIM), lambda i: (i // tiles_per_seq, 0, 0))],
        out_shape=[jax.ShapeDtypeStruct((m, ATTN_DIM), BF16),
                   jax.ShapeDtypeStruct((m, KV_DIM), F32),
                   jax.ShapeDtypeStruct((m, KV_DIM), F32),
                   jax.ShapeDtypeStruct((m, CONV_DIM), BF16),
                   jax.ShapeDtypeStruct((m // seq, 2, CONV_DIM), F32)],
        scratch_shapes=[pltpu.VMEM((tm + 8, CONV_DIM), F32)],
        compiler_params=pltpu.CompilerParams(
            dimension_semantics=("arbitrary",), vmem_limit_bytes=VMEM_LIMIT),
        name="proj_prompt",
    )(x, g, w, wc, gc, umeta)


def _proj_tail_kernel(x_ref, g_ref, w_ref, wc_ref, gc_ref, h0_ref, h1_ref,
                      q_ref, k_ref, v_ref, u_ref, cn_ref):
    xn, gate_b, u = _project_conv(x_ref, g_ref, w_ref)
    q, k, v = _project_qkv(xn, w_ref)
    q_ref[...] = q
    k_ref[...] = k
    v_ref[...] = v
    u_ref[...] = u
    y = wc_ref[0:1, :] * h0_ref[...] + wc_ref[1:2, :] * h1_ref[...] + wc_ref[2:3, :] * u
    cn_ref[...] = _rms(gate_b * y, gc_ref[...]).astype(BF16)


def _proj_tail(x, g, w, wc, gc, h0, h1):
    m = x.shape[0]
    full = lambda shape: pl.BlockSpec(shape, lambda i: (0,) * len(shape))
    return pl.pallas_call(
        _proj_tail_kernel,
        grid=(1,),
        in_specs=[full((m, D_MODEL)), full((1, D_MODEL)), _resident((D_MODEL, IN_DIM)),
                  full((3, CONV_DIM)), full((1, CONV_DIM)), full((m, CONV_DIM)),
                  full((m, CONV_DIM))],
        out_specs=[full((m, ATTN_DIM)), full((m, KV_DIM)), full((m, KV_DIM)),
                   full((m, CONV_DIM)), full((m, CONV_DIM))],
        out_shape=[jax.ShapeDtypeStruct((m, ATTN_DIM), F32),
                   jax.ShapeDtypeStruct((m, KV_DIM), F32),
                   jax.ShapeDtypeStruct((m, KV_DIM), F32),
                   jax.ShapeDtypeStruct((m, CONV_DIM), F32),
                   jax.ShapeDtypeStruct((m, CONV_DIM), BF16)],
        compiler_params=pltpu.CompilerParams(
            dimension_semantics=("arbitrary",), vmem_limit_bytes=VMEM_LIMIT),
        name="proj_tail",
    )(x, g, w, wc, gc, h0, h1)


def _attn_prompt_kernel(sink_ref, q_ref, kp_ref, kc_ref, vp_ref, vc_ref, km_ref, vm_ref, ga_ref,
                        o_ref, *, nsub):
    first = pl.program_id(1) == 0
    kall = jnp.concatenate([jnp.where(first, km_ref[...], kp_ref[...]), kc_ref[...]], axis=0)
    vall = jnp.concatenate([jnp.where(first, vm_ref[...], vp_ref[...]), vc_ref[...]], axis=0)
    c = lax.broadcasted_iota(jnp.int32, (2 * WINDOW, WINDOW), 0)
    r = lax.broadcasted_iota(jnp.int32, (2 * WINDOW, WINDOW), 1)
    band = (c >= r) & (c <= r + WINDOW)
    band_first = band & ((c >= WINDOW - N_META) | jnp.logical_not(first))
    low = lax.broadcasted_iota(jnp.int32, kall.shape[:1] + (LANES,), 1) < HALF

    n_pairs = N_KV_HEADS // 2
    k_half, vt_half = [], []
    for pair in range(n_pairs):
        kcol = kall[:, pair * LANES:(pair + 1) * LANES]
        vcol = vall[:, pair * LANES:(pair + 1) * LANES]
        k_half.append((jnp.where(low, kcol, 0.0).astype(BF16),
                       jnp.where(low, 0.0, kcol).astype(BF16)))
        vt_half.append((jnp.where(low, vcol, 0.0).T.astype(BF16),
                        jnp.where(low, 0.0, vcol).T.astype(BF16)))

    for sb in range(nsub):
        rows = slice(sb * WINDOW, (sb + 1) * WINDOW)
        keys = slice(sb * WINDOW, (sb + 2) * WINDOW)
        valid = band_first if sb == 0 else band
        outs = [None] * (ATTN_DIM // LANES)
        ssq = jnp.zeros((1, WINDOW), F32)
        for pair in range(n_pairs):
            kst = jnp.concatenate([k_half[pair][0][keys], k_half[pair][1][keys]], axis=0)
            vst = jnp.concatenate([vt_half[pair][0][:, keys], vt_half[pair][1][:, keys]], axis=1)
            qst = jnp.concatenate(
                [q_ref[rows, (grp * n_pairs + pair) * LANES:(grp * n_pairs + pair + 1) * LANES]
                 for grp in range(GQA_GROUP)], axis=0)
            st_all = _dot_t(kst, qst)
            pt_rows = []
            for half in range(2):
                pt_cols = []
                for grp in range(GQA_GROUP):
                    sink = sink_ref[(2 * pair + half) * GQA_GROUP + grp]
                    st = st_all[half * 2 * WINDOW:(half + 1) * 2 * WINDOW,
                                grp * WINDOW:(grp + 1) * WINDOW]
                    st = jnp.where(valid, st, NEG)
                    mx = jnp.maximum(jnp.max(st, axis=0, keepdims=True), sink)
                    e = jnp.exp(st - mx)
                    den = jnp.sum(e, axis=0, keepdims=True) + jnp.exp(sink - mx)
                    pt_cols.append((e * (1.0 / den)).astype(BF16))
                pt_rows.append(jnp.concatenate(pt_cols, axis=1))
            ot_all = _dot(vst, jnp.concatenate(pt_rows, axis=0))
            for grp in range(GQA_GROUP):
                ot = ot_all[:, grp * WINDOW:(grp + 1) * WINDOW]
                outs[grp * n_pairs + pair] = ot
                ssq = ssq + jnp.sum(ot * ot, axis=0, keepdims=True)
        inv = lax.rsqrt(ssq * (1.0 / ATTN_DIM) + EPS)
        for col, ot in enumerate(outs):
            sl = slice(col * LANES, (col + 1) * LANES)
            o_ref[rows, sl] = ((ot * inv).T * ga_ref[:, sl]).astype(BF16)


def _attn_prompt(sinks, q, k, v, kmeta, vmeta, ga, *, batch, seq, nsub):
    nb = seq // (nsub * WINDOW)
    cur = lambda b, i: (b * nb + i, 0)
    prev = lambda b, i: ((b * nb + i) * nsub - jnp.minimum(i, 1), 0)
    const = lambda shape: pl.BlockSpec(shape, lambda b, i: (0,) * len(shape))
    return pl.pallas_call(
        functools.partial(_attn_prompt_kernel, nsub=nsub),
        grid=(batch, nb),
        in_specs=[pl.BlockSpec(memory_space=pltpu.SMEM),
                  pl.BlockSpec((nsub * WINDOW, ATTN_DIM), cur),
                  pl.BlockSpec((WINDOW, KV_DIM), prev), pl.BlockSpec((nsub * WINDOW, KV_DIM), cur),
                  pl.BlockSpec((WINDOW, KV_DIM), prev), pl.BlockSpec((nsub * WINDOW, KV_DIM), cur),
                  const((WINDOW, KV_DIM)), const((WINDOW, KV_DIM)), const((1, ATTN_DIM))],
        out_specs=pl.BlockSpec((nsub * WINDOW, ATTN_DIM), cur),
        out_shape=jax.ShapeDtypeStruct((batch * seq, ATTN_DIM), BF16),
        compiler_params=pltpu.CompilerParams(
            dimension_semantics=("parallel", "arbitrary"), vmem_limit_bytes=VMEM_LIMIT),
        name="attn_prompt",
    )(sinks, q, k, k, v, v, kmeta, vmeta, ga)


def _attn_sample_kernel(sink_ref, q_ref, kn_ref, vn_ref, ck_ref, cv_ref, ga_ref,
                        ok_ref, ov_ref, an_ref, *, bs):
    n_pairs = N_KV_HEADS // 2
    heads_per_col = 2 * GQA_GROUP
    last_row = lax.broadcasted_iota(jnp.int32, (WINDOW, KV_DIM), 0) == WINDOW - 1
    for s in range(bs):
        ok_ref[s] = jnp.where(last_row, kn_ref[s:s + 1, :], pltpu.roll(ck_ref[s], WINDOW - 1, 0))
        ov_ref[s] = jnp.where(last_row, vn_ref[s:s + 1, :], pltpu.roll(cv_ref[s], WINDOW - 1, 0))

    kstack = ck_ref[...].reshape(bs * WINDOW, KV_DIM).astype(BF16)
    vstack = cv_ref[...].reshape(bs * WINDOW, KV_DIM).astype(BF16)
    n_rows = heads_per_col * bs
    row = lax.broadcasted_iota(jnp.int32, (n_rows, bs * WINDOW), 0)
    lane = lax.broadcasted_iota(jnp.int32, (n_rows, bs * WINDOW), 1)
    log2 = lambda n: n.bit_length() - 1
    assert bs == 1 << log2(bs) and WINDOW == 1 << log2(WINDOW)
    same_seq = (row & (bs - 1)) == (lane >> log2(WINDOW))
    head_row = lax.broadcasted_iota(jnp.int32, (n_rows, 1), 0) >> log2(bs)
    low = lax.broadcasted_iota(jnp.int32, (bs, LANES), 1) < HALF

    cols = [None] * (ATTN_DIM // LANES)
    for pair in range(n_pairs):
        sl = slice(pair * LANES, (pair + 1) * LANES)
        pieces = []
        sk = jnp.zeros((n_rows, 1), F32)
        for grp in range(GQA_GROUP):
            col = grp * n_pairs + pair
            qcol = q_ref[:, col * LANES:(col + 1) * LANES]
            pieces += [jnp.where(low, qcol, 0.0), jnp.where(low, 0.0, qcol)]
            for half in range(2):
                sink = sink_ref[(2 * pair + half) * GQA_GROUP + grp]
                sk = jnp.where(head_row == 2 * grp + half, sink, sk)
        lq = jnp.concatenate(pieces, axis=0).astype(BF16)
        s = jnp.where(same_seq, _dot_t(lq, kstack[:, sl]), NEG)
        k_new = jnp.concatenate([kn_ref[:, sl]] * heads_per_col, axis=0).astype(BF16)
        v_new = jnp.concatenate([vn_ref[:, sl]] * heads_per_col, axis=0).astype(BF16)
        s_new = jnp.sum(lq.astype(F32) * k_new.astype(F32), axis=-1, keepdims=True)
        mx = jnp.maximum(jnp.maximum(jnp.max(s, axis=-1, keepdims=True), s_new), sk)
        e = jnp.exp(s - mx)
        e_new = jnp.exp(s_new - mx)
        rden = 1.0 / (jnp.sum(e, axis=-1, keepdims=True) + e_new + jnp.exp(sk - mx))
        p = (e * rden).astype(BF16)
        p_new = (e_new * rden).astype(BF16).astype(F32)
        o = _dot(p, vstack[:, sl]) + p_new * v_new.astype(F32)
        for grp in range(GQA_GROUP):
            lo = o[(2 * grp) * bs:(2 * grp + 1) * bs]
            hi = o[(2 * grp + 1) * bs:(2 * grp + 2) * bs]
            cols[grp * n_pairs + pair] = jnp.where(low, lo, hi)
    an_ref[...] = _rms(jnp.concatenate(cols, axis=1), ga_ref[...])


def _attn_sample(sinks, q, kn, vn, ck, cv, ga, *, bs):
    nseq = q.shape[0]
    row = lambda i: (i, 0)
    blk3 = pl.BlockSpec((bs, WINDOW, KV_DIM), lambda i: (i, 0, 0))
    return pl.pallas_call(
        functools.partial(_attn_sample_kernel, bs=bs),
        grid=(nseq // bs,),
        in_specs=[pl.BlockSpec(memory_space=pltpu.SMEM),
                  pl.BlockSpec((bs, ATTN_DIM), row), pl.BlockSpec((bs, KV_DIM), row),
                  pl.BlockSpec((bs, KV_DIM), row), blk3, blk3,
                  pl.BlockSpec((1, ATTN_DIM), lambda i: (0, 0))],
        out_specs=[blk3, blk3, pl.BlockSpec((bs, ATTN_DIM), row)],
        out_shape=[jax.ShapeDtypeStruct(ck.shape, F32), jax.ShapeDtypeStruct(cv.shape, F32),
                   jax.ShapeDtypeStruct((nseq, ATTN_DIM), F32)],
        compiler_params=pltpu.CompilerParams(
            dimension_semantics=("parallel",), vmem_limit_bytes=VMEM_LIMIT),
        name="attn_sample",
    )(sinks, q, kn, vn, ck, cv, ga)


def _group_major(w, axis):
    shape = w.shape
    w = w.reshape(shape[:axis] + (N_KV_HEADS, GQA_GROUP, HEAD_DIM) + shape[axis + 1:])
    return jnp.swapaxes(w, axis, axis + 1).reshape(shape)


def kernel(x_prompt, x_sample, cache_swa_k, cache_swa_v, state_conv, meta_tokens, g_ffn1, w1_gate, w1_up, w1_down, g_mix, w_in, attn_sinks, w_conv, g_attn_out, g_conv_out, w_out, g_ffn2, w2_gate, w2_up, w2_down, g_final):
    assert g_ffn1.shape[0] == 1, "one layer"
    batch, seq, _ = x_prompt.shape
    n_dec = x_sample.shape[0]
    assert x_sample.shape[1] == 1 and cache_swa_k.shape[2] == WINDOW

    row2d = lambda a: a.reshape(1, -1).astype(F32)
    w1g, w1u, w1d = w1_gate[0].astype(BF16), w1_up[0].astype(BF16), w1_down[0].astype(BF16)
    w2g, w2u, w2d = w2_gate[0].astype(BF16), w2_up[0].astype(BF16), w2_down[0].astype(BF16)
    w_in_b = jnp.concatenate([_group_major(w_in[0, :, :ATTN_DIM], 1), w_in[0, :, ATTN_DIM:]],
                             axis=1).astype(BF16)
    wo_a = _group_major(w_out[0, :ATTN_DIM], 0).astype(BF16)
    wo_c = w_out[0, ATTN_DIM:].astype(BF16)
    ga = row2d(_group_major(g_attn_out[0], 0))
    gc = row2d(g_conv_out[0])
    sinks = attn_sinks[0].astype(F32)
    wc = w_conv[0].astype(F32)

    xp = x_prompt.reshape(batch * seq, D_MODEL)
    xt = jnp.concatenate([x_sample[:, 0, :], meta_tokens.astype(x_prompt.dtype)], axis=0)
    n_tail = xt.shape[0]

    xp1 = _ffn(xp, row2d(g_ffn1[0]), w1g, w1u, w1d, tm=FFN_TM, tf=FFN_TF, name="ffn1_prompt")
    xt1 = _ffn(xt, row2d(g_ffn1[0]), w1g, w1u, w1d, tm=n_tail, tf=FFN_TF, name="ffn1_tail")

    st = state_conv[0]
    zpad = jnp.zeros((N_META, CONV_DIM), F32)
    h0 = jnp.concatenate([st[:, 0, :], zpad], axis=0)
    h1 = jnp.concatenate([st[:, 1, :], zpad], axis=0)
    qt, kt, vt, ut, cnt = _proj_tail(xt1, row2d(g_mix[0]), w_in_b, wc, gc, h0, h1)
    kmeta = jnp.pad(kt[n_dec:], ((WINDOW - N_META, 0), (0, 0)))
    vmeta = jnp.pad(vt[n_dec:], ((WINDOW - N_META, 0), (0, 0)))
    umeta = ut[n_tail - 8:]

    qp, kp, vp, cnp, cstate = _proj_prompt(xp1, row2d(g_mix[0]), w_in_b, wc, gc, umeta,
                                           tm=PROJ_TM, seq=seq)
    anp = _attn_prompt(sinks, qp, kp, vp, kmeta, vmeta, ga, batch=batch, seq=seq,
                       nsub=ATTN_NSUB)

    ck = cache_swa_k[0].reshape(n_dec, WINDOW, KV_DIM)
    cv = cache_swa_v[0].reshape(n_dec, WINDOW, KV_DIM)
    nk, nv, ans = _attn_sample(sinks, qt[:n_dec], kt[:n_dec], vt[:n_dec], ck, cv, ga,
                               bs=SAMPLE_BS)

    xp2, xpn = _merge(xp1, anp, cnp, wo_a, wo_c, row2d(g_ffn2[0]), tm=MERGE_TM,
                      name="merge_prompt")
    xs2, xsn = _merge(xt1[:n_dec], ans, cnt[:n_dec], wo_a, wo_c, row2d(g_ffn2[0]), tm=n_dec,
                      name="merge_tail")
    yp = _ffn(xp2, xpn, w2g, w2u, w2d, tm=FFN_TM, tf=FFN_TF, g_final=row2d(g_final),
              name="ffn2_prompt")
    ys = _ffn(xs2, xsn, w2g, w2u, w2d, tm=n_dec, tf=FFN_TF, g_final=row2d(g_final),
              name="ffn2_tail")

    kv_shape = (1, batch, WINDOW, N_KV_HEADS, HEAD_DIM)
    new_k_prompt = kp.reshape(batch, seq, KV_DIM)[:, seq - WINDOW:].reshape(kv_shape)
    new_v_prompt = vp.reshape(batch, seq, KV_DIM)[:, seq - WINDOW:].reshape(kv_shape)
    new_conv_sample = jnp.stack([st[:, 1, :], ut[:n_dec]], axis=1)[None]
    return (yp.reshape(batch, seq, D_MODEL), ys.reshape(n_dec, 1, D_MODEL),
            new_k_prompt, new_v_prompt, cstate[None],
            nk.reshape(cache_swa_k.shape), nv.reshape(cache_swa_v.shape), new_conv_sample)
```

```python
import functools

import jax
import jax.numpy as jnp
from jax import lax
from jax.experimental import pallas as pl
from jax.experimental.pallas import tpu as pltpu

D_MODEL = 2048
D_FF = 5632
N_META = 16
ATTN_DIM = 1024
CONV_DIM = 1024
HEAD_DIM = 64
N_HEADS = 16
N_KV_HEADS = 4
GQA_GROUP = 4
KV_DIM = 256
WINDOW = 128
IN_DIM = ATTN_DIM + 2 * KV_DIM + 3 * CONV_DIM
EPS = 1e-5
NEG = -1e30

LANES = 128
HALF = LANES // 2
VMEM_LIMIT = 60 * 1024 * 1024
FFN_TM, FFN_TF = 1024, 512
MERGE_TM = 512
PROJ_TM = 512
PROJ_ROW_CHUNKS = 2
ATTN_NSUB = 2
SAMPLE_BS = 16

F32 = jnp.float32
BF16 = jnp.bfloat16


def _rms(x, g):
    return x * lax.rsqrt(jnp.mean(x * x, axis=-1, keepdims=True) + EPS) * g


def _dot(a, b):
    return jnp.dot(a, b, preferred_element_type=F32)


def _dot_t(a, b):
    return lax.dot_general(a, b, (((1,), (1,)), ((), ())), preferred_element_type=F32)


def _resident(shape):
    return pl.BlockSpec(shape, lambda *_: (0,) * len(shape), pipeline_mode=pl.Buffered(1))


def _ffn_kernel(*refs, n_ff, normed_input, has_final, cast_weights):
    refs = list(refs)
    x_ref, norm_ref, wg_ref, wu_ref, wd_ref = refs[:5]
    refs = refs[5:]
    if has_final:
        gf_ref = refs.pop(0)
    out_ref = refs.pop(0)
    if cast_weights:
        wgb_ref, wub_ref, wdb_ref = refs[:3]
        refs = refs[3:]
    xn_ref = norm_ref if normed_input else refs.pop(0)
    j = pl.program_id(1)

    @pl.when(j == 0)
    def _():
        x = x_ref[...]
        out_ref[...] = x
        if not normed_input:
            xn_ref[...] = _rms(x, norm_ref[...]).astype(BF16)

    if cast_weights:
        wgb_ref[...] = wg_ref[...].astype(BF16)
        wub_ref[...] = wu_ref[...].astype(BF16)
        wdb_ref[...] = wd_ref[...].astype(BF16)
        wg_ref, wu_ref, wd_ref = wgb_ref, wub_ref, wdb_ref
    xn = xn_ref[...]
    gate = _dot(xn, wg_ref[...])
    up = _dot(xn, wu_ref[...])
    h = (gate * jax.nn.sigmoid(gate) * up * 0.5).astype(BF16)
    out_ref[...] += _dot(h, wd_ref[...])

    if has_final:
        @pl.when(j == n_ff - 1)
        def _():
            out_ref[...] = _rms(out_ref[...], gf_ref[...])


def _ffn(x, norm, wg, wu, wd, *, tm, tf, g_final=None, name):
    m = x.shape[0]
    assert m % tm == 0 and D_FF % tf == 0
    n_ff = D_FF // tf
    normed_input = norm.shape[0] == m
    cast_weights = wg.dtype == F32
    assert not cast_weights or m == tm
    row = lambda i, j: (i, 0)
    w_specs = [pl.BlockSpec((D_MODEL, tf), lambda i, j: (0, j)),
               pl.BlockSpec((D_MODEL, tf), lambda i, j: (0, j)),
               pl.BlockSpec((tf, D_MODEL), lambda i, j: (j, 0))]
    in_specs = [pl.BlockSpec((tm, D_MODEL), row),
                pl.BlockSpec((tm, D_MODEL), row) if normed_input else _resident((1, D_MODEL)),
                *w_specs]
    args = [x, norm, wg, wu, wd]
    if g_final is not None:
        in_specs.append(_resident((1, D_MODEL)))
        args.append(g_final)
    out_specs = [pl.BlockSpec((tm, D_MODEL), row)]
    out_shape = [jax.ShapeDtypeStruct((m, D_MODEL), F32)]
    if cast_weights:
        out_specs += w_specs
        out_shape += [jax.ShapeDtypeStruct(w.shape, BF16) for w in (wg, wu, wd)]
    outs = pl.pallas_call(
        functools.partial(_ffn_kernel, n_ff=n_ff, normed_input=normed_input,
                          has_final=g_final is not None, cast_weights=cast_weights),
        grid=(m // tm, n_ff),
        in_specs=in_specs,
        out_specs=out_specs,
        out_shape=out_shape,
        scratch_shapes=[] if normed_input else [pltpu.VMEM((tm, D_MODEL), BF16)],
        compiler_params=pltpu.CompilerParams(
            dimension_semantics=("parallel", "arbitrary"), vmem_limit_bytes=VMEM_LIMIT),
        name=name,
    )(*args)
    return outs if cast_weights else outs[0]


def _merge_kernel(x_ref, a_ref, c_ref, woa_ref, woc_ref, g_ref, x2_ref, xn_ref):
    x = x_ref[...] + (_dot(a_ref[...].astype(BF16), woa_ref[...]) +
                      _dot(c_ref[...].astype(BF16), woc_ref[...]))
    x2_ref[...] = x
    xn_ref[...] = _rms(x, g_ref[...]).astype(BF16)


def _merge(x, a, c, woa, woc, g, *, tm, name):
    m = x.shape[0]
    assert m % tm == 0
    row = lambda i: (i, 0)
    return pl.pallas_call(
        _merge_kernel,
        grid=(m // tm,),
        in_specs=[pl.BlockSpec((tm, D_MODEL), row), pl.BlockSpec((tm, ATTN_DIM), row),
                  pl.BlockSpec((tm, CONV_DIM), row), _resident((ATTN_DIM, D_MODEL)),
                  _resident((CONV_DIM, D_MODEL)), _resident((1, D_MODEL))],
        out_specs=[pl.BlockSpec((tm, D_MODEL), row), pl.BlockSpec((tm, D_MODEL), row)],
        out_shape=[jax.ShapeDtypeStruct((m, D_MODEL), F32),
                   jax.ShapeDtypeStruct((m, D_MODEL), BF16)],
        compiler_params=pltpu.CompilerParams(
            dimension_semantics=("parallel",), vmem_limit_bytes=VMEM_LIMIT),
        name=name,
    )(x, a, c, woa, woc, g)


Q0, K0, V0, B0, C0, H0 = 0, 1024, 1280, 1536, 2560, 3584


def _project_conv(x_ref, g_ref, w_ref):
    xn = _rms(x_ref[...], g_ref[...]).astype(BF16)
    gate_b = _dot(xn, w_ref[:, B0:C0])
    u = _dot(xn, w_ref[:, C0:H0]) * _dot(xn, w_ref[:, H0:IN_DIM])
    return xn, gate_b, u


def _project_qkv(xn, w_ref):
    q = _dot(xn, w_ref[:, Q0:K0]) * (HEAD_DIM ** -0.5)
    return q, _dot(xn, w_ref[:, K0:V0]), _dot(xn, w_ref[:, V0:B0])


def _proj_prompt_kernel(x_ref, g_ref, w_ref, wc_ref, gc_ref, umeta_ref,
                        q_ref, k_ref, v_ref, cn_ref, cst_ref, us_ref, *, tm, tiles_per_seq):
    t = pl.program_id(0) % tiles_per_seq

    @pl.when(t == 0)
    def _():
        us_ref[0:8, :] = umeta_ref[...]

    @pl.when(t != 0)
    def _():
        us_ref[0:8, :] = us_ref[tm:tm + 8, :]

    hm = tm // PROJ_ROW_CHUNKS
    for h in range(PROJ_ROW_CHUNKS):
        rows = slice(h * hm, (h + 1) * hm)
        xn, gate_b, u = _project_conv(x_ref.at[rows], g_ref, w_ref)
        us_ref[8 + h * hm:8 + (h + 1) * hm, :] = u
        y = (wc_ref[0:1, :] * us_ref[6 + h * hm:6 + (h + 1) * hm, :]
             + wc_ref[1:2, :] * us_ref[7 + h * hm:7 + (h + 1) * hm, :] + wc_ref[2:3, :] * u)
        cn_ref[rows, :] = _rms(gate_b * y, gc_ref[...]).astype(BF16)
        q, k, v = _project_qkv(xn, w_ref)
        q_ref[rows, :] = q.astype(BF16)
        k_ref[rows, :] = k
        v_ref[rows, :] = v
    cst_ref[0] = us_ref[tm + 6:tm + 8, :]


def _proj_prompt(x, g, w, wc, gc, umeta, *, tm, seq):
    m = x.shape[0]
    tiles_per_seq = seq // tm
    row = lambda i: (i, 0)
    return pl.pallas_call(
        functools.partial(_proj_prompt_kernel, tm=tm, tiles_per_seq=tiles_per_seq),
        grid=(m // tm,),
        in_specs=[pl.BlockSpec((tm, D_MODEL), row), _resident((1, D_MODEL)),
                  _resident((D_MODEL, IN_DIM)), _resident((3, CONV_DIM)),
                  _resident((1, CONV_DIM)), _resident((8, CONV_DIM))],
        out_specs=[pl.BlockSpec((tm, ATTN_DIM), row), pl.BlockSpec((tm, KV_DIM), row),
                   pl.BlockSpec((tm, KV_DIM), row), pl.BlockSpec((tm, CONV_DIM), row),
                   pl.BlockSpec((1, 2, CONV_DIM), lambda i: (i // tiles_per_seq, 0, 0))],
        out_shape=[jax.ShapeDtypeStruct((m, ATTN_DIM), BF16),
                   jax.ShapeDtypeStruct((m, KV_DIM), F32),
                   jax.ShapeDtypeStruct((m, KV_DIM), F32),
                   jax.ShapeDtypeStruct((m, CONV_DIM), BF16),
                   jax.ShapeDtypeStruct((m // seq, 2, CONV_DIM), F32)],
        scratch_shapes=[pltpu.VMEM((tm + 8, CONV_DIM), F32)],
        compiler_params=pltpu.CompilerParams(
            dimension_semantics=("arbitrary",), vmem_limit_bytes=VMEM_LIMIT),
        name="proj_prompt",
    )(x, g, w, wc, gc, umeta)


def _proj_tail_kernel(x_ref, g_ref, w_ref, wc_ref, gc_ref, h0_ref, h1_ref,
                      q_ref, k_ref, v_ref, u_ref, cn_ref):
    xn, gate_b, u = _project_conv(x_ref, g_ref, w_ref)
    q, k, v = _project_qkv(xn, w_ref)
    q_ref[...] = q
    k_ref[...] = k
    v_ref[...] = v
    u_ref[...] = u
    y = wc_ref[0:1, :] * h0_ref[...] + wc_ref[1:2, :] * h1_ref[...] + wc_ref[2:3, :] * u
    cn_ref[...] = _rms(gate_b * y, gc_ref[...]).astype(BF16)


def _proj_tail(x, g, w, wc, gc, h0, h1):
    m = x.shape[0]
    full = lambda shape: pl.BlockSpec(shape, lambda i: (0,) * len(shape))
    return pl.pallas_call(
        _proj_tail_kernel,
        grid=(1,),
        in_specs=[full((m, D_MODEL)), full((1, D_MODEL)), _resident((D_MODEL, IN_DIM)),
                  full((3, CONV_DIM)), full((1, CONV_DIM)), full((m, CONV_DIM)),
                  full((m, CONV_DIM))],
        out_specs=[full((m, ATTN_DIM)), full((m, KV_DIM)), full((m, KV_DIM)),
                   full((m, CONV_DIM)), full((m, CONV_DIM))],
        out_shape=[jax.ShapeDtypeStruct((m, ATTN_DIM), F32),
                   jax.ShapeDtypeStruct((m, KV_DIM), F32),
                   jax.ShapeDtypeStruct((m, KV_DIM), F32),
                   jax.ShapeDtypeStruct((m, CONV_DIM), F32),
                   jax.ShapeDtypeStruct((m, CONV_DIM), BF16)],
        compiler_params=pltpu.CompilerParams(
            dimension_semantics=("arbitrary",), vmem_limit_bytes=VMEM_LIMIT),
        name="proj_tail",
    )(x, g, w, wc, gc, h0, h1)


def _attn_prompt_kernel(sink_ref, q_ref, kp_ref, kc_ref, vp_ref, vc_ref, km_ref, vm_ref, ga_ref,
                        o_ref, *, nsub):
    first = pl.program_id(1) == 0
    kall = jnp.concatenate([jnp.where(first, km_ref[...], kp_ref[...]), kc_ref[...]], axis=0)
    vall = jnp.concatenate([jnp.where(first, vm_ref[...], vp_ref[...]), vc_ref[...]], axis=0)
    c = lax.broadcasted_iota(jnp.int32, (2 * WINDOW, WINDOW), 0)
    r = lax.broadcasted_iota(jnp.int32, (2 * WINDOW, WINDOW), 1)
    band = (c >= r) & (c <= r + WINDOW)
    band_first = band & ((c >= WINDOW - N_META) | jnp.logical_not(first))
    low = lax.broadcasted_iota(jnp.int32, kall.shape[:1] + (LANES,), 1) < HALF

    n_pairs = N_KV_HEADS // 2
    k_half, vt_half = [], []
    for pair in range(n_pairs):
        kcol = kall[:, pair * LANES:(pair + 1) * LANES]
        vcol = vall[:, pair * LANES:(pair + 1) * LANES]
        k_half.append((jnp.where(low, kcol, 0.0).astype(BF16),
                       jnp.where(low, 0.0, kcol).astype(BF16)))
        vt_half.append((jnp.where(low, vcol, 0.0).T.astype(BF16),
                        jnp.where(low, 0.0, vcol).T.astype(BF16)))

    for sb in range(nsub):
        rows = slice(sb * WINDOW, (sb + 1) * WINDOW)
        keys = slice(sb * WINDOW, (sb + 2) * WINDOW)
        valid = band_first if sb == 0 else band
        outs = [None] * (ATTN_DIM // LANES)
        ssq = jnp.zeros((1, WINDOW), F32)
        for pair in range(n_pairs):
            kst = jnp.concatenate([k_half[pair][0][keys], k_half[pair][1][keys]], axis=0)
            vst = jnp.concatenate([vt_half[pair][0][:, keys], vt_half[pair][1][:, keys]], axis=1)
            qst = jnp.concatenate(
                [q_ref[rows, (grp * n_pairs + pair) * LANES:(grp * n_pairs + pair + 1) * LANES]
                 for grp in range(GQA_GROUP)], axis=0)
            st_all = _dot_t(kst, qst)
            pt_rows = []
            for half in range(2):
                pt_cols = []
                for grp in range(GQA_GROUP):
                    sink = sink_ref[(2 * pair + half) * GQA_GROUP + grp]
                    st = st_all[half * 2 * WINDOW:(half + 1) * 2 * WINDOW,
                                grp * WINDOW:(grp + 1) * WINDOW]
                    st = jnp.where(valid, st, NEG)
                    mx = jnp.maximum(jnp.max(st, axis=0, keepdims=True), sink)
                    e = jnp.exp(st - mx)
                    den = jnp.sum(e, axis=0, keepdims=True) + jnp.exp(sink - mx)
                    pt_cols.append((e * (1.0 / den)).astype(BF16))
                pt_rows.append(jnp.concatenate(pt_cols, axis=1))
            ot_all = _dot(vst, jnp.concatenate(pt_rows, axis=0))
            for grp in range(GQA_GROUP):
                ot = ot_all[:, grp * WINDOW:(grp + 1) * WINDOW]
                outs[grp * n_pairs + pair] = ot
                ssq = ssq + jnp.sum(ot * ot, axis=0, keepdims=True)
        inv = lax.rsqrt(ssq * (1.0 / ATTN_DIM) + EPS)
        for col, ot in enumerate(outs):
            sl = slice(col * LANES, (col + 1) * LANES)
            o_ref[rows, sl] = ((ot * inv).T * ga_ref[:, sl]).astype(BF16)


def _attn_prompt(sinks, q, k, v, kmeta, vmeta, ga, *, batch, seq, nsub):
    nb = seq // (nsub * WINDOW)
    cur = lambda b, i: (b * nb + i, 0)
    prev = lambda b, i: ((b * nb + i) * nsub - jnp.minimum(i, 1), 0)
    const = lambda shape: pl.BlockSpec(shape, lambda b, i: (0,) * len(shape))
    return pl.pallas_call(
        functools.partial(_attn_prompt_kernel, nsub=nsub),
        grid=(batch, nb),
        in_specs=[pl.BlockSpec(memory_space=pltpu.SMEM),
                  pl.BlockSpec((nsub * WINDOW, ATTN_DIM), cur),
                  pl.BlockSpec((WINDOW, KV_DIM), prev), pl.BlockSpec((nsub * WINDOW, KV_DIM), cur),
                  pl.BlockSpec((WINDOW, KV_DIM), prev), pl.BlockSpec((nsub * WINDOW, KV_DIM), cur),
                  const((WINDOW, KV_DIM)), const((WINDOW, KV_DIM)), const((1, ATTN_DIM))],
        out_specs=pl.BlockSpec((nsub * WINDOW, ATTN_DIM), cur),
        out_shape=jax.ShapeDtypeStruct((batch * seq, ATTN_DIM), BF16),
        compiler_params=pltpu.CompilerParams(
            dimension_semantics=("parallel", "arbitrary"), vmem_limit_bytes=VMEM_LIMIT),
        name="attn_prompt",
    )(sinks, q, k, k, v, v, kmeta, vmeta, ga)


def _attn_sample_kernel(sink_ref, q_ref, kn_ref, vn_ref, ck_ref, cv_ref, ga_ref,
                        ok_ref, ov_ref, an_ref, *, bs):
    n_pairs = N_KV_HEADS // 2
    heads_per_col = 2 * GQA_GROUP
    last_row = lax.broadcasted_iota(jnp.int32, (WINDOW, KV_DIM), 0) == WINDOW - 1
    for s in range(bs):
        ok_ref[s] = jnp.where(last_row, kn_ref[s:s + 1, :], pltpu.roll(ck_ref[s], WINDOW - 1, 0))
        ov_ref[s] = jnp.where(last_row, vn_ref[s:s + 1, :], pltpu.roll(cv_ref[s], WINDOW - 1, 0))

    kstack = ck_ref[...].reshape(bs * WINDOW, KV_DIM).astype(BF16)
    vstack = cv_ref[...].reshape(bs * WINDOW, KV_DIM).astype(BF16)
    n_rows = heads_per_col * bs
    row = lax.broadcasted_iota(jnp.int32, (n_rows, bs * WINDOW), 0)
    lane = lax.broadcasted_iota(jnp.int32, (n_rows, bs * WINDOW), 1)
    log2 = lambda n: n.bit_length() - 1
    assert bs == 1 << log2(bs) and WINDOW == 1 << log2(WINDOW)
    same_seq = (row & (bs - 1)) == (lane >> log2(WINDOW))
    head_row = lax.broadcasted_iota(jnp.int32, (n_rows, 1), 0) >> log2(bs)
    low = lax.broadcasted_iota(jnp.int32, (bs, LANES), 1) < HALF

    cols = [None] * (ATTN_DIM // LANES)
    for pair in range(n_pairs):
        sl = slice(pair * LANES, (pair + 1) * LANES)
        pieces = []
        sk = jnp.zeros((n_rows, 1), F32)
        for grp in range(GQA_GROUP):
            col = grp * n_pairs + pair
            qcol = q_ref[:, col * LANES:(col + 1) * LANES]
            pieces += [jnp.where(low, qcol, 0.0), jnp.where(low, 0.0, qcol)]
            for half in range(2):
                sink = sink_ref[(2 * pair + half) * GQA_GROUP + grp]
                sk = jnp.where(head_row == 2 * grp + half, sink, sk)
        lq = jnp.concatenate(pieces, axis=0).astype(BF16)
        s = jnp.where(same_seq, _dot_t(lq, kstack[:, sl]), NEG)
        k_new = jnp.concatenate([kn_ref[:, sl]] * heads_per_col, axis=0).astype(BF16)
        v_new = jnp.concatenate([vn_ref[:, sl]] * heads_per_col, axis=0).astype(BF16)
        s_new = jnp.sum(lq.astype(F32) * k_new.astype(F32), axis=-1, keepdims=True)
        mx = jnp.maximum(jnp.maximum(jnp.max(s, axis=-1, keepdims=True), s_new), sk)
        e = jnp.exp(s - mx)
        e_new = jnp.exp(s_new - mx)
        rden = 1.0 / (jnp.sum(e, axis=-1, keepdims=True) + e_new + jnp.exp(sk - mx))
        p = (e * rden).astype(BF16)
        p_new = (e_new * rden).astype(BF16).astype(F32)
        o = _dot(p, vstack[:, sl]) + p_new * v_new.astype(F32)
        for grp in range(GQA_GROUP):
            lo = o[(2 * grp) * bs:(2 * grp + 1) * bs]
            hi = o[(2 * grp + 1) * bs:(2 * grp + 2) * bs]
            cols[grp * n_pairs + pair] = jnp.where(low, lo, hi)
    an_ref[...] = _rms(jnp.concatenate(cols, axis=1), ga_ref[...])


def _attn_sample(sinks, q, kn, vn, ck, cv, ga, *, bs):
    nseq = q.shape[0]
    row = lambda i: (i, 0)
    blk3 = pl.BlockSpec((bs, WINDOW, KV_DIM), lambda i: (i, 0, 0))
    return pl.pallas_call(
        functools.partial(_attn_sample_kernel, bs=bs),
        grid=(nseq // bs,),
        in_specs=[pl.BlockSpec(memory_space=pltpu.SMEM),
                  pl.BlockSpec((bs, ATTN_DIM), row), pl.BlockSpec((bs, KV_DIM), row),
                  pl.BlockSpec((bs, KV_DIM), row), blk3, blk3,
                  pl.BlockSpec((1, ATTN_DIM), lambda i: (0, 0))],
        out_specs=[blk3, blk3, pl.BlockSpec((bs, ATTN_DIM), row)],
        out_shape=[jax.ShapeDtypeStruct(ck.shape, F32), jax.ShapeDtypeStruct(cv.shape, F32),
                   jax.ShapeDtypeStruct((nseq, ATTN_DIM), F32)],
        compiler_params=pltpu.CompilerParams(
            dimension_semantics=("parallel",), vmem_limit_bytes=VMEM_LIMIT),
        name="attn_sample",
    )(sinks, q, kn, vn, ck, cv, ga)


def _group_major(w, axis):
    shape = w.shape
    w = w.reshape(shape[:axis] + (N_KV_HEADS, GQA_GROUP, HEAD_DIM) + shape[axis + 1:])
    return jnp.swapaxes(w, axis, axis + 1).reshape(shape)


def kernel(x_prompt, x_sample, cache_swa_k, cache_swa_v, state_conv, meta_tokens, g_ffn1, w1_gate, w1_up, w1_down, g_mix, w_in, attn_sinks, w_conv, g_attn_out, g_conv_out, w_out, g_ffn2, w2_gate, w2_up, w2_down, g_final):
    assert g_ffn1.shape[0] == 1, "one layer"
    batch, seq, _ = x_prompt.shape
    n_dec = x_sample.shape[0]
    assert x_sample.shape[1] == 1 and cache_swa_k.shape[2] == WINDOW

    row2d = lambda a: a.reshape(1, -1).astype(F32)
    w_in_b = jnp.concatenate([_group_major(w_in[0, :, :ATTN_DIM], 1), w_in[0, :, ATTN_DIM:]],
                             axis=1).astype(BF16)
    wo_a = _group_major(w_out[0, :ATTN_DIM], 0).astype(BF16)
    wo_c = w_out[0, ATTN_DIM:].astype(BF16)
    ga = row2d(_group_major(g_attn_out[0], 0))
    gc = row2d(g_conv_out[0])
    sinks = attn_sinks[0].astype(F32)
    wc = w_conv[0].astype(F32)

    xp = x_prompt.reshape(batch * seq, D_MODEL)
    xt = jnp.concatenate([x_sample[:, 0, :], meta_tokens.astype(x_prompt.dtype)], axis=0)
    n_tail = xt.shape[0]

    xt1, w1g, w1u, w1d = _ffn(xt, row2d(g_ffn1[0]), w1_gate[0], w1_up[0], w1_down[0],
                              tm=n_tail, tf=FFN_TF, name="ffn1_tail")
    xp1 = _ffn(xp, row2d(g_ffn1[0]), w1g, w1u, w1d, tm=FFN_TM, tf=FFN_TF, name="ffn1_prompt")

    st = state_conv[0]
    zpad = jnp.zeros((N_META, CONV_DIM), F32)
    h0 = jnp.concatenate([st[:, 0, :], zpad], axis=0)
    h1 = jnp.concatenate([st[:, 1, :], zpad], axis=0)
    qt, kt, vt, ut, cnt = _proj_tail(xt1, row2d(g_mix[0]), w_in_b, wc, gc, h0, h1)
    kmeta = jnp.pad(kt[n_dec:], ((WINDOW - N_META, 0), (0, 0)))
    vmeta = jnp.pad(vt[n_dec:], ((WINDOW - N_META, 0), (0, 0)))
    umeta = ut[n_tail - 8:]

    qp, kp, vp, cnp, cstate = _proj_prompt(xp1, row2d(g_mix[0]), w_in_b, wc, gc, umeta,
                                           tm=PROJ_TM, seq=seq)
    anp = _attn_prompt(sinks, qp, kp, vp, kmeta, vmeta, ga, batch=batch, seq=seq,
                       nsub=ATTN_NSUB)

    ck = cache_swa_k[0].reshape(n_dec, WINDOW, KV_DIM)
    cv = cache_swa_v[0].reshape(n_dec, WINDOW, KV_DIM)
    nk, nv, ans = _attn_sample(sinks, qt[:n_dec], kt[:n_dec], vt[:n_dec], ck, cv, ga,
                               bs=SAMPLE_BS)

    xp2, xpn = _merge(xp1, anp, cnp, wo_a, wo_c, row2d(g_ffn2[0]), tm=MERGE_TM,
                      name="merge_prompt")
    xs2, xsn = _merge(xt1[:n_dec], ans, cnt[:n_dec], wo_a, wo_c, row2d(g_ffn2[0]), tm=n_dec,
                      name="merge_tail")
    ys, w2g, w2u, w2d = _ffn(xs2, xsn, w2_gate[0], w2_up[0], w2_down[0], tm=n_dec, tf=FFN_TF,
                             g_final=row2d(g_final), name="ffn2_tail")
    yp = _ffn(xp2, xpn, w2g, w2u, w2d, tm=FFN_TM, tf=FFN_TF, g_final=row2d(g_final),
              name="ffn2_prompt")

    kv_shape = (1, batch, WINDOW, N_KV_HEADS, HEAD_DIM)
    new_k_prompt = kp.reshape(batch, seq, KV_DIM)[:, seq - WINDOW:].reshape(kv_shape)
    new_v_prompt = vp.reshape(batch, seq, KV_DIM)[:, seq - WINDOW:].reshape(kv_shape)
    new_conv_sample = jnp.stack([st[:, 1, :], ut[:n_dec]], axis=1)[None]
    return (yp.reshape(batch, seq, D_MODEL), ys.reshape(n_dec, 1, D_MODEL),
            new_k_prompt, new_v_prompt, cstate[None],
            nk.reshape(cache_swa_k.shape), nv.reshape(cache_swa_v.shape), new_conv_sample)
```

```python
import functools

import jax
import jax.numpy as jnp
from jax import lax
from jax.experimental import pallas as pl
from jax.experimental.pallas import tpu as pltpu

D_MODEL = 2048
D_FF = 5632
N_META = 16
ATTN_DIM = 1024
CONV_DIM = 1024
HEAD_DIM = 64
N_HEADS = 16
N_KV_HEADS = 4
GQA_GROUP = 4
KV_DIM = 256
WINDOW = 128
IN_DIM = ATTN_DIM + 2 * KV_DIM + 3 * CONV_DIM
EPS = 1e-5
NEG = -1e30

LANES = 128
HALF = LANES // 2
VMEM_LIMIT = 60 * 1024 * 1024
FFN_TM, FFN_TF = 1024, 512
MERGE_TM = 512
PROJ_TM = 512
PROJ_ROW_CHUNKS = 2
CAST_TF = 256
ATTN_NSUB = 2
SAMPLE_BS = 16

F32 = jnp.float32
BF16 = jnp.bfloat16


def _rms(x, g):
    return x * lax.rsqrt(jnp.mean(x * x, axis=-1, keepdims=True) + EPS) * g


def _dot(a, b):
    return jnp.dot(a, b, preferred_element_type=F32)


def _dot_t(a, b):
    return lax.dot_general(a, b, (((1,), (1,)), ((), ())), preferred_element_type=F32)


def _resident(shape):
    return pl.BlockSpec(shape, lambda *_: (0,) * len(shape), pipeline_mode=pl.Buffered(1))


def _ffn_kernel(*refs, n_ff, normed_input, has_final, cast_weights):
    refs = list(refs)
    x_ref, norm_ref, wg_ref, wu_ref, wd_ref = refs[:5]
    refs = refs[5:]
    if has_final:
        gf_ref = refs.pop(0)
    out_ref = refs.pop(0)
    if cast_weights:
        wgb_ref, wub_ref, wdb_ref = refs[:3]
        refs = refs[3:]
    xn_ref = norm_ref if normed_input else refs.pop(0)
    j = pl.program_id(1)

    @pl.when(j == 0)
    def _():
        x = x_ref[...]
        out_ref[...] = x
        if not normed_input:
            xn_ref[...] = _rms(x, norm_ref[...]).astype(BF16)

    if cast_weights:
        wgb_ref[...] = wg_ref[...].astype(BF16)
        wub_ref[...] = wu_ref[...].astype(BF16)
        wdb_ref[...] = wd_ref[...].astype(BF16)
        wg_ref, wu_ref, wd_ref = wgb_ref, wub_ref, wdb_ref
    xn = xn_ref[...]
    gate = _dot(xn, wg_ref[...])
    up = _dot(xn, wu_ref[...])
    h = (gate * jax.nn.sigmoid(gate) * up * 0.5).astype(BF16)
    out_ref[...] += _dot(h, wd_ref[...])

    if has_final:
        @pl.when(j == n_ff - 1)
        def _():
            out_ref[...] = _rms(out_ref[...], gf_ref[...])


def _ffn(x, norm, wg, wu, wd, *, tm, tf, g_final=None, name):
    m = x.shape[0]
    assert m % tm == 0 and D_FF % tf == 0
    n_ff = D_FF // tf
    normed_input = norm.shape[0] == m
    cast_weights = wg.dtype == F32
    assert not cast_weights or m == tm
    row = lambda i, j: (i, 0)
    w_specs = [pl.BlockSpec((D_MODEL, tf), lambda i, j: (0, j)),
               pl.BlockSpec((D_MODEL, tf), lambda i, j: (0, j)),
               pl.BlockSpec((tf, D_MODEL), lambda i, j: (j, 0))]
    in_specs = [pl.BlockSpec((tm, D_MODEL), row),
                pl.BlockSpec((tm, D_MODEL), row) if normed_input else _resident((1, D_MODEL)),
                *w_specs]
    args = [x, norm, wg, wu, wd]
    if g_final is not None:
        in_specs.append(_resident((1, D_MODEL)))
        args.append(g_final)
    out_specs = [pl.BlockSpec((tm, D_MODEL), row)]
    out_shape = [jax.ShapeDtypeStruct((m, D_MODEL), F32)]
    if cast_weights:
        out_specs += w_specs
        out_shape += [jax.ShapeDtypeStruct(w.shape, BF16) for w in (wg, wu, wd)]
    outs = pl.pallas_call(
        functools.partial(_ffn_kernel, n_ff=n_ff, normed_input=normed_input,
                          has_final=g_final is not None, cast_weights=cast_weights),
        grid=(m // tm, n_ff),
        in_specs=in_specs,
        out_specs=out_specs,
        out_shape=out_shape,
        scratch_shapes=[] if normed_input else [pltpu.VMEM((tm, D_MODEL), BF16)],
        compiler_params=pltpu.CompilerParams(
            dimension_semantics=("parallel", "arbitrary"), vmem_limit_bytes=VMEM_LIMIT),
        name=name,
    )(*args)
    return outs if cast_weights else outs[0]


def _merge_kernel(x_ref, a_ref, c_ref, woa_ref, woc_ref, g_ref, x2_ref, xn_ref):
    x = x_ref[...] + (_dot(a_ref[...].astype(BF16), woa_ref[...]) +
                      _dot(c_ref[...].astype(BF16), woc_ref[...]))
    x2_ref[...] = x
    xn_ref[...] = _rms(x, g_ref[...]).astype(BF16)


def _merge(x, a, c, woa, woc, g, *, tm, name):
    m = x.shape[0]
    assert m % tm == 0
    row = lambda i: (i, 0)
    return pl.pallas_call(
        _merge_kernel,
        grid=(m // tm,),
        in_specs=[pl.BlockSpec((tm, D_MODEL), row), pl.BlockSpec((tm, ATTN_DIM), row),
                  pl.BlockSpec((tm, CONV_DIM), row), _resident((ATTN_DIM, D_MODEL)),
                  _resident((CONV_DIM, D_MODEL)), _resident((1, D_MODEL))],
        out_specs=[pl.BlockSpec((tm, D_MODEL), row), pl.BlockSpec((tm, D_MODEL), row)],
        out_shape=[jax.ShapeDtypeStruct((m, D_MODEL), F32),
                   jax.ShapeDtypeStruct((m, D_MODEL), BF16)],
        compiler_params=pltpu.CompilerParams(
            dimension_semantics=("parallel",), vmem_limit_bytes=VMEM_LIMIT),
        name=name,
    )(x, a, c, woa, woc, g)


K0, V0, B0, C0, H0, R_END = 0, 256, 512, 1536, 2560, 3584


def _project_conv(x_ref, g_ref, wr_ref):
    xn = _rms(x_ref[...], g_ref[...]).astype(BF16)
    gate_b = _dot(xn, wr_ref[:, B0:C0])
    u = _dot(xn, wr_ref[:, C0:H0]) * _dot(xn, wr_ref[:, H0:R_END])
    return xn, gate_b, u


def _project_qkv(xn, wq_ref, wr_ref):
    q = _dot(xn, wq_ref[...]) * (HEAD_DIM ** -0.5)
    return q, _dot(xn, wr_ref[:, K0:V0]), _dot(xn, wr_ref[:, V0:B0])


def _proj_prompt_kernel(x_ref, g_ref, wq_ref, wr_ref, wc_ref, gc_ref, umeta_ref,
                        fg_ref, fu_ref, fd_ref,
                        q_ref, k_ref, v_ref, cn_ref, cst_ref, bg_ref, bu_ref, bd_ref,
                        us_ref, *, tm, tiles_per_seq):
    t = pl.program_id(0) % tiles_per_seq

    @pl.when(t == 0)
    def _():
        us_ref[0:8, :] = umeta_ref[...]

    @pl.when(t != 0)
    def _():
        us_ref[0:8, :] = us_ref[tm:tm + 8, :]

    hm = tm // PROJ_ROW_CHUNKS
    casts = ((fg_ref, bg_ref), (fu_ref, bu_ref), (fd_ref, bd_ref))
    for h in range(PROJ_ROW_CHUNKS):
        rows = slice(h * hm, (h + 1) * hm)
        for src, dst in casts[h::PROJ_ROW_CHUNKS]:
            dst[...] = src[...].astype(BF16)
        xn, gate_b, u = _project_conv(x_ref.at[rows], g_ref, wr_ref)
        us_ref[8 + h * hm:8 + (h + 1) * hm, :] = u
        y = (wc_ref[0:1, :] * us_ref[6 + h * hm:6 + (h + 1) * hm, :]
             + wc_ref[1:2, :] * us_ref[7 + h * hm:7 + (h + 1) * hm, :] + wc_ref[2:3, :] * u)
        cn_ref[rows, :] = _rms(gate_b * y, gc_ref[...]).astype(BF16)
        q, k, v = _project_qkv(xn, wq_ref, wr_ref)
        q_ref[rows, :] = q.astype(BF16)
        k_ref[rows, :] = k
        v_ref[rows, :] = v
    cst_ref[0] = us_ref[tm + 6:tm + 8, :]


def _proj_prompt(x, g, wq, wr, wc, gc, umeta, ffn_w, *, tm, seq):
    m = x.shape[0]
    tiles_per_seq = seq // tm
    n_steps = m // tm
    n_cast = D_FF // CAST_TF
    assert D_FF % CAST_TF == 0 and n_cast <= n_steps
    row = lambda i: (i, 0)
    cast_cols = pl.BlockSpec((D_MODEL, CAST_TF), lambda i: (0, jnp.minimum(i, n_cast - 1)))
    cast_rows = pl.BlockSpec((CAST_TF, D_MODEL), lambda i: (jnp.minimum(i, n_cast - 1), 0))
    return pl.pallas_call(
        functools.partial(_proj_prompt_kernel, tm=tm, tiles_per_seq=tiles_per_seq),
        grid=(n_steps,),
        in_specs=[pl.BlockSpec((tm, D_MODEL), row), _resident((1, D_MODEL)),
                  _resident((D_MODEL, ATTN_DIM)), _resident((D_MODEL, R_END)),
                  _resident((3, CONV_DIM)), _resident((1, CONV_DIM)), _resident((8, CONV_DIM)),
                  cast_cols, cast_cols, cast_rows],
        out_specs=[pl.BlockSpec((tm, ATTN_DIM), row), pl.BlockSpec((tm, KV_DIM), row),
                   pl.BlockSpec((tm, KV_DIM), row), pl.BlockSpec((tm, CONV_DIM), row),
                   pl.BlockSpec((1, 2, CONV_DIM), lambda i: (i // tiles_per_seq, 0, 0)),
                   cast_cols, cast_cols, cast_rows],
        out_shape=[jax.ShapeDtypeStruct((m, ATTN_DIM), BF16),
                   jax.ShapeDtypeStruct((m, KV_DIM), F32),
                   jax.ShapeDtypeStruct((m, KV_DIM), F32),
                   jax.ShapeDtypeStruct((m, CONV_DIM), BF16),
                   jax.ShapeDtypeStruct((m // seq, 2, CONV_DIM), F32)]
                  + [jax.ShapeDtypeStruct(w.shape, BF16) for w in ffn_w],
        scratch_shapes=[pltpu.VMEM((tm + 8, CONV_DIM), F32)],
        compiler_params=pltpu.CompilerParams(
            dimension_semantics=("arbitrary",), vmem_limit_bytes=VMEM_LIMIT),
        name="proj_prompt",
    )(x, g, wq, wr, wc, gc, umeta, *ffn_w)


def _proj_tail_kernel(x_ref, g_ref, wq_ref, wr_ref, wc_ref, gc_ref, h0_ref, h1_ref,
                      q_ref, k_ref, v_ref, u_ref, cn_ref):
    xn, gate_b, u = _project_conv(x_ref, g_ref, wr_ref)
    q, k, v = _project_qkv(xn, wq_ref, wr_ref)
    q_ref[...] = q
    k_ref[...] = k
    v_ref[...] = v
    u_ref[...] = u
    y = wc_ref[0:1, :] * h0_ref[...] + wc_ref[1:2, :] * h1_ref[...] + wc_ref[2:3, :] * u
    cn_ref[...] = _rms(gate_b * y, gc_ref[...]).astype(BF16)


def _proj_tail(x, g, wq, wr, wc, gc, h0, h1):
    m = x.shape[0]
    full = lambda shape: pl.BlockSpec(shape, lambda i: (0,) * len(shape))
    return pl.pallas_call(
        _proj_tail_kernel,
        grid=(1,),
        in_specs=[full((m, D_MODEL)), full((1, D_MODEL)), _resident((D_MODEL, ATTN_DIM)),
                  _resident((D_MODEL, R_END)),
                  full((3, CONV_DIM)), full((1, CONV_DIM)), full((m, CONV_DIM)),
                  full((m, CONV_DIM))],
        out_specs=[full((m, ATTN_DIM)), full((m, KV_DIM)), full((m, KV_DIM)),
                   full((m, CONV_DIM)), full((m, CONV_DIM))],
        out_shape=[jax.ShapeDtypeStruct((m, ATTN_DIM), F32),
                   jax.ShapeDtypeStruct((m, KV_DIM), F32),
                   jax.ShapeDtypeStruct((m, KV_DIM), F32),
                   jax.ShapeDtypeStruct((m, CONV_DIM), F32),
                   jax.ShapeDtypeStruct((m, CONV_DIM), BF16)],
        compiler_params=pltpu.CompilerParams(
            dimension_semantics=("arbitrary",), vmem_limit_bytes=VMEM_LIMIT),
        name="proj_tail",
    )(x, g, wq, wr, wc, gc, h0, h1)


def _attn_prompt_kernel(sink_ref, q_ref, kp_ref, kc_ref, vp_ref, vc_ref, km_ref, vm_ref, ga_ref,
                        o_ref, *, nsub):
    first = pl.program_id(1) == 0
    kall = jnp.concatenate([jnp.where(first, km_ref[...], kp_ref[...]), kc_ref[...]], axis=0)
    vall = jnp.concatenate([jnp.where(first, vm_ref[...], vp_ref[...]), vc_ref[...]], axis=0)
    c = lax.broadcasted_iota(jnp.int32, (2 * WINDOW, WINDOW), 0)
    r = lax.broadcasted_iota(jnp.int32, (2 * WINDOW, WINDOW), 1)
    band = (c >= r) & (c <= r + WINDOW)
    band_first = band & ((c >= WINDOW - N_META) | jnp.logical_not(first))
    low = lax.broadcasted_iota(jnp.int32, kall.shape[:1] + (LANES,), 1) < HALF

    n_pairs = N_KV_HEADS // 2
    k_half, vt_half = [], []
    for pair in range(n_pairs):
        kcol = kall[:, pair * LANES:(pair + 1) * LANES]
        vcol = vall[:, pair * LANES:(pair + 1) * LANES]
        k_half.append((jnp.where(low, kcol, 0.0).astype(BF16),
                       jnp.where(low, 0.0, kcol).astype(BF16)))
        vt_half.append((jnp.where(low, vcol, 0.0).T.astype(BF16),
                        jnp.where(low, 0.0, vcol).T.astype(BF16)))

    for sb in range(nsub):
        rows = slice(sb * WINDOW, (sb + 1) * WINDOW)
        keys = slice(sb * WINDOW, (sb + 2) * WINDOW)
        valid = band_first if sb == 0 else band
        outs = [None] * (ATTN_DIM // LANES)
        ssq = jnp.zeros((1, WINDOW), F32)
        for pair in range(n_pairs):
            kst = jnp.concatenate([k_half[pair][0][keys], k_half[pair][1][keys]], axis=0)
            vst = jnp.concatenate([vt_half[pair][0][:, keys], vt_half[pair][1][:, keys]], axis=1)
            qst = jnp.concatenate(
                [q_ref[rows, (grp * n_pairs + pair) * LANES:(grp * n_pairs + pair + 1) * LANES]
                 for grp in range(GQA_GROUP)], axis=0)
            st_all = _dot_t(kst, qst)
            pt_rows = []
            for half in range(2):
                pt_cols = []
                for grp in range(GQA_GROUP):
                    sink = sink_ref[(2 * pair + half) * GQA_GROUP + grp]
                    st = st_all[half * 2 * WINDOW:(half + 1) * 2 * WINDOW,
                                grp * WINDOW:(grp + 1) * WINDOW]
                    st = jnp.where(valid, st, NEG)
                    mx = jnp.maximum(jnp.max(st, axis=0, keepdims=True), sink)
                    e = jnp.exp(st - mx)
                    den = jnp.sum(e, axis=0, keepdims=True) + jnp.exp(sink - mx)
                    pt_cols.append((e * (1.0 / den)).astype(BF16))
                pt_rows.append(jnp.concatenate(pt_cols, axis=1))
            ot_all = _dot(vst, jnp.concatenate(pt_rows, axis=0))
            for grp in range(GQA_GROUP):
                ot = ot_all[:, grp * WINDOW:(grp + 1) * WINDOW]
                outs[grp * n_pairs + pair] = ot
                ssq = ssq + jnp.sum(ot * ot, axis=0, keepdims=True)
        inv = lax.rsqrt(ssq * (1.0 / ATTN_DIM) + EPS)
        for col, ot in enumerate(outs):
            sl = slice(col * LANES, (col + 1) * LANES)
            o_ref[rows, sl] = ((ot * inv).T * ga_ref[:, sl]).astype(BF16)


def _attn_prompt(sinks, q, k, v, kmeta, vmeta, ga, *, batch, seq, nsub):
    nb = seq // (nsub * WINDOW)
    cur = lambda b, i: (b * nb + i, 0)
    prev = lambda b, i: ((b * nb + i) * nsub - jnp.minimum(i, 1), 0)
    const = lambda shape: pl.BlockSpec(shape, lambda b, i: (0,) * len(shape))
    return pl.pallas_call(
        functools.partial(_attn_prompt_kernel, nsub=nsub),
        grid=(batch, nb),
        in_specs=[pl.BlockSpec(memory_space=pltpu.SMEM),
                  pl.BlockSpec((nsub * WINDOW, ATTN_DIM), cur),
                  pl.BlockSpec((WINDOW, KV_DIM), prev), pl.BlockSpec((nsub * WINDOW, KV_DIM), cur),
                  pl.BlockSpec((WINDOW, KV_DIM), prev), pl.BlockSpec((nsub * WINDOW, KV_DIM), cur),
                  const((WINDOW, KV_DIM)), const((WINDOW, KV_DIM)), const((1, ATTN_DIM))],
        out_specs=pl.BlockSpec((nsub * WINDOW, ATTN_DIM), cur),
        out_shape=jax.ShapeDtypeStruct((batch * seq, ATTN_DIM), BF16),
        compiler_params=pltpu.CompilerParams(
            dimension_semantics=("parallel", "arbitrary"), vmem_limit_bytes=VMEM_LIMIT),
        name="attn_prompt",
    )(sinks, q, k, k, v, v, kmeta, vmeta, ga)


def _attn_sample_kernel(sink_ref, q_ref, kn_ref, vn_ref, ck_ref, cv_ref, ga_ref,
                        ok_ref, ov_ref, an_ref, *, bs):
    n_pairs = N_KV_HEADS // 2
    heads_per_col = 2 * GQA_GROUP
    last_key = lax.broadcasted_iota(jnp.int32, (KV_DIM, WINDOW), 1) == WINDOW - 1
    pad = jnp.zeros((WINDOW - bs, KV_DIM), F32)
    knt = jnp.concatenate([kn_ref[...], pad], axis=0).T
    vnt = jnp.concatenate([vn_ref[...], pad], axis=0).T
    for s in range(bs):
        ok_ref[s] = jnp.where(last_key, knt[:, s:s + 1], pltpu.roll(ck_ref[s], WINDOW - 1, 1))
        ov_ref[s] = jnp.where(last_key, vnt[:, s:s + 1], pltpu.roll(cv_ref[s], WINDOW - 1, 1))

    n_rows = heads_per_col * bs
    row = lax.broadcasted_iota(jnp.int32, (n_rows, bs * WINDOW), 0)
    lane = lax.broadcasted_iota(jnp.int32, (n_rows, bs * WINDOW), 1)
    log2 = lambda n: n.bit_length() - 1
    assert bs == 1 << log2(bs) and WINDOW == 1 << log2(WINDOW)
    same_seq = (row & (bs - 1)) == (lane >> log2(WINDOW))
    head_row = lax.broadcasted_iota(jnp.int32, (n_rows, 1), 0) >> log2(bs)
    low = lax.broadcasted_iota(jnp.int32, (bs, LANES), 1) < HALF

    cols = [None] * (ATTN_DIM // LANES)
    for pair in range(n_pairs):
        sl = slice(pair * LANES, (pair + 1) * LANES)
        pieces = []
        sk = jnp.zeros((n_rows, 1), F32)
        for grp in range(GQA_GROUP):
            col = grp * n_pairs + pair
            qcol = q_ref[:, col * LANES:(col + 1) * LANES]
            pieces += [jnp.where(low, qcol, 0.0), jnp.where(low, 0.0, qcol)]
            for half in range(2):
                sink = sink_ref[(2 * pair + half) * GQA_GROUP + grp]
                sk = jnp.where(head_row == 2 * grp + half, sink, sk)
        lq = jnp.concatenate(pieces, axis=0).astype(BF16)
        kstack = jnp.concatenate([ck_ref[s, sl, :] for s in range(bs)], axis=1).astype(BF16)
        vstack = jnp.concatenate([cv_ref[s, sl, :] for s in range(bs)], axis=1).astype(BF16)
        s = jnp.where(same_seq, _dot(lq, kstack), NEG)
        k_new = jnp.concatenate([kn_ref[:, sl]] * heads_per_col, axis=0).astype(BF16)
        v_new = jnp.concatenate([vn_ref[:, sl]] * heads_per_col, axis=0).astype(BF16)
        s_new = jnp.sum(lq.astype(F32) * k_new.astype(F32), axis=-1, keepdims=True)
        mx = jnp.maximum(jnp.maximum(jnp.max(s, axis=-1, keepdims=True), s_new), sk)
        e = jnp.exp(s - mx)
        e_new = jnp.exp(s_new - mx)
        rden = 1.0 / (jnp.sum(e, axis=-1, keepdims=True) + e_new + jnp.exp(sk - mx))
        p = (e * rden).astype(BF16)
        p_new = (e_new * rden).astype(BF16).astype(F32)
        o = _dot_t(p, vstack) + p_new * v_new.astype(F32)
        for grp in range(GQA_GROUP):
            lo = o[(2 * grp) * bs:(2 * grp + 1) * bs]
            hi = o[(2 * grp + 1) * bs:(2 * grp + 2) * bs]
            cols[grp * n_pairs + pair] = jnp.where(low, lo, hi)
    an_ref[...] = _rms(jnp.concatenate(cols, axis=1), ga_ref[...])


def _attn_sample(sinks, q, kn, vn, ck, cv, ga, *, bs):
    nseq = q.shape[0]
    row = lambda i: (i, 0)
    blk3 = pl.BlockSpec((bs, KV_DIM, WINDOW), lambda i: (i, 0, 0))
    return pl.pallas_call(
        functools.partial(_attn_sample_kernel, bs=bs),
        grid=(nseq // bs,),
        in_specs=[pl.BlockSpec(memory_space=pltpu.SMEM),
                  pl.BlockSpec((bs, ATTN_DIM), row), pl.BlockSpec((bs, KV_DIM), row),
                  pl.BlockSpec((bs, KV_DIM), row), blk3, blk3,
                  pl.BlockSpec((1, ATTN_DIM), lambda i: (0, 0))],
        out_specs=[blk3, blk3, pl.BlockSpec((bs, ATTN_DIM), row)],
        out_shape=[jax.ShapeDtypeStruct(ck.shape, F32), jax.ShapeDtypeStruct(cv.shape, F32),
                   jax.ShapeDtypeStruct((nseq, ATTN_DIM), F32)],
        compiler_params=pltpu.CompilerParams(
            dimension_semantics=("parallel",), vmem_limit_bytes=VMEM_LIMIT),
        name="attn_sample",
    )(sinks, q, kn, vn, ck, cv, ga)


def _group_major(w, axis):
    shape = w.shape
    w = w.reshape(shape[:axis] + (N_KV_HEADS, GQA_GROUP, HEAD_DIM) + shape[axis + 1:])
    return jnp.swapaxes(w, axis, axis + 1).reshape(shape)


def kernel(x_prompt, x_sample, cache_swa_k, cache_swa_v, state_conv, meta_tokens, g_ffn1, w1_gate, w1_up, w1_down, g_mix, w_in, attn_sinks, w_conv, g_attn_out, g_conv_out, w_out, g_ffn2, w2_gate, w2_up, w2_down, g_final):
    assert g_ffn1.shape[0] == 1, "one layer"
    batch, seq, _ = x_prompt.shape
    n_dec = x_sample.shape[0]
    assert x_sample.shape[1] == 1 and cache_swa_k.shape[2] == WINDOW

    row2d = lambda a: a.reshape(1, -1).astype(F32)
    wq = _group_major(w_in[0, :, :ATTN_DIM], 1).astype(BF16)
    wr = w_in[0, :, ATTN_DIM:].astype(BF16)
    wo_a = _group_major(w_out[0, :ATTN_DIM], 0).astype(BF16)
    wo_c = w_out[0, ATTN_DIM:].astype(BF16)
    ga = row2d(_group_major(g_attn_out[0], 0))
    gc = row2d(g_conv_out[0])
    sinks = attn_sinks[0].astype(F32)
    wc = w_conv[0].astype(F32)

    xp = x_prompt.reshape(batch * seq, D_MODEL)
    xt = jnp.concatenate([x_sample[:, 0, :], meta_tokens.astype(x_prompt.dtype)], axis=0)
    n_tail = xt.shape[0]

    xt1, w1g, w1u, w1d = _ffn(xt, row2d(g_ffn1[0]), w1_gate[0], w1_up[0], w1_down[0],
                              tm=n_tail, tf=FFN_TF, name="ffn1_tail")
    xp1 = _ffn(xp, row2d(g_ffn1[0]), w1g, w1u, w1d, tm=FFN_TM, tf=FFN_TF, name="ffn1_prompt")

    st = state_conv[0]
    zpad = jnp.zeros((N_META, CONV_DIM), F32)
    h0 = jnp.concatenate([st[:, 0, :], zpad], axis=0)
    h1 = jnp.concatenate([st[:, 1, :], zpad], axis=0)
    qt, kt, vt, ut, cnt = _proj_tail(xt1, row2d(g_mix[0]), wq, wr, wc, gc, h0, h1)
    kmeta = jnp.pad(kt[n_dec:], ((WINDOW - N_META, 0), (0, 0)))
    vmeta = jnp.pad(vt[n_dec:], ((WINDOW - N_META, 0), (0, 0)))
    umeta = ut[n_tail - 8:]

    qp, kp, vp, cnp, cstate, w2g, w2u, w2d = _proj_prompt(
        xp1, row2d(g_mix[0]), wq, wr, wc, gc, umeta, (w2_gate[0], w2_up[0], w2_down[0]),
        tm=PROJ_TM, seq=seq)
    anp = _attn_prompt(sinks, qp, kp, vp, kmeta, vmeta, ga, batch=batch, seq=seq,
                       nsub=ATTN_NSUB)

    to_channel_major = lambda c: jnp.transpose(c[0], (0, 2, 3, 1)).reshape(n_dec, KV_DIM, WINDOW)
    from_channel_major = lambda c: jnp.transpose(
        c.reshape(n_dec, N_KV_HEADS, HEAD_DIM, WINDOW), (0, 3, 1, 2))[None]
    nk, nv, ans = _attn_sample(sinks, qt[:n_dec], kt[:n_dec], vt[:n_dec],
                               to_channel_major(cache_swa_k), to_channel_major(cache_swa_v), ga,
                               bs=SAMPLE_BS)

    xp2, xpn = _merge(xp1, anp, cnp, wo_a, wo_c, row2d(g_ffn2[0]), tm=MERGE_TM,
                      name="merge_prompt")
    xs2, xsn = _merge(xt1[:n_dec], ans, cnt[:n_dec], wo_a, wo_c, row2d(g_ffn2[0]), tm=n_dec,
                      name="merge_tail")
    ys = _ffn(xs2, xsn, w2g, w2u, w2d, tm=n_dec, tf=FFN_TF, g_final=row2d(g_final),
              name="ffn2_tail")
    yp = _ffn(xp2, xpn, w2g, w2u, w2d, tm=FFN_TM, tf=FFN_TF, g_final=row2d(g_final),
              name="ffn2_prompt")

    kv_shape = (1, batch, WINDOW, N_KV_HEADS, HEAD_DIM)
    new_k_prompt = kp.reshape(batch, seq, KV_DIM)[:, seq - WINDOW:].reshape(kv_shape)
    new_v_prompt = vp.reshape(batch, seq, KV_DIM)[:, seq - WINDOW:].reshape(kv_shape)
    new_conv_sample = jnp.stack([st[:, 1, :], ut[:n_dec]], axis=1)[None]
    return (yp.reshape(batch, seq, D_MODEL), ys.reshape(n_dec, 1, D_MODEL),
            new_k_prompt, new_v_prompt, cstate[None],
            from_channel_major(nk), from_channel_major(nv), new_conv_sample)
```

```python
import functools

import jax
import jax.numpy as jnp
from jax import lax
from jax.experimental import pallas as pl
from jax.experimental.pallas import tpu as pltpu

D_MODEL = 2048
D_FF = 5632
N_META = 16
ATTN_DIM = 1024
CONV_DIM = 1024
HEAD_DIM = 64
N_HEADS = 16
N_KV_HEADS = 4
GQA_GROUP = 4
KV_DIM = 256
WINDOW = 128
IN_DIM = ATTN_DIM + 2 * KV_DIM + 3 * CONV_DIM
EPS = 1e-5
NEG = -1e30

LANES = 128
HALF = LANES // 2
VMEM_LIMIT = 60 * 1024 * 1024
FFN_TM, FFN_TF = 1024, 512
PROJ_TM = 512
PROJ_ROW_CHUNKS = 2
CAST_TF = 256
ATTN_NSUB = 4
SAMPLE_BS = 16

F32 = jnp.float32
BF16 = jnp.bfloat16


def _rms(x, g):
    return x * lax.rsqrt(jnp.mean(x * x, axis=-1, keepdims=True) + EPS) * g


def _dot(a, b):
    return jnp.dot(a, b, preferred_element_type=F32)


def _dot_t(a, b):
    return lax.dot_general(a, b, (((1,), (1,)), ((), ())), preferred_element_type=F32)


def _resident(shape):
    return pl.BlockSpec(shape, lambda *_: (0,) * len(shape), pipeline_mode=pl.Buffered(1))


def _ffn_kernel(*refs, n_ff, normed_input, has_final, cast_weights):
    refs = list(refs)
    x_ref, norm_ref, wg_ref, wu_ref, wd_ref = refs[:5]
    refs = refs[5:]
    if has_final:
        gf_ref = refs.pop(0)
    out_ref = refs.pop(0)
    if cast_weights:
        wgb_ref, wub_ref, wdb_ref = refs[:3]
        refs = refs[3:]
    xn_ref = norm_ref if normed_input else refs.pop(0)
    j = pl.program_id(1)

    @pl.when(j == 0)
    def _():
        x = x_ref[...]
        out_ref[...] = x
        if not normed_input:
            xn_ref[...] = _rms(x, norm_ref[...]).astype(BF16)

    if cast_weights:
        wgb_ref[...] = wg_ref[...].astype(BF16)
        wub_ref[...] = wu_ref[...].astype(BF16)
        wdb_ref[...] = wd_ref[...].astype(BF16)
        wg_ref, wu_ref, wd_ref = wgb_ref, wub_ref, wdb_ref
    xn = xn_ref[...]
    gate = _dot(xn, wg_ref[...])
    up = _dot(xn, wu_ref[...])
    h = (gate * jax.nn.sigmoid(gate) * up * 0.5).astype(BF16)
    out_ref[...] += _dot(h, wd_ref[...])

    if has_final:
        @pl.when(j == n_ff - 1)
        def _():
            out_ref[...] = _rms(out_ref[...], gf_ref[...])


def _ffn(x, norm, wg, wu, wd, *, tm, tf, g_final=None, name):
    m = x.shape[0]
    assert m % tm == 0 and D_FF % tf == 0
    n_ff = D_FF // tf
    normed_input = norm.shape[0] == m
    cast_weights = wg.dtype == F32
    assert not cast_weights or m == tm
    row = lambda i, j: (i, 0)
    w_specs = [pl.BlockSpec((D_MODEL, tf), lambda i, j: (0, j)),
               pl.BlockSpec((D_MODEL, tf), lambda i, j: (0, j)),
               pl.BlockSpec((tf, D_MODEL), lambda i, j: (j, 0))]
    in_specs = [pl.BlockSpec((tm, D_MODEL), row),
                pl.BlockSpec((tm, D_MODEL), row) if normed_input else _resident((1, D_MODEL)),
                *w_specs]
    args = [x, norm, wg, wu, wd]
    if g_final is not None:
        in_specs.append(_resident((1, D_MODEL)))
        args.append(g_final)
    out_specs = [pl.BlockSpec((tm, D_MODEL), row)]
    out_shape = [jax.ShapeDtypeStruct((m, D_MODEL), F32)]
    if cast_weights:
        out_specs += w_specs
        out_shape += [jax.ShapeDtypeStruct(w.shape, BF16) for w in (wg, wu, wd)]
    outs = pl.pallas_call(
        functools.partial(_ffn_kernel, n_ff=n_ff, normed_input=normed_input,
                          has_final=g_final is not None, cast_weights=cast_weights),
        grid=(m // tm, n_ff),
        in_specs=in_specs,
        out_specs=out_specs,
        out_shape=out_shape,
        scratch_shapes=[] if normed_input else [pltpu.VMEM((tm, D_MODEL), BF16)],
        compiler_params=pltpu.CompilerParams(
            dimension_semantics=("parallel", "arbitrary"), vmem_limit_bytes=VMEM_LIMIT),
        name=name,
    )(*args)
    return outs if cast_weights else outs[0]


def _merge_kernel(x_ref, a_ref, c_ref, woa_ref, woc_ref, g_ref, x2_ref, xn_ref):
    x = x_ref[...] + (_dot(a_ref[...].astype(BF16), woa_ref[...]) +
                      _dot(c_ref[...].astype(BF16), woc_ref[...]))
    x2_ref[...] = x
    xn_ref[...] = _rms(x, g_ref[...]).astype(BF16)


def _merge(x, a, c, woa, woc, g, *, tm, name):
    m = x.shape[0]
    assert m % tm == 0
    row = lambda i: (i, 0)
    return pl.pallas_call(
        _merge_kernel,
        grid=(m // tm,),
        in_specs=[pl.BlockSpec((tm, D_MODEL), row), pl.BlockSpec((tm, ATTN_DIM), row),
                  pl.BlockSpec((tm, CONV_DIM), row), _resident((ATTN_DIM, D_MODEL)),
                  _resident((CONV_DIM, D_MODEL)), _resident((1, D_MODEL))],
        out_specs=[pl.BlockSpec((tm, D_MODEL), row), pl.BlockSpec((tm, D_MODEL), row)],
        out_shape=[jax.ShapeDtypeStruct((m, D_MODEL), F32),
                   jax.ShapeDtypeStruct((m, D_MODEL), BF16)],
        compiler_params=pltpu.CompilerParams(
            dimension_semantics=("parallel",), vmem_limit_bytes=VMEM_LIMIT),
        name=name,
    )(x, a, c, woa, woc, g)


K0, V0, B0, C0, H0, R_END = 0, 256, 512, 1536, 2560, 3584


def _project_conv(x_ref, g_ref, wr_ref):
    xn = _rms(x_ref[...], g_ref[...]).astype(BF16)
    gate_b = _dot(xn, wr_ref[:, B0:C0])
    u = _dot(xn, wr_ref[:, C0:H0]) * _dot(xn, wr_ref[:, H0:R_END])
    return xn, gate_b, u


def _project_qkv(xn, wq_ref, wr_ref):
    q = _dot(xn, wq_ref[...]) * (HEAD_DIM ** -0.5)
    return q, _dot(xn, wr_ref[:, K0:V0]), _dot(xn, wr_ref[:, V0:B0])


def _proj_prompt_kernel(x_ref, g_ref, wq_ref, wr_ref, wc_ref, gc_ref, umeta_ref,
                        fg_ref, fu_ref, fd_ref,
                        q_ref, k_ref, v_ref, cn_ref, cst_ref, bg_ref, bu_ref, bd_ref,
                        us_ref, *, tm, tiles_per_seq):
    t = pl.program_id(0) % tiles_per_seq

    @pl.when(t == 0)
    def _():
        us_ref[0:8, :] = umeta_ref[...]

    @pl.when(t != 0)
    def _():
        us_ref[0:8, :] = us_ref[tm:tm + 8, :]

    hm = tm // PROJ_ROW_CHUNKS
    casts = ((fg_ref, bg_ref), (fu_ref, bu_ref), (fd_ref, bd_ref))
    for h in range(PROJ_ROW_CHUNKS):
        rows = slice(h * hm, (h + 1) * hm)
        for src, dst in casts[h::PROJ_ROW_CHUNKS]:
            dst[...] = src[...].astype(BF16)
        xn, gate_b, u = _project_conv(x_ref.at[rows], g_ref, wr_ref)
        us_ref[8 + h * hm:8 + (h + 1) * hm, :] = u
        y = (wc_ref[0:1, :] * us_ref[6 + h * hm:6 + (h + 1) * hm, :]
             + wc_ref[1:2, :] * us_ref[7 + h * hm:7 + (h + 1) * hm, :] + wc_ref[2:3, :] * u)
        cn_ref[rows, :] = _rms(gate_b * y, gc_ref[...]).astype(BF16)
        q, k, v = _project_qkv(xn, wq_ref, wr_ref)
        q_ref[rows, :] = q.astype(BF16)
        k_ref[rows, :] = k
        v_ref[rows, :] = v
    cst_ref[0] = us_ref[tm + 6:tm + 8, :]


def _proj_prompt(x, g, wq, wr, wc, gc, umeta, ffn_w, *, tm, seq):
    m = x.shape[0]
    tiles_per_seq = seq // tm
    n_steps = m // tm
    n_cast = D_FF // CAST_TF
    assert D_FF % CAST_TF == 0 and n_cast <= n_steps
    row = lambda i: (i, 0)
    cast_cols = pl.BlockSpec((D_MODEL, CAST_TF), lambda i: (0, jnp.minimum(i, n_cast - 1)))
    cast_rows = pl.BlockSpec((CAST_TF, D_MODEL), lambda i: (jnp.minimum(i, n_cast - 1), 0))
    return pl.pallas_call(
        functools.partial(_proj_prompt_kernel, tm=tm, tiles_per_seq=tiles_per_seq),
        grid=(n_steps,),
        in_specs=[pl.BlockSpec((tm, D_MODEL), row), _resident((1, D_MODEL)),
                  _resident((D_MODEL, ATTN_DIM)), _resident((D_MODEL, R_END)),
                  _resident((3, CONV_DIM)), _resident((1, CONV_DIM)), _resident((8, CONV_DIM)),
                  cast_cols, cast_cols, cast_rows],
        out_specs=[pl.BlockSpec((tm, ATTN_DIM), row), pl.BlockSpec((tm, KV_DIM), row),
                   pl.BlockSpec((tm, KV_DIM), row), pl.BlockSpec((tm, CONV_DIM), row),
                   pl.BlockSpec((1, 2, CONV_DIM), lambda i: (i // tiles_per_seq, 0, 0)),
                   cast_cols, cast_cols, cast_rows],
        out_shape=[jax.ShapeDtypeStruct((m, ATTN_DIM), BF16),
                   jax.ShapeDtypeStruct((m, KV_DIM), F32),
                   jax.ShapeDtypeStruct((m, KV_DIM), F32),
                   jax.ShapeDtypeStruct((m, CONV_DIM), BF16),
                   jax.ShapeDtypeStruct((m // seq, 2, CONV_DIM), F32)]
                  + [jax.ShapeDtypeStruct(w.shape, BF16) for w in ffn_w],
        scratch_shapes=[pltpu.VMEM((tm + 8, CONV_DIM), F32)],
        compiler_params=pltpu.CompilerParams(
            dimension_semantics=("arbitrary",), vmem_limit_bytes=VMEM_LIMIT),
        name="proj_prompt",
    )(x, g, wq, wr, wc, gc, umeta, *ffn_w)


def _proj_tail_kernel(x_ref, g_ref, wq_ref, wr_ref, wc_ref, gc_ref, h0_ref, h1_ref,
                      q_ref, k_ref, v_ref, u_ref, cn_ref):
    xn, gate_b, u = _project_conv(x_ref, g_ref, wr_ref)
    q, k, v = _project_qkv(xn, wq_ref, wr_ref)
    q_ref[...] = q
    k_ref[...] = k
    v_ref[...] = v
    u_ref[...] = u
    y = wc_ref[0:1, :] * h0_ref[...] + wc_ref[1:2, :] * h1_ref[...] + wc_ref[2:3, :] * u
    cn_ref[...] = _rms(gate_b * y, gc_ref[...]).astype(BF16)


def _proj_tail(x, g, wq, wr, wc, gc, h0, h1):
    m = x.shape[0]
    full = lambda shape: pl.BlockSpec(shape, lambda i: (0,) * len(shape))
    return pl.pallas_call(
        _proj_tail_kernel,
        grid=(1,),
        in_specs=[full((m, D_MODEL)), full((1, D_MODEL)), _resident((D_MODEL, ATTN_DIM)),
                  _resident((D_MODEL, R_END)),
                  full((3, CONV_DIM)), full((1, CONV_DIM)), full((m, CONV_DIM)),
                  full((m, CONV_DIM))],
        out_specs=[full((m, ATTN_DIM)), full((m, KV_DIM)), full((m, KV_DIM)),
                   full((m, CONV_DIM)), full((m, CONV_DIM))],
        out_shape=[jax.ShapeDtypeStruct((m, ATTN_DIM), F32),
                   jax.ShapeDtypeStruct((m, KV_DIM), F32),
                   jax.ShapeDtypeStruct((m, KV_DIM), F32),
                   jax.ShapeDtypeStruct((m, CONV_DIM), F32),
                   jax.ShapeDtypeStruct((m, CONV_DIM), BF16)],
        compiler_params=pltpu.CompilerParams(
            dimension_semantics=("arbitrary",), vmem_limit_bytes=VMEM_LIMIT),
        name="proj_tail",
    )(x, g, wq, wr, wc, gc, h0, h1)


class _BandedAttention:
    def __init__(self, first, sink_ref, q_ref, kprev, kcur, vprev, vcur, ga_ref, o_ref, nsub):
        self.sink_ref, self.q_ref, self.ga_ref, self.o_ref = sink_ref, q_ref, ga_ref, o_ref
        self.n_pairs = N_KV_HEADS // 2
        self.units = [(sb, pair) for sb in range(nsub) for pair in range(self.n_pairs)]
        kall = jnp.concatenate([kprev, kcur], axis=0)
        vall = jnp.concatenate([vprev, vcur], axis=0)
        c = lax.broadcasted_iota(jnp.int32, (2 * WINDOW, WINDOW), 0)
        r = lax.broadcasted_iota(jnp.int32, (2 * WINDOW, WINDOW), 1)
        self.band = (c >= r) & (c <= r + WINDOW)
        self.band_first = self.band & ((c >= WINDOW - N_META) | jnp.logical_not(first))
        low = lax.broadcasted_iota(jnp.int32, kall.shape[:1] + (LANES,), 1) < HALF
        self.k_half, self.vt_half = [], []
        for pair in range(self.n_pairs):
            kcol = kall[:, pair * LANES:(pair + 1) * LANES]
            vcol = vall[:, pair * LANES:(pair + 1) * LANES]
            self.k_half.append((jnp.where(low, kcol, 0.0).astype(BF16),
                                jnp.where(low, 0.0, kcol).astype(BF16)))
            self.vt_half.append((jnp.where(low, vcol, 0.0).T.astype(BF16),
                                 jnp.where(low, 0.0, vcol).T.astype(BF16)))
        self.outs = {}

    def scores(self, unit):
        sb, pair = unit
        rows = slice(sb * WINDOW, (sb + 1) * WINDOW)
        keys = slice(sb * WINDOW, (sb + 2) * WINDOW)
        kst = jnp.concatenate([self.k_half[pair][0][keys], self.k_half[pair][1][keys]], axis=0)
        qst = jnp.concatenate(
            [self.q_ref[rows, (grp * self.n_pairs + pair) * LANES:
                        (grp * self.n_pairs + pair + 1) * LANES]
             for grp in range(GQA_GROUP)], axis=0)
        return _dot_t(kst, qst)

    def softmax(self, unit, st_all):
        sb, pair = unit
        valid = self.band_first if sb == 0 else self.band
        pt_rows = []
        for half in range(2):
            pt_cols = []
            for grp in range(GQA_GROUP):
                sink = self.sink_ref[(2 * pair + half) * GQA_GROUP + grp]
                st = st_all[half * 2 * WINDOW:(half + 1) * 2 * WINDOW,
                            grp * WINDOW:(grp + 1) * WINDOW]
                st = jnp.where(valid, st, NEG)
                mx = jnp.maximum(jnp.max(st, axis=0, keepdims=True), sink)
                e = jnp.exp(st - mx)
                den = jnp.sum(e, axis=0, keepdims=True) + jnp.exp(sink - mx)
                pt_cols.append((e * (1.0 / den)).astype(BF16))
            pt_rows.append(jnp.concatenate(pt_cols, axis=1))
        return jnp.concatenate(pt_rows, axis=0)

    def values(self, unit, pt):
        sb, pair = unit
        keys = slice(sb * WINDOW, (sb + 2) * WINDOW)
        vst = jnp.concatenate([self.vt_half[pair][0][:, keys], self.vt_half[pair][1][:, keys]],
                              axis=1)
        ot_all = _dot(vst, pt)
        for grp in range(GQA_GROUP):
            self.outs[(sb, grp * self.n_pairs + pair)] = ot_all[:, grp * WINDOW:(grp + 1) * WINDOW]
        if pair == self.n_pairs - 1:
            self._normalise(sb)

    def _normalise(self, sb):
        rows = slice(sb * WINDOW, (sb + 1) * WINDOW)
        cols = range(ATTN_DIM // LANES)
        ssq = sum(jnp.sum(self.outs[(sb, col)] ** 2, axis=0, keepdims=True) for col in cols)
        inv = lax.rsqrt(ssq * (1.0 / ATTN_DIM) + EPS)
        for col in cols:
            sl = slice(col * LANES, (col + 1) * LANES)
            self.o_ref[rows, sl] = ((self.outs.pop((sb, col)) * inv).T
                                    * self.ga_ref[:, sl]).astype(BF16)


def _attn_merge_kernel(sink_ref, q_ref, kp_ref, kc_ref, vp_ref, vc_ref, km_ref, vm_ref, ga_ref,
                       x_ref, c_ref, woa_ref, woc_ref, g_ref, x2_ref, xn_ref,
                       an_new_ref, an_old_ref, *, nsub, tiles_per_seq, n_tiles):
    t = pl.program_id(0)

    @pl.when(t == 0)
    def _():
        an_new_ref[...] = jnp.zeros(an_new_ref.shape, an_new_ref.dtype)

    an_old_ref[...] = an_new_ref[...]

    first = (jnp.minimum(t, n_tiles - 1) % tiles_per_seq) == 0
    attn = _BandedAttention(first, sink_ref, q_ref,
                            jnp.where(first, km_ref[...], kp_ref[...]), kc_ref[...],
                            jnp.where(first, vm_ref[...], vp_ref[...]), vc_ref[...],
                            ga_ref, an_new_ref, nsub)
    units = attn.units
    n_chunks = len(units)
    cw = D_MODEL // n_chunks
    ssq = jnp.zeros((x_ref.shape[0], 1), F32)
    st_next = attn.scores(units[0])
    for u, unit in enumerate(units):
        cols = slice(u * cw, (u + 1) * cw)
        xc = x_ref[:, cols] + (_dot(an_old_ref[...], woa_ref[:, cols]) +
                               _dot(c_ref[...], woc_ref[:, cols]))
        x2_ref[:, cols] = xc
        ssq = ssq + jnp.sum(xc * xc, axis=-1, keepdims=True)
        st_all, st_next = st_next, (attn.scores(units[u + 1]) if u + 1 < n_chunks else None)
        attn.values(unit, attn.softmax(unit, st_all))
    inv = lax.rsqrt(ssq * (1.0 / D_MODEL) + EPS)
    for u in range(n_chunks):
        cols = slice(u * cw, (u + 1) * cw)
        xn_ref[:, cols] = (x2_ref[:, cols] * inv * g_ref[:, cols]).astype(BF16)


def _attn_merge(sinks, q, k, v, kmeta, vmeta, ga, x, c, woa, woc, g, *, seq, nsub):
    m = x.shape[0]
    tm = nsub * WINDOW
    n_tiles = m // tm
    tiles_per_seq = seq // tm
    cur = lambda t: (jnp.minimum(t, n_tiles - 1), 0)
    prev = lambda t: (jnp.maximum(jnp.minimum(t, n_tiles - 1) * nsub - 1, 0), 0)
    lag = lambda t: (jnp.maximum(t - 1, 0), 0)
    return pl.pallas_call(
        functools.partial(_attn_merge_kernel, nsub=nsub, tiles_per_seq=tiles_per_seq,
                          n_tiles=n_tiles),
        grid=(n_tiles + 1,),
        in_specs=[pl.BlockSpec(memory_space=pltpu.SMEM),
                  pl.BlockSpec((tm, ATTN_DIM), cur),
                  pl.BlockSpec((WINDOW, KV_DIM), prev), pl.BlockSpec((tm, KV_DIM), cur),
                  pl.BlockSpec((WINDOW, KV_DIM), prev), pl.BlockSpec((tm, KV_DIM), cur),
                  _resident((WINDOW, KV_DIM)), _resident((WINDOW, KV_DIM)),
                  _resident((1, ATTN_DIM)),
                  pl.BlockSpec((tm, D_MODEL), lag), pl.BlockSpec((tm, CONV_DIM), lag),
                  _resident((ATTN_DIM, D_MODEL)), _resident((CONV_DIM, D_MODEL)),
                  _resident((1, D_MODEL))],
        out_specs=[pl.BlockSpec((tm, D_MODEL), lag), pl.BlockSpec((tm, D_MODEL), lag)],
        out_shape=[jax.ShapeDtypeStruct((m, D_MODEL), F32),
                   jax.ShapeDtypeStruct((m, D_MODEL), BF16)],
        scratch_shapes=[pltpu.VMEM((tm, ATTN_DIM), BF16), pltpu.VMEM((tm, ATTN_DIM), BF16)],
        compiler_params=pltpu.CompilerParams(
            dimension_semantics=("arbitrary",), vmem_limit_bytes=VMEM_LIMIT),
        name="attn_merge_prompt",
    )(sinks, q, k, k, v, v, kmeta, vmeta, ga, x, c, woa, woc, g)


def _attn_sample_kernel(sink_ref, q_ref, kn_ref, vn_ref, ck_ref, cv_ref, ga_ref,
                        ok_ref, ov_ref, an_ref, *, bs):
    n_pairs = N_KV_HEADS // 2
    heads_per_col = 2 * GQA_GROUP
    last_key = lax.broadcasted_iota(jnp.int32, (KV_DIM, WINDOW), 1) == WINDOW - 1
    pad = jnp.zeros((WINDOW - bs, KV_DIM), F32)
    knt = jnp.concatenate([kn_ref[...], pad], axis=0).T
    vnt = jnp.concatenate([vn_ref[...], pad], axis=0).T
    for s in range(bs):
        ok_ref[s] = jnp.where(last_key, knt[:, s:s + 1], pltpu.roll(ck_ref[s], WINDOW - 1, 1))
        ov_ref[s] = jnp.where(last_key, vnt[:, s:s + 1], pltpu.roll(cv_ref[s], WINDOW - 1, 1))

    n_rows = heads_per_col * bs
    row = lax.broadcasted_iota(jnp.int32, (n_rows, bs * WINDOW), 0)
    lane = lax.broadcasted_iota(jnp.int32, (n_rows, bs * WINDOW), 1)
    log2 = lambda n: n.bit_length() - 1
    assert bs == 1 << log2(bs) and WINDOW == 1 << log2(WINDOW)
    same_seq = (row & (bs - 1)) == (lane >> log2(WINDOW))
    head_row = lax.broadcasted_iota(jnp.int32, (n_rows, 1), 0) >> log2(bs)
    low = lax.broadcasted_iota(jnp.int32, (bs, LANES), 1) < HALF

    cols = [None] * (ATTN_DIM // LANES)
    for pair in range(n_pairs):
        sl = slice(pair * LANES, (pair + 1) * LANES)
        pieces = []
        sk = jnp.zeros((n_rows, 1), F32)
        for grp in range(GQA_GROUP):
            col = grp * n_pairs + pair
            qcol = q_ref[:, col * LANES:(col + 1) * LANES]
            pieces += [jnp.where(low, qcol, 0.0), jnp.where(low, 0.0, qcol)]
            for half in range(2):
                sink = sink_ref[(2 * pair + half) * GQA_GROUP + grp]
                sk = jnp.where(head_row == 2 * grp + half, sink, sk)
        lq = jnp.concatenate(pieces, axis=0).astype(BF16)
        kstack = jnp.concatenate([ck_ref[s, sl, :] for s in range(bs)], axis=1).astype(BF16)
        vstack = jnp.concatenate([cv_ref[s, sl, :] for s in range(bs)], axis=1).astype(BF16)
        s = jnp.where(same_seq, _dot(lq, kstack), NEG)
        k_new = jnp.concatenate([kn_ref[:, sl]] * heads_per_col, axis=0).astype(BF16)
        v_new = jnp.concatenate([vn_ref[:, sl]] * heads_per_col, axis=0).astype(BF16)
        s_new = jnp.sum(lq.astype(F32) * k_new.astype(F32), axis=-1, keepdims=True)
        mx = jnp.maximum(jnp.maximum(jnp.max(s, axis=-1, keepdims=True), s_new), sk)
        e = jnp.exp(s - mx)
        e_new = jnp.exp(s_new - mx)
        rden = 1.0 / (jnp.sum(e, axis=-1, keepdims=True) + e_new + jnp.exp(sk - mx))
        p = (e * rden).astype(BF16)
        p_new = (e_new * rden).astype(BF16).astype(F32)
        o = _dot_t(p, vstack) + p_new * v_new.astype(F32)
        for grp in range(GQA_GROUP):
            lo = o[(2 * grp) * bs:(2 * grp + 1) * bs]
            hi = o[(2 * grp + 1) * bs:(2 * grp + 2) * bs]
            cols[grp * n_pairs + pair] = jnp.where(low, lo, hi)
    an_ref[...] = _rms(jnp.concatenate(cols, axis=1), ga_ref[...])


def _attn_sample(sinks, q, kn, vn, ck, cv, ga, *, bs):
    nseq = q.shape[0]
    row = lambda i: (i, 0)
    blk3 = pl.BlockSpec((bs, KV_DIM, WINDOW), lambda i: (i, 0, 0))
    return pl.pallas_call(
        functools.partial(_attn_sample_kernel, bs=bs),
        grid=(nseq // bs,),
        in_specs=[pl.BlockSpec(memory_space=pltpu.SMEM),
                  pl.BlockSpec((bs, ATTN_DIM), row), pl.BlockSpec((bs, KV_DIM), row),
                  pl.BlockSpec((bs, KV_DIM), row), blk3, blk3,
                  pl.BlockSpec((1, ATTN_DIM), lambda i: (0, 0))],
        out_specs=[blk3, blk3, pl.BlockSpec((bs, ATTN_DIM), row)],
        out_shape=[jax.ShapeDtypeStruct(ck.shape, F32), jax.ShapeDtypeStruct(cv.shape, F32),
                   jax.ShapeDtypeStruct((nseq, ATTN_DIM), F32)],
        compiler_params=pltpu.CompilerParams(
            dimension_semantics=("parallel",), vmem_limit_bytes=VMEM_LIMIT),
        name="attn_sample",
    )(sinks, q, kn, vn, ck, cv, ga)


def _group_major(w, axis):
    shape = w.shape
    w = w.reshape(shape[:axis] + (N_KV_HEADS, GQA_GROUP, HEAD_DIM) + shape[axis + 1:])
    return jnp.swapaxes(w, axis, axis + 1).reshape(shape)


def kernel(x_prompt, x_sample, cache_swa_k, cache_swa_v, state_conv, meta_tokens, g_ffn1, w1_gate, w1_up, w1_down, g_mix, w_in, attn_sinks, w_conv, g_attn_out, g_conv_out, w_out, g_ffn2, w2_gate, w2_up, w2_down, g_final):
    assert g_ffn1.shape[0] == 1, "one layer"
    batch, seq, _ = x_prompt.shape
    n_dec = x_sample.shape[0]
    assert x_sample.shape[1] == 1 and cache_swa_k.shape[2] == WINDOW

    row2d = lambda a: a.reshape(1, -1).astype(F32)
    wq = _group_major(w_in[0, :, :ATTN_DIM], 1).astype(BF16)
    wr = w_in[0, :, ATTN_DIM:].astype(BF16)
    wo_a = _group_major(w_out[0, :ATTN_DIM], 0).astype(BF16)
    wo_c = w_out[0, ATTN_DIM:].astype(BF16)
    ga = row2d(_group_major(g_attn_out[0], 0))
    gc = row2d(g_conv_out[0])
    sinks = attn_sinks[0].astype(F32)
    wc = w_conv[0].astype(F32)

    xp = x_prompt.reshape(batch * seq, D_MODEL)
    xt = jnp.concatenate([x_sample[:, 0, :], meta_tokens.astype(x_prompt.dtype)], axis=0)
    n_tail = xt.shape[0]

    xt1, w1g, w1u, w1d = _ffn(xt, row2d(g_ffn1[0]), w1_gate[0], w1_up[0], w1_down[0],
                              tm=n_tail, tf=FFN_TF, name="ffn1_tail")
    xp1 = _ffn(xp, row2d(g_ffn1[0]), w1g, w1u, w1d, tm=FFN_TM, tf=FFN_TF, name="ffn1_prompt")

    st = state_conv[0]
    zpad = jnp.zeros((N_META, CONV_DIM), F32)
    h0 = jnp.concatenate([st[:, 0, :], zpad], axis=0)
    h1 = jnp.concatenate([st[:, 1, :], zpad], axis=0)
    qt, kt, vt, ut, cnt = _proj_tail(xt1, row2d(g_mix[0]), wq, wr, wc, gc, h0, h1)
    kmeta = jnp.pad(kt[n_dec:], ((WINDOW - N_META, 0), (0, 0)))
    vmeta = jnp.pad(vt[n_dec:], ((WINDOW - N_META, 0), (0, 0)))
    umeta = ut[n_tail - 8:]

    qp, kp, vp, cnp, cstate, w2g, w2u, w2d = _proj_prompt(
        xp1, row2d(g_mix[0]), wq, wr, wc, gc, umeta, (w2_gate[0], w2_up[0], w2_down[0]),
        tm=PROJ_TM, seq=seq)
    xp2, xpn = _attn_merge(sinks, qp, kp, vp, kmeta, vmeta, ga, xp1, cnp, wo_a, wo_c,
                           row2d(g_ffn2[0]), seq=seq, nsub=ATTN_NSUB)

    to_channel_major = lambda c: jnp.transpose(c[0], (0, 2, 3, 1)).reshape(n_dec, KV_DIM, WINDOW)
    from_channel_major = lambda c: jnp.transpose(
        c.reshape(n_dec, N_KV_HEADS, HEAD_DIM, WINDOW), (0, 3, 1, 2))[None]
    nk, nv, ans = _attn_sample(sinks, qt[:n_dec], kt[:n_dec], vt[:n_dec],
                               to_channel_major(cache_swa_k), to_channel_major(cache_swa_v), ga,
                               bs=SAMPLE_BS)

    xs2, xsn = _merge(xt1[:n_dec], ans, cnt[:n_dec], wo_a, wo_c, row2d(g_ffn2[0]), tm=n_dec,
                      name="merge_tail")
    ys = _ffn(xs2, xsn, w2g, w2u, w2d, tm=n_dec, tf=FFN_TF, g_final=row2d(g_final),
              name="ffn2_tail")
    yp = _ffn(xp2, xpn, w2g, w2u, w2d, tm=FFN_TM, tf=FFN_TF, g_final=row2d(g_final),
              name="ffn2_prompt")

    kv_shape = (1, batch, WINDOW, N_KV_HEADS, HEAD_DIM)
    new_k_prompt = kp.reshape(batch, seq, KV_DIM)[:, seq - WINDOW:].reshape(kv_shape)
    new_v_prompt = vp.reshape(batch, seq, KV_DIM)[:, seq - WINDOW:].reshape(kv_shape)
    new_conv_sample = jnp.stack([st[:, 1, :], ut[:n_dec]], axis=1)[None]
    return (yp.reshape(batch, seq, D_MODEL), ys.reshape(n_dec, 1, D_MODEL),
            new_k_prompt, new_v_prompt, cstate[None],
            from_channel_major(nk), from_channel_major(nv), new_conv_sample)
```

```python
import functools

import jax
import jax.numpy as jnp
from jax import lax
from jax.experimental import pallas as pl
from jax.experimental.pallas import tpu as pltpu

D_MODEL = 2048
D_FF = 5632
N_META = 16
ATTN_DIM = 1024
CONV_DIM = 1024
HEAD_DIM = 64
N_HEADS = 16
N_KV_HEADS = 4
GQA_GROUP = 4
KV_DIM = 256
WINDOW = 128
IN_DIM = ATTN_DIM + 2 * KV_DIM + 3 * CONV_DIM
EPS = 1e-5
NEG = -1e30

LANES = 128
HALF = LANES // 2
VMEM_LIMIT = 60 * 1024 * 1024
FFN_TM, FFN_TF = 1024, 512
FFN_EDGE_ROWS = 256
PROJ_TM = 512
PROJ_ROW_CHUNKS = 2
CAST_TF = 256
ATTN_NSUB = 4
SAMPLE_BS = 16

F32 = jnp.float32
BF16 = jnp.bfloat16


def _rms(x, g):
    return x * lax.rsqrt(jnp.mean(x * x, axis=-1, keepdims=True) + EPS) * g


def _dot(a, b):
    return jnp.dot(a, b, preferred_element_type=F32)


def _dot_t(a, b):
    return lax.dot_general(a, b, (((1,), (1,)), ((), ())), preferred_element_type=F32)


def _resident(shape):
    return pl.BlockSpec(shape, lambda *_: (0,) * len(shape), pipeline_mode=pl.Buffered(1))


def _ffn_kernel(*refs, n_ff, normed_input, has_final, cast_weights, side_chunks):
    refs = list(refs)
    x_ref, norm_ref, wg_ref, wu_ref, wd_ref = refs[:5]
    refs = refs[5:]
    if has_final:
        gf_ref = refs.pop(0)
    if side_chunks:
        side_src_ref = refs.pop(0)
    out_ref = refs.pop(0)
    if cast_weights:
        wgb_ref, wub_ref, wdb_ref = refs[:3]
        refs = refs[3:]
    if side_chunks:
        side_dst_ref = refs.pop(0)
    xn_ref = norm_ref if normed_input else refs.pop(0)
    j = pl.program_id(1)

    if side_chunks:
        @pl.when(pl.program_id(0) * n_ff + j < side_chunks)
        def _():
            side_dst_ref[...] = side_src_ref[...].astype(BF16)

    if cast_weights:
        wgb_ref[...] = wg_ref[...].astype(BF16)
        wub_ref[...] = wu_ref[...].astype(BF16)
        wdb_ref[...] = wd_ref[...].astype(BF16)
        wg_ref, wu_ref, wd_ref = wgb_ref, wub_ref, wdb_ref

    def chunk_step(first, last, chunks):
        cm = out_ref.shape[0] // chunks
        for c in range(chunks):
            rows = slice(c * cm, (c + 1) * cm)
            if first and not normed_input:
                xn = _rms(x_ref[rows, :], norm_ref[...]).astype(BF16)
                xn_ref[rows, :] = xn
            else:
                xn = xn_ref[rows, :]
            gate = _dot(xn, wg_ref[...])
            up = _dot(xn, wu_ref[...])
            h = (gate * jax.nn.sigmoid(gate) * up * 0.5).astype(BF16)
            acc = (x_ref[rows, :] if first else out_ref[rows, :]) + _dot(h, wd_ref[...])
            if last and has_final:
                acc = _rms(acc, gf_ref[...])
            out_ref[rows, :] = acc

    row_chunks = max(1, out_ref.shape[0] // FFN_EDGE_ROWS)
    first_chunks = 1 if normed_input else row_chunks
    last_chunks = row_chunks if has_final else 1
    if n_ff == 1:
        chunk_step(True, True, row_chunks)
        return
    pl.when(j == 0)(lambda: chunk_step(True, False, first_chunks))
    if has_final:
        pl.when((j > 0) & (j < n_ff - 1))(lambda: chunk_step(False, False, 1))
        pl.when(j == n_ff - 1)(lambda: chunk_step(False, True, last_chunks))
    else:
        pl.when(j > 0)(lambda: chunk_step(False, False, 1))


def _ffn(x, norm, wg, wu, wd, *, tm, tf, g_final=None, side_cast=None, name):
    m = x.shape[0]
    assert m % tm == 0 and D_FF % tf == 0
    n_ff = D_FF // tf
    normed_input = norm.shape[0] == m
    cast_weights = wg.dtype == F32
    assert not cast_weights or m == tm
    side_src, side_col0 = side_cast if side_cast is not None else (None, 0)
    side_chunks = 0 if side_src is None else (side_src.shape[1] - side_col0) // CAST_TF
    assert side_chunks <= (m // tm) * n_ff and side_col0 % CAST_TF == 0
    row = lambda i, j: (i, 0)
    side_chunk = lambda i, j: jnp.minimum(i * n_ff + j, side_chunks - 1)
    side_in = pl.BlockSpec((D_MODEL, CAST_TF),
                           lambda i, j: (0, side_col0 // CAST_TF + side_chunk(i, j)))
    side_out = pl.BlockSpec((D_MODEL, CAST_TF), lambda i, j: (0, side_chunk(i, j)))
    w_specs = [pl.BlockSpec((D_MODEL, tf), lambda i, j: (0, j)),
               pl.BlockSpec((D_MODEL, tf), lambda i, j: (0, j)),
               pl.BlockSpec((tf, D_MODEL), lambda i, j: (j, 0))]
    in_specs = [pl.BlockSpec((tm, D_MODEL), row),
                pl.BlockSpec((tm, D_MODEL), row) if normed_input else _resident((1, D_MODEL)),
                *w_specs]
    args = [x, norm, wg, wu, wd]
    if g_final is not None:
        in_specs.append(_resident((1, D_MODEL)))
        args.append(g_final)
    if side_chunks:
        in_specs.append(side_in)
        args.append(side_src)
    out_specs = [pl.BlockSpec((tm, D_MODEL), row)]
    out_shape = [jax.ShapeDtypeStruct((m, D_MODEL), F32)]
    if cast_weights:
        out_specs += w_specs
        out_shape += [jax.ShapeDtypeStruct(w.shape, BF16) for w in (wg, wu, wd)]
    if side_chunks:
        out_specs.append(side_out)
        out_shape.append(jax.ShapeDtypeStruct((D_MODEL, side_chunks * CAST_TF), BF16))
    outs = pl.pallas_call(
        functools.partial(_ffn_kernel, n_ff=n_ff, normed_input=normed_input,
                          has_final=g_final is not None, cast_weights=cast_weights,
                          side_chunks=side_chunks),
        grid=(m // tm, n_ff),
        in_specs=in_specs,
        out_specs=out_specs,
        out_shape=out_shape,
        scratch_shapes=[] if normed_input else [pltpu.VMEM((tm, D_MODEL), BF16)],
        compiler_params=pltpu.CompilerParams(
            dimension_semantics=("arbitrary" if side_chunks else "parallel", "arbitrary"),
            vmem_limit_bytes=VMEM_LIMIT),
        name=name,
    )(*args)
    return outs if len(outs) > 1 else outs[0]


def _merge_kernel(x_ref, a_ref, c_ref, woa_ref, woc_ref, g_ref, x2_ref, xn_ref):
    x = x_ref[...] + (_dot(a_ref[...].astype(BF16), woa_ref[...]) +
                      _dot(c_ref[...].astype(BF16), woc_ref[...]))
    x2_ref[...] = x
    xn_ref[...] = _rms(x, g_ref[...]).astype(BF16)


def _merge(x, a, c, woa, woc, g, *, tm, name):
    m = x.shape[0]
    assert m % tm == 0
    row = lambda i: (i, 0)
    return pl.pallas_call(
        _merge_kernel,
        grid=(m // tm,),
        in_specs=[pl.BlockSpec((tm, D_MODEL), row), pl.BlockSpec((tm, ATTN_DIM), row),
                  pl.BlockSpec((tm, CONV_DIM), row), _resident((ATTN_DIM, D_MODEL)),
                  _resident((CONV_DIM, D_MODEL)), _resident((1, D_MODEL))],
        out_specs=[pl.BlockSpec((tm, D_MODEL), row), pl.BlockSpec((tm, D_MODEL), row)],
        out_shape=[jax.ShapeDtypeStruct((m, D_MODEL), F32),
                   jax.ShapeDtypeStruct((m, D_MODEL), BF16)],
        compiler_params=pltpu.CompilerParams(
            dimension_semantics=("parallel",), vmem_limit_bytes=VMEM_LIMIT),
        name=name,
    )(x, a, c, woa, woc, g)


K0, V0, B0, C0, H0, R_END = 0, 256, 512, 1536, 2560, 3584


def _project_conv(x_ref, g_ref, wr_ref):
    xn = _rms(x_ref[...], g_ref[...]).astype(BF16)
    gate_b = _dot(xn, wr_ref[:, B0:C0])
    u = _dot(xn, wr_ref[:, C0:H0]) * _dot(xn, wr_ref[:, H0:R_END])
    return xn, gate_b, u


def _project_qkv(xn, wq_ref, wr_ref):
    q = _dot(xn, wq_ref[...]) * (HEAD_DIM ** -0.5)
    return q, _dot(xn, wr_ref[:, K0:V0]), _dot(xn, wr_ref[:, V0:B0])


def _proj_prompt_kernel(x_ref, g_ref, wq_ref, wr_ref, wc_ref, gc_ref, umeta_ref,
                        fg_ref, fu_ref, fd_ref,
                        q_ref, k_ref, v_ref, cn_ref, cst_ref, bg_ref, bu_ref, bd_ref,
                        us_ref, *, tm, tiles_per_seq):
    t = pl.program_id(0) % tiles_per_seq

    @pl.when(t == 0)
    def _():
        us_ref[0:8, :] = umeta_ref[...]

    @pl.when(t != 0)
    def _():
        us_ref[0:8, :] = us_ref[tm:tm + 8, :]

    hm = tm // PROJ_ROW_CHUNKS
    casts = ((fg_ref, bg_ref), (fu_ref, bu_ref), (fd_ref, bd_ref))
    for h in range(PROJ_ROW_CHUNKS):
        rows = slice(h * hm, (h + 1) * hm)
        for src, dst in casts[h::PROJ_ROW_CHUNKS]:
            dst[...] = src[...].astype(BF16)
        xn, gate_b, u = _project_conv(x_ref.at[rows], g_ref, wr_ref)
        us_ref[8 + h * hm:8 + (h + 1) * hm, :] = u
        y = (wc_ref[0:1, :] * us_ref[6 + h * hm:6 + (h + 1) * hm, :]
             + wc_ref[1:2, :] * us_ref[7 + h * hm:7 + (h + 1) * hm, :] + wc_ref[2:3, :] * u)
        cn_ref[rows, :] = _rms(gate_b * y, gc_ref[...]).astype(BF16)
        q, k, v = _project_qkv(xn, wq_ref, wr_ref)
        q_ref[rows, :] = q.astype(BF16)
        k_ref[rows, :] = k
        v_ref[rows, :] = v
    cst_ref[0] = us_ref[tm + 6:tm + 8, :]


def _proj_prompt(x, g, wq, wr, wc, gc, umeta, ffn_w, *, tm, seq):
    m = x.shape[0]
    tiles_per_seq = seq // tm
    n_steps = m // tm
    n_cast = D_FF // CAST_TF
    assert D_FF % CAST_TF == 0 and n_cast <= n_steps
    row = lambda i: (i, 0)
    cast_cols = pl.BlockSpec((D_MODEL, CAST_TF), lambda i: (0, jnp.minimum(i, n_cast - 1)))
    cast_rows = pl.BlockSpec((CAST_TF, D_MODEL), lambda i: (jnp.minimum(i, n_cast - 1), 0))
    return pl.pallas_call(
        functools.partial(_proj_prompt_kernel, tm=tm, tiles_per_seq=tiles_per_seq),
        grid=(n_steps,),
        in_specs=[pl.BlockSpec((tm, D_MODEL), row), _resident((1, D_MODEL)),
                  _resident((D_MODEL, ATTN_DIM)), _resident((D_MODEL, R_END)),
                  _resident((3, CONV_DIM)), _resident((1, CONV_DIM)), _resident((8, CONV_DIM)),
                  cast_cols, cast_cols, cast_rows],
        out_specs=[pl.BlockSpec((tm, ATTN_DIM), row), pl.BlockSpec((tm, KV_DIM), row),
                   pl.BlockSpec((tm, KV_DIM), row), pl.BlockSpec((tm, CONV_DIM), row),
                   pl.BlockSpec((1, 2, CONV_DIM), lambda i: (i // tiles_per_seq, 0, 0)),
                   cast_cols, cast_cols, cast_rows],
        out_shape=[jax.ShapeDtypeStruct((m, ATTN_DIM), BF16),
                   jax.ShapeDtypeStruct((m, KV_DIM), F32),
                   jax.ShapeDtypeStruct((m, KV_DIM), F32),
                   jax.ShapeDtypeStruct((m, CONV_DIM), BF16),
                   jax.ShapeDtypeStruct((m // seq, 2, CONV_DIM), F32)]
                  + [jax.ShapeDtypeStruct(w.shape, BF16) for w in ffn_w],
        scratch_shapes=[pltpu.VMEM((tm + 8, CONV_DIM), F32)],
        compiler_params=pltpu.CompilerParams(
            dimension_semantics=("arbitrary",), vmem_limit_bytes=VMEM_LIMIT),
        name="proj_prompt",
    )(x, g, wq, wr, wc, gc, umeta, *ffn_w)


def _proj_tail_kernel(x_ref, g_ref, wq_ref, wr_ref, wc_ref, gc_ref, h0_ref, h1_ref,
                      q_ref, k_ref, v_ref, u_ref, cn_ref):
    xn, gate_b, u = _project_conv(x_ref, g_ref, wr_ref)
    q, k, v = _project_qkv(xn, wq_ref, wr_ref)
    q_ref[...] = q
    k_ref[...] = k
    v_ref[...] = v
    u_ref[...] = u
    y = wc_ref[0:1, :] * h0_ref[...] + wc_ref[1:2, :] * h1_ref[...] + wc_ref[2:3, :] * u
    cn_ref[...] = _rms(gate_b * y, gc_ref[...]).astype(BF16)


def _proj_tail(x, g, wq, wr, wc, gc, h0, h1):
    m = x.shape[0]
    full = lambda shape: pl.BlockSpec(shape, lambda i: (0,) * len(shape))
    return pl.pallas_call(
        _proj_tail_kernel,
        grid=(1,),
        in_specs=[full((m, D_MODEL)), full((1, D_MODEL)), _resident((D_MODEL, ATTN_DIM)),
                  _resident((D_MODEL, R_END)),
                  full((3, CONV_DIM)), full((1, CONV_DIM)), full((m, CONV_DIM)),
                  full((m, CONV_DIM))],
        out_specs=[full((m, ATTN_DIM)), full((m, KV_DIM)), full((m, KV_DIM)),
                   full((m, CONV_DIM)), full((m, CONV_DIM))],
        out_shape=[jax.ShapeDtypeStruct((m, ATTN_DIM), F32),
                   jax.ShapeDtypeStruct((m, KV_DIM), F32),
                   jax.ShapeDtypeStruct((m, KV_DIM), F32),
                   jax.ShapeDtypeStruct((m, CONV_DIM), F32),
                   jax.ShapeDtypeStruct((m, CONV_DIM), BF16)],
        compiler_params=pltpu.CompilerParams(
            dimension_semantics=("arbitrary",), vmem_limit_bytes=VMEM_LIMIT),
        name="proj_tail",
    )(x, g, wq, wr, wc, gc, h0, h1)


class _BandedAttention:
    def __init__(self, first, sink_ref, q_ref, kprev, kcur, vprev, vcur, ga_ref, o_ref, nsub):
        self.sink_ref, self.q_ref, self.ga_ref, self.o_ref = sink_ref, q_ref, ga_ref, o_ref
        self.n_pairs = N_KV_HEADS // 2
        self.units = [(sb, pair) for sb in range(nsub) for pair in range(self.n_pairs)]
        kall = jnp.concatenate([kprev, kcur], axis=0)
        vall = jnp.concatenate([vprev, vcur], axis=0)
        c = lax.broadcasted_iota(jnp.int32, (2 * WINDOW, WINDOW), 0)
        r = lax.broadcasted_iota(jnp.int32, (2 * WINDOW, WINDOW), 1)
        self.band = (c >= r) & (c <= r + WINDOW)
        self.band_first = self.band & ((c >= WINDOW - N_META) | jnp.logical_not(first))
        low = lax.broadcasted_iota(jnp.int32, kall.shape[:1] + (LANES,), 1) < HALF
        self.k_half, self.vt_half = [], []
        for pair in range(self.n_pairs):
            kcol = kall[:, pair * LANES:(pair + 1) * LANES]
            vcol = vall[:, pair * LANES:(pair + 1) * LANES]
            self.k_half.append((jnp.where(low, kcol, 0.0).astype(BF16),
                                jnp.where(low, 0.0, kcol).astype(BF16)))
            self.vt_half.append((jnp.where(low, vcol, 0.0).T.astype(BF16),
                                 jnp.where(low, 0.0, vcol).T.astype(BF16)))
        self.outs = {}

    def scores(self, unit):
        sb, pair = unit
        rows = slice(sb * WINDOW, (sb + 1) * WINDOW)
        keys = slice(sb * WINDOW, (sb + 2) * WINDOW)
        kst = jnp.concatenate([self.k_half[pair][0][keys], self.k_half[pair][1][keys]], axis=0)
        qst = jnp.concatenate(
            [self.q_ref[rows, (grp * self.n_pairs + pair) * LANES:
                        (grp * self.n_pairs + pair + 1) * LANES]
             for grp in range(GQA_GROUP)], axis=0)
        return _dot_t(kst, qst)

    def softmax(self, unit, st_all):
        sb, pair = unit
        valid = self.band_first if sb == 0 else self.band
        pt_rows = []
        for half in range(2):
            pt_cols = []
            for grp in range(GQA_GROUP):
                sink = self.sink_ref[(2 * pair + half) * GQA_GROUP + grp]
                st = st_all[half * 2 * WINDOW:(half + 1) * 2 * WINDOW,
                            grp * WINDOW:(grp + 1) * WINDOW]
                st = jnp.where(valid, st, NEG)
                mx = jnp.maximum(jnp.max(st, axis=0, keepdims=True), sink)
                e = jnp.exp(st - mx)
                den = jnp.sum(e, axis=0, keepdims=True) + jnp.exp(sink - mx)
                pt_cols.append((e * (1.0 / den)).astype(BF16))
            pt_rows.append(jnp.concatenate(pt_cols, axis=1))
        return jnp.concatenate(pt_rows, axis=0)

    def values(self, unit, pt):
        sb, pair = unit
        keys = slice(sb * WINDOW, (sb + 2) * WINDOW)
        vst = jnp.concatenate([self.vt_half[pair][0][:, keys], self.vt_half[pair][1][:, keys]],
                              axis=1)
        ot_all = _dot(vst, pt)
        for grp in range(GQA_GROUP):
            self.outs[(sb, grp * self.n_pairs + pair)] = ot_all[:, grp * WINDOW:(grp + 1) * WINDOW]
        if pair == self.n_pairs - 1:
            self._normalise(sb)

    def _normalise(self, sb):
        rows = slice(sb * WINDOW, (sb + 1) * WINDOW)
        cols = range(ATTN_DIM // LANES)
        ssq = sum(jnp.sum(self.outs[(sb, col)] ** 2, axis=0, keepdims=True) for col in cols)
        inv = lax.rsqrt(ssq * (1.0 / ATTN_DIM) + EPS)
        for col in cols:
            sl = slice(col * LANES, (col + 1) * LANES)
            self.o_ref[rows, sl] = ((self.outs.pop((sb, col)) * inv).T
                                    * self.ga_ref[:, sl]).astype(BF16)


def _attn_merge_kernel(sink_ref, q_ref, kp_ref, kc_ref, vp_ref, vc_ref, km_ref, vm_ref, ga_ref,
                       x_ref, c_ref, woa_ref, woc_ref, g_ref, x2_ref, xn_ref,
                       an_new_ref, an_old_ref, *, nsub, tiles_per_seq, n_tiles):
    t = pl.program_id(0)

    @pl.when(t == 0)
    def _():
        an_new_ref[...] = jnp.zeros(an_new_ref.shape, an_new_ref.dtype)

    an_old_ref[...] = an_new_ref[...]

    first = (jnp.minimum(t, n_tiles - 1) % tiles_per_seq) == 0
    attn = _BandedAttention(first, sink_ref, q_ref,
                            jnp.where(first, km_ref[...], kp_ref[...]), kc_ref[...],
                            jnp.where(first, vm_ref[...], vp_ref[...]), vc_ref[...],
                            ga_ref, an_new_ref, nsub)
    units = attn.units
    n_chunks = len(units)
    cw = D_MODEL // n_chunks
    ssq = jnp.zeros((x_ref.shape[0], 1), F32)
    st_next = attn.scores(units[0])
    for u, unit in enumerate(units):
        cols = slice(u * cw, (u + 1) * cw)
        xc = x_ref[:, cols] + (_dot(an_old_ref[...], woa_ref[:, cols]) +
                               _dot(c_ref[...], woc_ref[:, cols]))
        x2_ref[:, cols] = xc
        ssq = ssq + jnp.sum(xc * xc, axis=-1, keepdims=True)
        st_all, st_next = st_next, (attn.scores(units[u + 1]) if u + 1 < n_chunks else None)
        attn.values(unit, attn.softmax(unit, st_all))
    inv = lax.rsqrt(ssq * (1.0 / D_MODEL) + EPS)
    for u in range(n_chunks):
        cols = slice(u * cw, (u + 1) * cw)
        xn_ref[:, cols] = (x2_ref[:, cols] * inv * g_ref[:, cols]).astype(BF16)


def _attn_merge(sinks, q, k, v, kmeta, vmeta, ga, x, c, woa, woc, g, *, seq, nsub):
    m = x.shape[0]
    tm = nsub * WINDOW
    n_tiles = m // tm
    tiles_per_seq = seq // tm
    cur = lambda t: (jnp.minimum(t, n_tiles - 1), 0)
    prev = lambda t: (jnp.maximum(jnp.minimum(t, n_tiles - 1) * nsub - 1, 0), 0)
    lag = lambda t: (jnp.maximum(t - 1, 0), 0)
    return pl.pallas_call(
        functools.partial(_attn_merge_kernel, nsub=nsub, tiles_per_seq=tiles_per_seq,
                          n_tiles=n_tiles),
        grid=(n_tiles + 1,),
        in_specs=[pl.BlockSpec(memory_space=pltpu.SMEM),
                  pl.BlockSpec((tm, ATTN_DIM), cur),
                  pl.BlockSpec((WINDOW, KV_DIM), prev), pl.BlockSpec((tm, KV_DIM), cur),
                  pl.BlockSpec((WINDOW, KV_DIM), prev), pl.BlockSpec((tm, KV_DIM), cur),
                  _resident((WINDOW, KV_DIM)), _resident((WINDOW, KV_DIM)),
                  _resident((1, ATTN_DIM)),
                  pl.BlockSpec((tm, D_MODEL), lag), pl.BlockSpec((tm, CONV_DIM), lag),
                  _resident((ATTN_DIM, D_MODEL)), _resident((CONV_DIM, D_MODEL)),
                  _resident((1, D_MODEL))],
        out_specs=[pl.BlockSpec((tm, D_MODEL), lag), pl.BlockSpec((tm, D_MODEL), lag)],
        out_shape=[jax.ShapeDtypeStruct((m, D_MODEL), F32),
                   jax.ShapeDtypeStruct((m, D_MODEL), BF16)],
        scratch_shapes=[pltpu.VMEM((tm, ATTN_DIM), BF16), pltpu.VMEM((tm, ATTN_DIM), BF16)],
        compiler_params=pltpu.CompilerParams(
            dimension_semantics=("arbitrary",), vmem_limit_bytes=VMEM_LIMIT),
        name="attn_merge_prompt",
    )(sinks, q, k, k, v, v, kmeta, vmeta, ga, x, c, woa, woc, g)


def _attn_sample_kernel(sink_ref, q_ref, kn_ref, vn_ref, ck_ref, cv_ref, ga_ref,
                        ok_ref, ov_ref, an_ref, *, bs):
    n_pairs = N_KV_HEADS // 2
    heads_per_col = 2 * GQA_GROUP
    last_key = lax.broadcasted_iota(jnp.int32, (KV_DIM, WINDOW), 1) == WINDOW - 1
    pad = jnp.zeros((WINDOW - bs, KV_DIM), F32)
    knt = jnp.concatenate([kn_ref[...], pad], axis=0).T
    vnt = jnp.concatenate([vn_ref[...], pad], axis=0).T
    for s in range(bs):
        ok_ref[s] = jnp.where(last_key, knt[:, s:s + 1], pltpu.roll(ck_ref[s], WINDOW - 1, 1))
        ov_ref[s] = jnp.where(last_key, vnt[:, s:s + 1], pltpu.roll(cv_ref[s], WINDOW - 1, 1))

    n_rows = heads_per_col * bs
    row = lax.broadcasted_iota(jnp.int32, (n_rows, bs * WINDOW), 0)
    lane = lax.broadcasted_iota(jnp.int32, (n_rows, bs * WINDOW), 1)
    log2 = lambda n: n.bit_length() - 1
    assert bs == 1 << log2(bs) and WINDOW == 1 << log2(WINDOW)
    same_seq = (row & (bs - 1)) == (lane >> log2(WINDOW))
    head_row = lax.broadcasted_iota(jnp.int32, (n_rows, 1), 0) >> log2(bs)
    low = lax.broadcasted_iota(jnp.int32, (bs, LANES), 1) < HALF

    cols = [None] * (ATTN_DIM // LANES)
    for pair in range(n_pairs):
        sl = slice(pair * LANES, (pair + 1) * LANES)
        pieces = []
        sk = jnp.zeros((n_rows, 1), F32)
        for grp in range(GQA_GROUP):
            col = grp * n_pairs + pair
            qcol = q_ref[:, col * LANES:(col + 1) * LANES]
            pieces += [jnp.where(low, qcol, 0.0), jnp.where(low, 0.0, qcol)]
            for half in range(2):
                sink = sink_ref[(2 * pair + half) * GQA_GROUP + grp]
                sk = jnp.where(head_row == 2 * grp + half, sink, sk)
        lq = jnp.concatenate(pieces, axis=0).astype(BF16)
        kstack = jnp.concatenate([ck_ref[s, sl, :] for s in range(bs)], axis=1).astype(BF16)
        vstack = jnp.concatenate([cv_ref[s, sl, :] for s in range(bs)], axis=1).astype(BF16)
        s = jnp.where(same_seq, _dot(lq, kstack), NEG)
        k_new = jnp.concatenate([kn_ref[:, sl]] * heads_per_col, axis=0).astype(BF16)
        v_new = jnp.concatenate([vn_ref[:, sl]] * heads_per_col, axis=0).astype(BF16)
        s_new = jnp.sum(lq.astype(F32) * k_new.astype(F32), axis=-1, keepdims=True)
        mx = jnp.maximum(jnp.maximum(jnp.max(s, axis=-1, keepdims=True), s_new), sk)
        e = jnp.exp(s - mx)
        e_new = jnp.exp(s_new - mx)
        rden = 1.0 / (jnp.sum(e, axis=-1, keepdims=True) + e_new + jnp.exp(sk - mx))
        p = (e * rden).astype(BF16)
        p_new = (e_new * rden).astype(BF16).astype(F32)
        o = _dot_t(p, vstack) + p_new * v_new.astype(F32)
        for grp in range(GQA_GROUP):
            lo = o[(2 * grp) * bs:(2 * grp + 1) * bs]
            hi = o[(2 * grp + 1) * bs:(2 * grp + 2) * bs]
            cols[grp * n_pairs + pair] = jnp.where(low, lo, hi)
    an_ref[...] = _rms(jnp.concatenate(cols, axis=1), ga_ref[...])


def _attn_sample(sinks, q, kn, vn, ck, cv, ga, *, bs):
    nseq = q.shape[0]
    row = lambda i: (i, 0)
    blk3 = pl.BlockSpec((bs, KV_DIM, WINDOW), lambda i: (i, 0, 0))
    return pl.pallas_call(
        functools.partial(_attn_sample_kernel, bs=bs),
        grid=(nseq // bs,),
        in_specs=[pl.BlockSpec(memory_space=pltpu.SMEM),
                  pl.BlockSpec((bs, ATTN_DIM), row), pl.BlockSpec((bs, KV_DIM), row),
                  pl.BlockSpec((bs, KV_DIM), row), blk3, blk3,
                  pl.BlockSpec((1, ATTN_DIM), lambda i: (0, 0))],
        out_specs=[blk3, blk3, pl.BlockSpec((bs, ATTN_DIM), row)],
        out_shape=[jax.ShapeDtypeStruct(ck.shape, F32), jax.ShapeDtypeStruct(cv.shape, F32),
                   jax.ShapeDtypeStruct((nseq, ATTN_DIM), F32)],
        compiler_params=pltpu.CompilerParams(
            dimension_semantics=("parallel",), vmem_limit_bytes=VMEM_LIMIT),
        name="attn_sample",
    )(sinks, q, kn, vn, ck, cv, ga)


def _group_major(w, axis):
    shape = w.shape
    w = w.reshape(shape[:axis] + (N_KV_HEADS, GQA_GROUP, HEAD_DIM) + shape[axis + 1:])
    return jnp.swapaxes(w, axis, axis + 1).reshape(shape)


def kernel(x_prompt, x_sample, cache_swa_k, cache_swa_v, state_conv, meta_tokens, g_ffn1, w1_gate, w1_up, w1_down, g_mix, w_in, attn_sinks, w_conv, g_attn_out, g_conv_out, w_out, g_ffn2, w2_gate, w2_up, w2_down, g_final):
    assert g_ffn1.shape[0] == 1, "one layer"
    batch, seq, _ = x_prompt.shape
    n_dec = x_sample.shape[0]
    assert x_sample.shape[1] == 1 and cache_swa_k.shape[2] == WINDOW

    row2d = lambda a: a.reshape(1, -1).astype(F32)
    wq = _group_major(w_in[0, :, :ATTN_DIM], 1).astype(BF16)
    wo_a = _group_major(w_out[0, :ATTN_DIM], 0).astype(BF16)
    wo_c = w_out[0, ATTN_DIM:].astype(BF16)
    ga = row2d(_group_major(g_attn_out[0], 0))
    gc = row2d(g_conv_out[0])
    sinks = attn_sinks[0].astype(F32)
    wc = w_conv[0].astype(F32)

    xp = x_prompt.reshape(batch * seq, D_MODEL)
    xt = jnp.concatenate([x_sample[:, 0, :], meta_tokens.astype(x_prompt.dtype)], axis=0)
    n_tail = xt.shape[0]

    xt1, w1g, w1u, w1d = _ffn(xt, row2d(g_ffn1[0]), w1_gate[0], w1_up[0], w1_down[0],
                              tm=n_tail, tf=FFN_TF, name="ffn1_tail")
    xp1, wr = _ffn(xp, row2d(g_ffn1[0]), w1g, w1u, w1d, tm=FFN_TM, tf=FFN_TF,
                   side_cast=(w_in[0], ATTN_DIM), name="ffn1_prompt")

    st = state_conv[0]
    zpad = jnp.zeros((N_META, CONV_DIM), F32)
    h0 = jnp.concatenate([st[:, 0, :], zpad], axis=0)
    h1 = jnp.concatenate([st[:, 1, :], zpad], axis=0)
    qt, kt, vt, ut, cnt = _proj_tail(xt1, row2d(g_mix[0]), wq, wr, wc, gc, h0, h1)
    kmeta = jnp.pad(kt[n_dec:], ((WINDOW - N_META, 0), (0, 0)))
    vmeta = jnp.pad(vt[n_dec:], ((WINDOW - N_META, 0), (0, 0)))
    umeta = ut[n_tail - 8:]

    qp, kp, vp, cnp, cstate, w2g, w2u, w2d = _proj_prompt(
        xp1, row2d(g_mix[0]), wq, wr, wc, gc, umeta, (w2_gate[0], w2_up[0], w2_down[0]),
        tm=PROJ_TM, seq=seq)
    xp2, xpn = _attn_merge(sinks, qp, kp, vp, kmeta, vmeta, ga, xp1, cnp, wo_a, wo_c,
                           row2d(g_ffn2[0]), seq=seq, nsub=ATTN_NSUB)

    to_channel_major = lambda c: jnp.transpose(c[0], (0, 2, 3, 1)).reshape(n_dec, KV_DIM, WINDOW)
    from_channel_major = lambda c: jnp.transpose(
        c.reshape(n_dec, N_KV_HEADS, HEAD_DIM, WINDOW), (0, 3, 1, 2))[None]
    nk, nv, ans = _attn_sample(sinks, qt[:n_dec], kt[:n_dec], vt[:n_dec],
                               to_channel_major(cache_swa_k), to_channel_major(cache_swa_v), ga,
                               bs=SAMPLE_BS)

    xs2, xsn = _merge(xt1[:n_dec], ans, cnt[:n_dec], wo_a, wo_c, row2d(g_ffn2[0]), tm=n_dec,
                      name="merge_tail")
    ys = _ffn(xs2, xsn, w2g, w2u, w2d, tm=n_dec, tf=FFN_TF, g_final=row2d(g_final),
              name="ffn2_tail")
    yp = _ffn(xp2, xpn, w2g, w2u, w2d, tm=FFN_TM, tf=FFN_TF, g_final=row2d(g_final),
              name="ffn2_prompt")

    kv_shape = (1, batch, WINDOW, N_KV_HEADS, HEAD_DIM)
    new_k_prompt = kp.reshape(batch, seq, KV_DIM)[:, seq - WINDOW:].reshape(kv_shape)
    new_v_prompt = vp.reshape(batch, seq, KV_DIM)[:, seq - WINDOW:].reshape(kv_shape)
    new_conv_sample = jnp.stack([st[:, 1, :], ut[:n_dec]], axis=1)[None]
    return (yp.reshape(batch, seq, D_MODEL), ys.reshape(n_dec, 1, D_MODEL),
            new_k_prompt, new_v_prompt, cstate[None],
            from_channel_major(nk), from_channel_major(nv), new_conv_sample)
```

```python
import functools

import jax
import jax.numpy as jnp
from jax import lax
from jax.experimental import pallas as pl
from jax.experimental.pallas import tpu as pltpu

D_MODEL = 2048
D_FF = 5632
N_META = 16
ATTN_DIM = 1024
CONV_DIM = 1024
HEAD_DIM = 64
N_HEADS = 16
N_KV_HEADS = 4
GQA_GROUP = 4
KV_DIM = 256
WINDOW = 128
IN_DIM = ATTN_DIM + 2 * KV_DIM + 3 * CONV_DIM
EPS = 1e-5
NEG = -1e30

LANES = 128
HALF = LANES // 2
VMEM_LIMIT = 60 * 1024 * 1024
FFN_TM, FFN_TF = 1024, 512
FFN_EDGE_ROWS = 256
PROJ_TM = 512
PROJ_ROW_CHUNKS = 2
CAST_TF = 256
ATTN_NSUB = 4
SAMPLE_BS = 16

F32 = jnp.float32
BF16 = jnp.bfloat16


def _rms(x, g):
    return x * lax.rsqrt(jnp.mean(x * x, axis=-1, keepdims=True) + EPS) * g


def _dot(a, b):
    return jnp.dot(a, b, preferred_element_type=F32)


def _dot_t(a, b):
    return lax.dot_general(a, b, (((1,), (1,)), ((), ())), preferred_element_type=F32)


def _resident(shape):
    return pl.BlockSpec(shape, lambda *_: (0,) * len(shape), pipeline_mode=pl.Buffered(1))


def _ffn_kernel(*refs, n_ff, normed_input, has_final, cast_weights, side_chunks, prefetch_x):
    refs = list(refs)
    x_ref, norm_ref, wg_ref, wu_ref, wd_ref = refs[:5]
    refs = refs[5:]
    if has_final:
        gf_ref = refs.pop(0)
    if side_chunks:
        side_src_ref = refs.pop(0)
    out_ref = refs.pop(0)
    if cast_weights:
        wgb_ref, wub_ref, wdb_ref = refs[:3]
        refs = refs[3:]
    if side_chunks:
        side_dst_ref = refs.pop(0)
    xn_ref = norm_ref if normed_input else refs.pop(0)
    i = pl.program_id(0)
    j = pl.program_id(1)

    if prefetch_x:
        x_hbm, x_ref, x_sem = x_ref, refs.pop(0), refs.pop(0)
        tm = x_ref.shape[0]

        def x_copy(tile):
            return pltpu.make_async_copy(x_hbm.at[pl.ds(tile * tm, tm), :], x_ref, x_sem)

        @pl.when(j == 0)
        def _():
            pl.when(i == 0)(lambda: x_copy(0).start())
            x_copy(i).wait()

        pl.when((j == 1) & (i + 1 < pl.num_programs(0)))(lambda: x_copy(i + 1).start())

    if side_chunks:
        @pl.when(pl.program_id(0) * n_ff + j < side_chunks)
        def _():
            side_dst_ref[...] = side_src_ref[...].astype(BF16)

    if cast_weights:
        wgb_ref[...] = wg_ref[...].astype(BF16)
        wub_ref[...] = wu_ref[...].astype(BF16)
        wdb_ref[...] = wd_ref[...].astype(BF16)
        wg_ref, wu_ref, wd_ref = wgb_ref, wub_ref, wdb_ref

    def chunk_step(first, last, chunks):
        cm = out_ref.shape[0] // chunks
        for c in range(chunks):
            rows = slice(c * cm, (c + 1) * cm)
            if first and not normed_input:
                xn = _rms(x_ref[rows, :], norm_ref[...]).astype(BF16)
                xn_ref[rows, :] = xn
            else:
                xn = xn_ref[rows, :]
            gate = _dot(xn, wg_ref[...])
            up = _dot(xn, wu_ref[...])
            h = (gate * jax.nn.sigmoid(gate) * up * 0.5).astype(BF16)
            acc = (x_ref[rows, :] if first else out_ref[rows, :]) + _dot(h, wd_ref[...])
            if last and has_final:
                acc = _rms(acc, gf_ref[...])
            out_ref[rows, :] = acc

    row_chunks = max(1, out_ref.shape[0] // FFN_EDGE_ROWS)
    first_chunks = 1 if normed_input else row_chunks
    last_chunks = row_chunks if has_final else 1
    if n_ff == 1:
        chunk_step(True, True, row_chunks)
        return
    pl.when(j == 0)(lambda: chunk_step(True, False, first_chunks))
    if has_final:
        pl.when((j > 0) & (j < n_ff - 1))(lambda: chunk_step(False, False, 1))
        pl.when(j == n_ff - 1)(lambda: chunk_step(False, True, last_chunks))
    else:
        pl.when(j > 0)(lambda: chunk_step(False, False, 1))


def _ffn(x, norm, wg, wu, wd, *, tm, tf, g_final=None, side_cast=None, name):
    m = x.shape[0]
    assert m % tm == 0 and D_FF % tf == 0
    n_ff = D_FF // tf
    normed_input = norm.shape[0] == m
    cast_weights = wg.dtype == F32
    assert not cast_weights or m == tm
    side_src, side_col0 = side_cast if side_cast is not None else (None, 0)
    side_chunks = 0 if side_src is None else (side_src.shape[1] - side_col0) // CAST_TF
    assert side_chunks <= (m // tm) * n_ff and side_col0 % CAST_TF == 0
    row = lambda i, j: (i, 0)
    side_chunk = lambda i, j: jnp.minimum(i * n_ff + j, side_chunks - 1)
    side_in = pl.BlockSpec((D_MODEL, CAST_TF),
                           lambda i, j: (0, side_col0 // CAST_TF + side_chunk(i, j)))
    side_out = pl.BlockSpec((D_MODEL, CAST_TF), lambda i, j: (0, side_chunk(i, j)))
    w_specs = [pl.BlockSpec((D_MODEL, tf), lambda i, j: (0, j)),
               pl.BlockSpec((D_MODEL, tf), lambda i, j: (0, j)),
               pl.BlockSpec((tf, D_MODEL), lambda i, j: (j, 0))]
    prefetch_x = m // tm > 1 and n_ff > 1
    in_specs = [pl.BlockSpec(memory_space=pl.ANY) if prefetch_x
                else pl.BlockSpec((tm, D_MODEL), row),
                pl.BlockSpec((tm, D_MODEL), row) if normed_input else _resident((1, D_MODEL)),
                *w_specs]
    args = [x, norm, wg, wu, wd]
    if g_final is not None:
        in_specs.append(_resident((1, D_MODEL)))
        args.append(g_final)
    if side_chunks:
        in_specs.append(side_in)
        args.append(side_src)
    out_specs = [pl.BlockSpec((tm, D_MODEL), row)]
    out_shape = [jax.ShapeDtypeStruct((m, D_MODEL), F32)]
    if cast_weights:
        out_specs += w_specs
        out_shape += [jax.ShapeDtypeStruct(w.shape, BF16) for w in (wg, wu, wd)]
    if side_chunks:
        out_specs.append(side_out)
        out_shape.append(jax.ShapeDtypeStruct((D_MODEL, side_chunks * CAST_TF), BF16))
    scratch_shapes = [] if normed_input else [pltpu.VMEM((tm, D_MODEL), BF16)]
    if prefetch_x:
        scratch_shapes += [pltpu.VMEM((tm, D_MODEL), F32), pltpu.SemaphoreType.DMA(())]
    ordered_rows = bool(side_chunks) or prefetch_x
    outs = pl.pallas_call(
        functools.partial(_ffn_kernel, n_ff=n_ff, normed_input=normed_input,
                          has_final=g_final is not None, cast_weights=cast_weights,
                          side_chunks=side_chunks, prefetch_x=prefetch_x),
        grid=(m // tm, n_ff),
        in_specs=in_specs,
        out_specs=out_specs,
        out_shape=out_shape,
        scratch_shapes=scratch_shapes,
        compiler_params=pltpu.CompilerParams(
            dimension_semantics=("arbitrary" if ordered_rows else "parallel", "arbitrary"),
            vmem_limit_bytes=VMEM_LIMIT),
        name=name,
    )(*args)
    return outs if len(outs) > 1 else outs[0]


def _merge_kernel(x_ref, a_ref, c_ref, woa_ref, woc_ref, g_ref, x2_ref, xn_ref):
    x = x_ref[...] + (_dot(a_ref[...].astype(BF16), woa_ref[...]) +
                      _dot(c_ref[...].astype(BF16), woc_ref[...]))
    x2_ref[...] = x
    xn_ref[...] = _rms(x, g_ref[...]).astype(BF16)


def _merge(x, a, c, woa, woc, g, *, tm, name):
    m = x.shape[0]
    assert m % tm == 0
    row = lambda i: (i, 0)
    return pl.pallas_call(
        _merge_kernel,
        grid=(m // tm,),
        in_specs=[pl.BlockSpec((tm, D_MODEL), row), pl.BlockSpec((tm, ATTN_DIM), row),
                  pl.BlockSpec((tm, CONV_DIM), row), _resident((ATTN_DIM, D_MODEL)),
                  _resident((CONV_DIM, D_MODEL)), _resident((1, D_MODEL))],
        out_specs=[pl.BlockSpec((tm, D_MODEL), row), pl.BlockSpec((tm, D_MODEL), row)],
        out_shape=[jax.ShapeDtypeStruct((m, D_MODEL), F32),
                   jax.ShapeDtypeStruct((m, D_MODEL), BF16)],
        compiler_params=pltpu.CompilerParams(
            dimension_semantics=("parallel",), vmem_limit_bytes=VMEM_LIMIT),
        name=name,
    )(x, a, c, woa, woc, g)


K0, V0, B0, C0, H0, R_END = 0, 256, 512, 1536, 2560, 3584


def _project_conv(x_ref, g_ref, wr_ref):
    xn = _rms(x_ref[...], g_ref[...]).astype(BF16)
    gate_b = _dot(xn, wr_ref[:, B0:C0])
    u = _dot(xn, wr_ref[:, C0:H0]) * _dot(xn, wr_ref[:, H0:R_END])
    return xn, gate_b, u


def _project_qkv(xn, wq_ref, wr_ref):
    q = _dot(xn, wq_ref[...]) * (HEAD_DIM ** -0.5)
    return q, _dot(xn, wr_ref[:, K0:V0]), _dot(xn, wr_ref[:, V0:B0])


def _proj_prompt_kernel(x_ref, g_ref, wq_ref, wr_ref, wc_ref, gc_ref, umeta_ref,
                        fg_ref, fu_ref, fd_ref,
                        q_ref, k_ref, v_ref, cn_ref, cst_ref, bg_ref, bu_ref, bd_ref,
                        us_ref, *, tm, tiles_per_seq):
    t = pl.program_id(0) % tiles_per_seq

    @pl.when(t == 0)
    def _():
        us_ref[0:8, :] = umeta_ref[...]

    @pl.when(t != 0)
    def _():
        us_ref[0:8, :] = us_ref[tm:tm + 8, :]

    hm = tm // PROJ_ROW_CHUNKS
    casts = ((fg_ref, bg_ref), (fu_ref, bu_ref), (fd_ref, bd_ref))
    for h in range(PROJ_ROW_CHUNKS):
        rows = slice(h * hm, (h + 1) * hm)
        for src, dst in casts[h::PROJ_ROW_CHUNKS]:
            dst[...] = src[...].astype(BF16)
        xn, gate_b, u = _project_conv(x_ref.at[rows], g_ref, wr_ref)
        us_ref[8 + h * hm:8 + (h + 1) * hm, :] = u
        y = (wc_ref[0:1, :] * us_ref[6 + h * hm:6 + (h + 1) * hm, :]
             + wc_ref[1:2, :] * us_ref[7 + h * hm:7 + (h + 1) * hm, :] + wc_ref[2:3, :] * u)
        cn_ref[rows, :] = _rms(gate_b * y, gc_ref[...]).astype(BF16)
        q, k, v = _project_qkv(xn, wq_ref, wr_ref)
        q_ref[rows, :] = q.astype(BF16)
        k_ref[rows, :] = k
        v_ref[rows, :] = v
    cst_ref[0] = us_ref[tm + 6:tm + 8, :]


def _proj_prompt(x, g, wq, wr, wc, gc, umeta, ffn_w, *, tm, seq):
    m = x.shape[0]
    tiles_per_seq = seq // tm
    n_steps = m // tm
    n_cast = D_FF // CAST_TF
    assert D_FF % CAST_TF == 0 and n_cast <= n_steps
    row = lambda i: (i, 0)
    cast_cols = pl.BlockSpec((D_MODEL, CAST_TF), lambda i: (0, jnp.minimum(i, n_cast - 1)))
    cast_rows = pl.BlockSpec((CAST_TF, D_MODEL), lambda i: (jnp.minimum(i, n_cast - 1), 0))
    return pl.pallas_call(
        functools.partial(_proj_prompt_kernel, tm=tm, tiles_per_seq=tiles_per_seq),
        grid=(n_steps,),
        in_specs=[pl.BlockSpec((tm, D_MODEL), row), _resident((1, D_MODEL)),
                  _resident((D_MODEL, ATTN_DIM)), _resident((D_MODEL, R_END)),
                  _resident((3, CONV_DIM)), _resident((1, CONV_DIM)), _resident((8, CONV_DIM)),
                  cast_cols, cast_cols, cast_rows],
        out_specs=[pl.BlockSpec((tm, ATTN_DIM), row), pl.BlockSpec((tm, KV_DIM), row),
                   pl.BlockSpec((tm, KV_DIM), row), pl.BlockSpec((tm, CONV_DIM), row),
                   pl.BlockSpec((1, 2, CONV_DIM), lambda i: (i // tiles_per_seq, 0, 0)),
                   cast_cols, cast_cols, cast_rows],
        out_shape=[jax.ShapeDtypeStruct((m, ATTN_DIM), BF16),
                   jax.ShapeDtypeStruct((m, KV_DIM), F32),
                   jax.ShapeDtypeStruct((m, KV_DIM), F32),
                   jax.ShapeDtypeStruct((m, CONV_DIM), BF16),
                   jax.ShapeDtypeStruct((m // seq, 2, CONV_DIM), F32)]
                  + [jax.ShapeDtypeStruct(w.shape, BF16) for w in ffn_w],
        scratch_shapes=[pltpu.VMEM((tm + 8, CONV_DIM), F32)],
        compiler_params=pltpu.CompilerParams(
            dimension_semantics=("arbitrary",), vmem_limit_bytes=VMEM_LIMIT),
        name="proj_prompt",
    )(x, g, wq, wr, wc, gc, umeta, *ffn_w)


def _proj_tail_kernel(x_ref, g_ref, wq_ref, wr_ref, wc_ref, gc_ref, h0_ref, h1_ref,
                      q_ref, k_ref, v_ref, u_ref, cn_ref):
    xn, gate_b, u = _project_conv(x_ref, g_ref, wr_ref)
    q, k, v = _project_qkv(xn, wq_ref, wr_ref)
    q_ref[...] = q
    k_ref[...] = k
    v_ref[...] = v
    u_ref[...] = u
    y = wc_ref[0:1, :] * h0_ref[...] + wc_ref[1:2, :] * h1_ref[...] + wc_ref[2:3, :] * u
    cn_ref[...] = _rms(gate_b * y, gc_ref[...]).astype(BF16)


def _proj_tail(x, g, wq, wr, wc, gc, h0, h1):
    m = x.shape[0]
    full = lambda shape: pl.BlockSpec(shape, lambda i: (0,) * len(shape))
    return pl.pallas_call(
        _proj_tail_kernel,
        grid=(1,),
        in_specs=[full((m, D_MODEL)), full((1, D_MODEL)), _resident((D_MODEL, ATTN_DIM)),
                  _resident((D_MODEL, R_END)),
                  full((3, CONV_DIM)), full((1, CONV_DIM)), full((m, CONV_DIM)),
                  full((m, CONV_DIM))],
        out_specs=[full((m, ATTN_DIM)), full((m, KV_DIM)), full((m, KV_DIM)),
                   full((m, CONV_DIM)), full((m, CONV_DIM))],
        out_shape=[jax.ShapeDtypeStruct((m, ATTN_DIM), F32),
                   jax.ShapeDtypeStruct((m, KV_DIM), F32),
                   jax.ShapeDtypeStruct((m, KV_DIM), F32),
                   jax.ShapeDtypeStruct((m, CONV_DIM), F32),
                   jax.ShapeDtypeStruct((m, CONV_DIM), BF16)],
        compiler_params=pltpu.CompilerParams(
            dimension_semantics=("arbitrary",), vmem_limit_bytes=VMEM_LIMIT),
        name="proj_tail",
    )(x, g, wq, wr, wc, gc, h0, h1)


class _BandedAttention:
    def __init__(self, first, sink_ref, q_ref, kprev, kcur, vprev, vcur, ga_ref, o_ref, nsub):
        self.sink_ref, self.q_ref, self.ga_ref, self.o_ref = sink_ref, q_ref, ga_ref, o_ref
        self.n_pairs = N_KV_HEADS // 2
        self.units = [(sb, pair) for sb in range(nsub) for pair in range(self.n_pairs)]
        kall = jnp.concatenate([kprev, kcur], axis=0)
        vall = jnp.concatenate([vprev, vcur], axis=0)
        c = lax.broadcasted_iota(jnp.int32, (2 * WINDOW, WINDOW), 0)
        r = lax.broadcasted_iota(jnp.int32, (2 * WINDOW, WINDOW), 1)
        self.band = (c >= r) & (c <= r + WINDOW)
        self.band_first = self.band & ((c >= WINDOW - N_META) | jnp.logical_not(first))
        low = lax.broadcasted_iota(jnp.int32, kall.shape[:1] + (LANES,), 1) < HALF
        self.k_half, self.vt_half = [], []
        for pair in range(self.n_pairs):
            kcol = kall[:, pair * LANES:(pair + 1) * LANES]
            vcol = vall[:, pair * LANES:(pair + 1) * LANES]
            self.k_half.append((jnp.where(low, kcol, 0.0).astype(BF16),
                                jnp.where(low, 0.0, kcol).astype(BF16)))
            self.vt_half.append((jnp.where(low, vcol, 0.0).T.astype(BF16),
                                 jnp.where(low, 0.0, vcol).T.astype(BF16)))
        self.outs = {}

    def scores(self, unit):
        sb, pair = unit
        rows = slice(sb * WINDOW, (sb + 1) * WINDOW)
        keys = slice(sb * WINDOW, (sb + 2) * WINDOW)
        kst = jnp.concatenate([self.k_half[pair][0][keys], self.k_half[pair][1][keys]], axis=0)
        qst = jnp.concatenate(
            [self.q_ref[rows, (grp * self.n_pairs + pair) * LANES:
                        (grp * self.n_pairs + pair + 1) * LANES]
             for grp in range(GQA_GROUP)], axis=0)
        return _dot_t(kst, qst)

    def softmax(self, unit, st_all):
        sb, pair = unit
        valid = self.band_first if sb == 0 else self.band
        pt_rows = []
        for half in range(2):
            pt_cols = []
            for grp in range(GQA_GROUP):
                sink = self.sink_ref[(2 * pair + half) * GQA_GROUP + grp]
                st = st_all[half * 2 * WINDOW:(half + 1) * 2 * WINDOW,
                            grp * WINDOW:(grp + 1) * WINDOW]
                st = jnp.where(valid, st, NEG)
                mx = jnp.maximum(jnp.max(st, axis=0, keepdims=True), sink)
                e = jnp.exp(st - mx)
                den = jnp.sum(e, axis=0, keepdims=True) + jnp.exp(sink - mx)
                pt_cols.append((e * (1.0 / den)).astype(BF16))
            pt_rows.append(jnp.concatenate(pt_cols, axis=1))
        return jnp.concatenate(pt_rows, axis=0)

    def values(self, unit, pt):
        sb, pair = unit
        keys = slice(sb * WINDOW, (sb + 2) * WINDOW)
        vst = jnp.concatenate([self.vt_half[pair][0][:, keys], self.vt_half[pair][1][:, keys]],
                              axis=1)
        ot_all = _dot(vst, pt)
        for grp in range(GQA_GROUP):
            self.outs[(sb, grp * self.n_pairs + pair)] = ot_all[:, grp * WINDOW:(grp + 1) * WINDOW]
        if pair == self.n_pairs - 1:
            self._normalise(sb)

    def _normalise(self, sb):
        rows = slice(sb * WINDOW, (sb + 1) * WINDOW)
        cols = range(ATTN_DIM // LANES)
        ssq = sum(jnp.sum(self.outs[(sb, col)] ** 2, axis=0, keepdims=True) for col in cols)
        inv = lax.rsqrt(ssq * (1.0 / ATTN_DIM) + EPS)
        for col in cols:
            sl = slice(col * LANES, (col + 1) * LANES)
            self.o_ref[rows, sl] = ((self.outs.pop((sb, col)) * inv).T
                                    * self.ga_ref[:, sl]).astype(BF16)


def _attn_merge_kernel(sink_ref, q_ref, kp_ref, kc_ref, vp_ref, vc_ref, km_ref, vm_ref, ga_ref,
                       x_ref, c_ref, woa_ref, woc_ref, g_ref, x2_ref, xn_ref,
                       an_new_ref, an_old_ref, *, nsub, tiles_per_seq, n_tiles):
    t = pl.program_id(0)

    @pl.when(t == 0)
    def _():
        an_new_ref[...] = jnp.zeros(an_new_ref.shape, an_new_ref.dtype)

    an_old_ref[...] = an_new_ref[...]

    first = (jnp.minimum(t, n_tiles - 1) % tiles_per_seq) == 0
    attn = _BandedAttention(first, sink_ref, q_ref,
                            jnp.where(first, km_ref[...], kp_ref[...]), kc_ref[...],
                            jnp.where(first, vm_ref[...], vp_ref[...]), vc_ref[...],
                            ga_ref, an_new_ref, nsub)
    units = attn.units
    n_chunks = len(units)
    cw = D_MODEL // n_chunks
    ssq = jnp.zeros((x_ref.shape[0], 1), F32)
    st_next = attn.scores(units[0])
    for u, unit in enumerate(units):
        cols = slice(u * cw, (u + 1) * cw)
        xc = x_ref[:, cols] + (_dot(an_old_ref[...], woa_ref[:, cols]) +
                               _dot(c_ref[...], woc_ref[:, cols]))
        x2_ref[:, cols] = xc
        ssq = ssq + jnp.sum(xc * xc, axis=-1, keepdims=True)
        st_all, st_next = st_next, (attn.scores(units[u + 1]) if u + 1 < n_chunks else None)
        attn.values(unit, attn.softmax(unit, st_all))
    inv = lax.rsqrt(ssq * (1.0 / D_MODEL) + EPS)
    for u in range(n_chunks):
        cols = slice(u * cw, (u + 1) * cw)
        xn_ref[:, cols] = (x2_ref[:, cols] * inv * g_ref[:, cols]).astype(BF16)


def _attn_merge(sinks, q, k, v, kmeta, vmeta, ga, x, c, woa, woc, g, *, seq, nsub):
    m = x.shape[0]
    tm = nsub * WINDOW
    n_tiles = m // tm
    tiles_per_seq = seq // tm
    cur = lambda t: (jnp.minimum(t, n_tiles - 1), 0)
    prev = lambda t: (jnp.maximum(jnp.minimum(t, n_tiles - 1) * nsub - 1, 0), 0)
    lag = lambda t: (jnp.maximum(t - 1, 0), 0)
    return pl.pallas_call(
        functools.partial(_attn_merge_kernel, nsub=nsub, tiles_per_seq=tiles_per_seq,
                          n_tiles=n_tiles),
        grid=(n_tiles + 1,),
        in_specs=[pl.BlockSpec(memory_space=pltpu.SMEM),
                  pl.BlockSpec((tm, ATTN_DIM), cur),
                  pl.BlockSpec((WINDOW, KV_DIM), prev), pl.BlockSpec((tm, KV_DIM), cur),
                  pl.BlockSpec((WINDOW, KV_DIM), prev), pl.BlockSpec((tm, KV_DIM), cur),
                  _resident((WINDOW, KV_DIM)), _resident((WINDOW, KV_DIM)),
                  _resident((1, ATTN_DIM)),
                  pl.BlockSpec((tm, D_MODEL), lag), pl.BlockSpec((tm, CONV_DIM), lag),
                  _resident((ATTN_DIM, D_MODEL)), _resident((CONV_DIM, D_MODEL)),
                  _resident((1, D_MODEL))],
        out_specs=[pl.BlockSpec((tm, D_MODEL), lag), pl.BlockSpec((tm, D_MODEL), lag)],
        out_shape=[jax.ShapeDtypeStruct((m, D_MODEL), F32),
                   jax.ShapeDtypeStruct((m, D_MODEL), BF16)],
        scratch_shapes=[pltpu.VMEM((tm, ATTN_DIM), BF16), pltpu.VMEM((tm, ATTN_DIM), BF16)],
        compiler_params=pltpu.CompilerParams(
            dimension_semantics=("arbitrary",), vmem_limit_bytes=VMEM_LIMIT),
        name="attn_merge_prompt",
    )(sinks, q, k, k, v, v, kmeta, vmeta, ga, x, c, woa, woc, g)


def _attn_sample_kernel(sink_ref, q_ref, kn_ref, vn_ref, ck_ref, cv_ref, ga_ref,
                        ok_ref, ov_ref, an_ref, *, bs):
    n_pairs = N_KV_HEADS // 2
    heads_per_col = 2 * GQA_GROUP
    last_key = lax.broadcasted_iota(jnp.int32, (KV_DIM, WINDOW), 1) == WINDOW - 1
    pad = jnp.zeros((WINDOW - bs, KV_DIM), F32)
    knt = jnp.concatenate([kn_ref[...], pad], axis=0).T
    vnt = jnp.concatenate([vn_ref[...], pad], axis=0).T
    for s in range(bs):
        ok_ref[s] = jnp.where(last_key, knt[:, s:s + 1], pltpu.roll(ck_ref[s], WINDOW - 1, 1))
        ov_ref[s] = jnp.where(last_key, vnt[:, s:s + 1], pltpu.roll(cv_ref[s], WINDOW - 1, 1))

    n_rows = heads_per_col * bs
    row = lax.broadcasted_iota(jnp.int32, (n_rows, bs * WINDOW), 0)
    lane = lax.broadcasted_iota(jnp.int32, (n_rows, bs * WINDOW), 1)
    log2 = lambda n: n.bit_length() - 1
    assert bs == 1 << log2(bs) and WINDOW == 1 << log2(WINDOW)
    same_seq = (row & (bs - 1)) == (lane >> log2(WINDOW))
    head_row = lax.broadcasted_iota(jnp.int32, (n_rows, 1), 0) >> log2(bs)
    low = lax.broadcasted_iota(jnp.int32, (bs, LANES), 1) < HALF

    cols = [None] * (ATTN_DIM // LANES)
    for pair in range(n_pairs):
        sl = slice(pair * LANES, (pair + 1) * LANES)
        pieces = []
        sk = jnp.zeros((n_rows, 1), F32)
        for grp in range(GQA_GROUP):
            col = grp * n_pairs + pair
            qcol = q_ref[:, col * LANES:(col + 1) * LANES]
            pieces += [jnp.where(low, qcol, 0.0), jnp.where(low, 0.0, qcol)]
            for half in range(2):
                sink = sink_ref[(2 * pair + half) * GQA_GROUP + grp]
                sk = jnp.where(head_row == 2 * grp + half, sink, sk)
        lq = jnp.concatenate(pieces, axis=0).astype(BF16)
        kstack = jnp.concatenate([ck_ref[s, sl, :] for s in range(bs)], axis=1).astype(BF16)
        vstack = jnp.concatenate([cv_ref[s, sl, :] for s in range(bs)], axis=1).astype(BF16)
        s = jnp.where(same_seq, _dot(lq, kstack), NEG)
        k_new = jnp.concatenate([kn_ref[:, sl]] * heads_per_col, axis=0).astype(BF16)
        v_new = jnp.concatenate([vn_ref[:, sl]] * heads_per_col, axis=0).astype(BF16)
        s_new = jnp.sum(lq.astype(F32) * k_new.astype(F32), axis=-1, keepdims=True)
        mx = jnp.maximum(jnp.maximum(jnp.max(s, axis=-1, keepdims=True), s_new), sk)
        e = jnp.exp(s - mx)
        e_new = jnp.exp(s_new - mx)
        rden = 1.0 / (jnp.sum(e, axis=-1, keepdims=True) + e_new + jnp.exp(sk - mx))
        p = (e * rden).astype(BF16)
        p_new = (e_new * rden).astype(BF16).astype(F32)
        o = _dot_t(p, vstack) + p_new * v_new.astype(F32)
        for grp in range(GQA_GROUP):
            lo = o[(2 * grp) * bs:(2 * grp + 1) * bs]
            hi = o[(2 * grp + 1) * bs:(2 * grp + 2) * bs]
            cols[grp * n_pairs + pair] = jnp.where(low, lo, hi)
    an_ref[...] = _rms(jnp.concatenate(cols, axis=1), ga_ref[...])


def _attn_sample(sinks, q, kn, vn, ck, cv, ga, *, bs):
    nseq = q.shape[0]
    row = lambda i: (i, 0)
    blk3 = pl.BlockSpec((bs, KV_DIM, WINDOW), lambda i: (i, 0, 0))
    return pl.pallas_call(
        functools.partial(_attn_sample_kernel, bs=bs),
        grid=(nseq // bs,),
        in_specs=[pl.BlockSpec(memory_space=pltpu.SMEM),
                  pl.BlockSpec((bs, ATTN_DIM), row), pl.BlockSpec((bs, KV_DIM), row),
                  pl.BlockSpec((bs, KV_DIM), row), blk3, blk3,
                  pl.BlockSpec((1, ATTN_DIM), lambda i: (0, 0))],
        out_specs=[blk3, blk3, pl.BlockSpec((bs, ATTN_DIM), row)],
        out_shape=[jax.ShapeDtypeStruct(ck.shape, F32), jax.ShapeDtypeStruct(cv.shape, F32),
                   jax.ShapeDtypeStruct((nseq, ATTN_DIM), F32)],
        compiler_params=pltpu.CompilerParams(
            dimension_semantics=("parallel",), vmem_limit_bytes=VMEM_LIMIT),
        name="attn_sample",
    )(sinks, q, kn, vn, ck, cv, ga)


def _group_major(w, axis):
    shape = w.shape
    w = w.reshape(shape[:axis] + (N_KV_HEADS, GQA_GROUP, HEAD_DIM) + shape[axis + 1:])
    return jnp.swapaxes(w, axis, axis + 1).reshape(shape)


def kernel(x_prompt, x_sample, cache_swa_k, cache_swa_v, state_conv, meta_tokens, g_ffn1, w1_gate, w1_up, w1_down, g_mix, w_in, attn_sinks, w_conv, g_attn_out, g_conv_out, w_out, g_ffn2, w2_gate, w2_up, w2_down, g_final):
    assert g_ffn1.shape[0] == 1, "one layer"
    batch, seq, _ = x_prompt.shape
    n_dec = x_sample.shape[0]
    assert x_sample.shape[1] == 1 and cache_swa_k.shape[2] == WINDOW

    row2d = lambda a: a.reshape(1, -1).astype(F32)
    wq = _group_major(w_in[0, :, :ATTN_DIM], 1).astype(BF16)
    wo_a = _group_major(w_out[0, :ATTN_DIM], 0).astype(BF16)
    wo_c = w_out[0, ATTN_DIM:].astype(BF16)
    ga = row2d(_group_major(g_attn_out[0], 0))
    gc = row2d(g_conv_out[0])
    sinks = attn_sinks[0].astype(F32)
    wc = w_conv[0].astype(F32)

    xp = x_prompt.reshape(batch * seq, D_MODEL)
    xt = jnp.concatenate([x_sample[:, 0, :], meta_tokens.astype(x_prompt.dtype)], axis=0)
    n_tail = xt.shape[0]

    xt1, w1g, w1u, w1d = _ffn(xt, row2d(g_ffn1[0]), w1_gate[0], w1_up[0], w1_down[0],
                              tm=n_tail, tf=FFN_TF, name="ffn1_tail")
    xp1, wr = _ffn(xp, row2d(g_ffn1[0]), w1g, w1u, w1d, tm=FFN_TM, tf=FFN_TF,
                   side_cast=(w_in[0], ATTN_DIM), name="ffn1_prompt")

    st = state_conv[0]
    zpad = jnp.zeros((N_META, CONV_DIM), F32)
    h0 = jnp.concatenate([st[:, 0, :], zpad], axis=0)
    h1 = jnp.concatenate([st[:, 1, :], zpad], axis=0)
    qt, kt, vt, ut, cnt = _proj_tail(xt1, row2d(g_mix[0]), wq, wr, wc, gc, h0, h1)
    kmeta = jnp.pad(kt[n_dec:], ((WINDOW - N_META, 0), (0, 0)))
    vmeta = jnp.pad(vt[n_dec:], ((WINDOW - N_META, 0), (0, 0)))
    umeta = ut[n_tail - 8:]

    qp, kp, vp, cnp, cstate, w2g, w2u, w2d = _proj_prompt(
        xp1, row2d(g_mix[0]), wq, wr, wc, gc, umeta, (w2_gate[0], w2_up[0], w2_down[0]),
        tm=PROJ_TM, seq=seq)
    xp2, xpn = _attn_merge(sinks, qp, kp, vp, kmeta, vmeta, ga, xp1, cnp, wo_a, wo_c,
                           row2d(g_ffn2[0]), seq=seq, nsub=ATTN_NSUB)

    to_channel_major = lambda c: jnp.transpose(c[0], (0, 2, 3, 1)).reshape(n_dec, KV_DIM, WINDOW)
    from_channel_major = lambda c: jnp.transpose(
        c.reshape(n_dec, N_KV_HEADS, HEAD_DIM, WINDOW), (0, 3, 1, 2))[None]
    nk, nv, ans = _attn_sample(sinks, qt[:n_dec], kt[:n_dec], vt[:n_dec],
                               to_channel_major(cache_swa_k), to_channel_major(cache_swa_v), ga,
                               bs=SAMPLE_BS)

    xs2, xsn = _merge(xt1[:n_dec], ans, cnt[:n_dec], wo_a, wo_c, row2d(g_ffn2[0]), tm=n_dec,
                      name="merge_tail")
    ys = _ffn(xs2, xsn, w2g, w2u, w2d, tm=n_dec, tf=FFN_TF, g_final=row2d(g_final),
              name="ffn2_tail")
    yp = _ffn(xp2, xpn, w2g, w2u, w2d, tm=FFN_TM, tf=FFN_TF, g_final=row2d(g_final),
              name="ffn2_prompt")

    kv_shape = (1, batch, WINDOW, N_KV_HEADS, HEAD_DIM)
    new_k_prompt = kp.reshape(batch, seq, KV_DIM)[:, seq - WINDOW:].reshape(kv_shape)
    new_v_prompt = vp.reshape(batch, seq, KV_DIM)[:, seq - WINDOW:].reshape(kv_shape)
    new_conv_sample = jnp.stack([st[:, 1, :], ut[:n_dec]], axis=1)[None]
    return (yp.reshape(batch, seq, D_MODEL), ys.reshape(n_dec, 1, D_MODEL),
            new_k_prompt, new_v_prompt, cstate[None],
            from_channel_major(nk), from_channel_major(nv), new_conv_sample)
```

```python
import functools

import jax
import jax.numpy as jnp
from jax import lax
from jax.experimental import pallas as pl
from jax.experimental.pallas import tpu as pltpu

D_MODEL = 2048
D_FF = 5632
N_META = 16
ATTN_DIM = 1024
CONV_DIM = 1024
HEAD_DIM = 64
N_HEADS = 16
N_KV_HEADS = 4
GQA_GROUP = 4
KV_DIM = 256
WINDOW = 128
IN_DIM = ATTN_DIM + 2 * KV_DIM + 3 * CONV_DIM
EPS = 1e-5
NEG = -1e30

LANES = 128
HALF = LANES // 2
VMEM_LIMIT = 60 * 1024 * 1024
FFN_TM, FFN_TF = 1024, 512
FFN_EDGE_ROWS = 256
PROJ_TM = 512
PROJ_ROW_CHUNKS = 2
CAST_TF = 256
ATTN_NSUB = 4
SAMPLE_BS = 16

F32 = jnp.float32
BF16 = jnp.bfloat16


def _rms(x, g):
    return x * lax.rsqrt(jnp.mean(x * x, axis=-1, keepdims=True) + EPS) * g


def _dot(a, b):
    return jnp.dot(a, b, preferred_element_type=F32)


def _dot_t(a, b):
    return lax.dot_general(a, b, (((1,), (1,)), ((), ())), preferred_element_type=F32)


def _resident(shape):
    return pl.BlockSpec(shape, lambda *_: (0,) * len(shape), pipeline_mode=pl.Buffered(1))


def _ffn_kernel(*refs, n_ff, normed_input, has_final, cast_weights, side_chunks, prefetch_x):
    refs = list(refs)
    x_ref, norm_ref, wg_ref, wu_ref, wd_ref = refs[:5]
    refs = refs[5:]
    if has_final:
        gf_ref = refs.pop(0)
    if side_chunks:
        side_src_ref = refs.pop(0)
    out_ref = refs.pop(0)
    if cast_weights:
        wgb_ref, wub_ref, wdb_ref = refs[:3]
        refs = refs[3:]
    if side_chunks:
        side_dst_ref = refs.pop(0)
    xn_ref = norm_ref if normed_input else refs.pop(0)
    i = pl.program_id(0)
    j = pl.program_id(1)

    if prefetch_x:
        x_hbm, x_ref, x_sem = x_ref, refs.pop(0), refs.pop(0)
        tm = x_ref.shape[0]

        def x_copy(tile):
            return pltpu.make_async_copy(x_hbm.at[pl.ds(tile * tm, tm), :], x_ref, x_sem)

        @pl.when(j == 0)
        def _():
            pl.when(i == 0)(lambda: x_copy(0).start())
            x_copy(i).wait()

        pl.when((j == 1) & (i + 1 < pl.num_programs(0)))(lambda: x_copy(i + 1).start())

    if side_chunks:
        @pl.when(pl.program_id(0) * n_ff + j < side_chunks)
        def _():
            side_dst_ref[...] = side_src_ref[...].astype(BF16)

    if cast_weights:
        wgb_ref[...] = wg_ref[...].astype(BF16)
        wub_ref[...] = wu_ref[...].astype(BF16)
        wdb_ref[...] = wd_ref[...].astype(BF16)
        wg_ref, wu_ref, wd_ref = wgb_ref, wub_ref, wdb_ref

    def chunk_step(first, last, chunks):
        cm = out_ref.shape[0] // chunks
        for c in range(chunks):
            rows = slice(c * cm, (c + 1) * cm)
            if first and not normed_input:
                xn = _rms(x_ref[rows, :], norm_ref[...]).astype(BF16)
                xn_ref[rows, :] = xn
            else:
                xn = xn_ref[rows, :]
            gate = _dot(xn, wg_ref[...])
            up = _dot(xn, wu_ref[...])
            h = (gate * jax.nn.sigmoid(gate) * up * 0.5).astype(BF16)
            acc = (x_ref[rows, :] if first else out_ref[rows, :]) + _dot(h, wd_ref[...])
            if last and has_final:
                acc = _rms(acc, gf_ref[...])
            out_ref[rows, :] = acc

    row_chunks = max(1, out_ref.shape[0] // FFN_EDGE_ROWS)
    first_chunks = 1 if normed_input else row_chunks
    last_chunks = row_chunks if has_final else 1
    if n_ff == 1:
        chunk_step(True, True, row_chunks)
        return
    pl.when(j == 0)(lambda: chunk_step(True, False, first_chunks))
    if has_final:
        pl.when((j > 0) & (j < n_ff - 1))(lambda: chunk_step(False, False, 1))
        pl.when(j == n_ff - 1)(lambda: chunk_step(False, True, last_chunks))
    else:
        pl.when(j > 0)(lambda: chunk_step(False, False, 1))


def _ffn(x, norm, wg, wu, wd, *, tm, tf, g_final=None, side_cast=None, name):
    m = x.shape[0]
    assert m % tm == 0 and D_FF % tf == 0
    n_ff = D_FF // tf
    normed_input = norm.shape[0] == m
    cast_weights = wg.dtype == F32
    assert not cast_weights or m == tm
    side_src, side_col0 = side_cast if side_cast is not None else (None, 0)
    side_chunks = 0 if side_src is None else (side_src.shape[1] - side_col0) // CAST_TF
    assert side_chunks <= (m // tm) * n_ff and side_col0 % CAST_TF == 0
    row = lambda i, j: (i, 0)
    side_chunk = lambda i, j: jnp.minimum(i * n_ff + j, side_chunks - 1)
    side_in = pl.BlockSpec((D_MODEL, CAST_TF),
                           lambda i, j: (0, side_col0 // CAST_TF + side_chunk(i, j)))
    side_out = pl.BlockSpec((D_MODEL, CAST_TF), lambda i, j: (0, side_chunk(i, j)))
    wd_spec = pl.BlockSpec((tf, D_MODEL), lambda i, j: (j, 0))
    w_f32_specs = [pl.BlockSpec((D_MODEL, tf), lambda i, j: (0, j))] * 2 + [wd_spec]
    w_bf16_specs = [pl.BlockSpec((None, D_MODEL, tf), lambda i, j: (j, 0, 0))] * 2 + [wd_spec]
    prefetch_x = m // tm > 1 and n_ff > 1
    in_specs = [pl.BlockSpec(memory_space=pl.ANY) if prefetch_x
                else pl.BlockSpec((tm, D_MODEL), row),
                pl.BlockSpec((tm, D_MODEL), row) if normed_input else _resident((1, D_MODEL)),
                *(w_f32_specs if cast_weights else w_bf16_specs)]
    args = [x, norm, wg, wu, wd]
    if g_final is not None:
        in_specs.append(_resident((1, D_MODEL)))
        args.append(g_final)
    if side_chunks:
        in_specs.append(side_in)
        args.append(side_src)
    out_specs = [pl.BlockSpec((tm, D_MODEL), row)]
    out_shape = [jax.ShapeDtypeStruct((m, D_MODEL), F32)]
    if cast_weights:
        out_specs += w_bf16_specs
        out_shape += [jax.ShapeDtypeStruct((n_ff, D_MODEL, tf), BF16)] * 2
        out_shape += [jax.ShapeDtypeStruct(wd.shape, BF16)]
    if side_chunks:
        out_specs.append(side_out)
        out_shape.append(jax.ShapeDtypeStruct((D_MODEL, side_chunks * CAST_TF), BF16))
    scratch_shapes = [] if normed_input else [pltpu.VMEM((tm, D_MODEL), BF16)]
    if prefetch_x:
        scratch_shapes += [pltpu.VMEM((tm, D_MODEL), F32), pltpu.SemaphoreType.DMA(())]
    ordered_rows = bool(side_chunks) or prefetch_x
    outs = pl.pallas_call(
        functools.partial(_ffn_kernel, n_ff=n_ff, normed_input=normed_input,
                          has_final=g_final is not None, cast_weights=cast_weights,
                          side_chunks=side_chunks, prefetch_x=prefetch_x),
        grid=(m // tm, n_ff),
        in_specs=in_specs,
        out_specs=out_specs,
        out_shape=out_shape,
        scratch_shapes=scratch_shapes,
        compiler_params=pltpu.CompilerParams(
            dimension_semantics=("arbitrary" if ordered_rows else "parallel", "arbitrary"),
            vmem_limit_bytes=VMEM_LIMIT),
        name=name,
    )(*args)
    return outs if len(outs) > 1 else outs[0]


def _merge_kernel(x_ref, a_ref, c_ref, woa_ref, woc_ref, g_ref, x2_ref, xn_ref):
    x = x_ref[...] + (_dot(a_ref[...].astype(BF16), woa_ref[...]) +
                      _dot(c_ref[...].astype(BF16), woc_ref[...]))
    x2_ref[...] = x
    xn_ref[...] = _rms(x, g_ref[...]).astype(BF16)


def _merge(x, a, c, woa, woc, g, *, tm, name):
    m = x.shape[0]
    assert m % tm == 0
    row = lambda i: (i, 0)
    return pl.pallas_call(
        _merge_kernel,
        grid=(m // tm,),
        in_specs=[pl.BlockSpec((tm, D_MODEL), row), pl.BlockSpec((tm, ATTN_DIM), row),
                  pl.BlockSpec((tm, CONV_DIM), row), _resident((ATTN_DIM, D_MODEL)),
                  _resident((CONV_DIM, D_MODEL)), _resident((1, D_MODEL))],
        out_specs=[pl.BlockSpec((tm, D_MODEL), row), pl.BlockSpec((tm, D_MODEL), row)],
        out_shape=[jax.ShapeDtypeStruct((m, D_MODEL), F32),
                   jax.ShapeDtypeStruct((m, D_MODEL), BF16)],
        compiler_params=pltpu.CompilerParams(
            dimension_semantics=("parallel",), vmem_limit_bytes=VMEM_LIMIT),
        name=name,
    )(x, a, c, woa, woc, g)


K0, V0, B0, C0, H0, R_END = 0, 256, 512, 1536, 2560, 3584


def _project_conv(x_ref, g_ref, wr_ref):
    xn = _rms(x_ref[...], g_ref[...]).astype(BF16)
    gate_b = _dot(xn, wr_ref[:, B0:C0])
    u = _dot(xn, wr_ref[:, C0:H0]) * _dot(xn, wr_ref[:, H0:R_END])
    return xn, gate_b, u


def _project_qkv(xn, wq_ref, wr_ref):
    q = _dot(xn, wq_ref[...]) * (HEAD_DIM ** -0.5)
    return q, _dot(xn, wr_ref[:, K0:V0]), _dot(xn, wr_ref[:, V0:B0])


def _proj_prompt_kernel(x_ref, g_ref, wq_ref, wr_ref, wc_ref, gc_ref, umeta_ref,
                        fg_ref, fu_ref, fd_ref,
                        q_ref, k_ref, v_ref, cn_ref, cst_ref, bg_ref, bu_ref, bd_ref,
                        us_ref, *, tm, tiles_per_seq):
    t = pl.program_id(0) % tiles_per_seq

    @pl.when(t == 0)
    def _():
        us_ref[0:8, :] = umeta_ref[...]

    @pl.when(t != 0)
    def _():
        us_ref[0:8, :] = us_ref[tm:tm + 8, :]

    hm = tm // PROJ_ROW_CHUNKS
    casts = ((fg_ref, bg_ref), (fu_ref, bu_ref), (fd_ref, bd_ref))
    for h in range(PROJ_ROW_CHUNKS):
        rows = slice(h * hm, (h + 1) * hm)
        for src, dst in casts[h::PROJ_ROW_CHUNKS]:
            dst[...] = src[...].astype(BF16)
        xn, gate_b, u = _project_conv(x_ref.at[rows], g_ref, wr_ref)
        us_ref[8 + h * hm:8 + (h + 1) * hm, :] = u
        y = (wc_ref[0:1, :] * us_ref[6 + h * hm:6 + (h + 1) * hm, :]
             + wc_ref[1:2, :] * us_ref[7 + h * hm:7 + (h + 1) * hm, :] + wc_ref[2:3, :] * u)
        cn_ref[rows, :] = _rms(gate_b * y, gc_ref[...]).astype(BF16)
        q, k, v = _project_qkv(xn, wq_ref, wr_ref)
        q_ref[rows, :] = q.astype(BF16)
        k_ref[rows, :] = k
        v_ref[rows, :] = v
    cst_ref[0] = us_ref[tm + 6:tm + 8, :]


def _proj_prompt(x, g, wq, wr, wc, gc, umeta, ffn_w, *, tm, seq):
    m = x.shape[0]
    tiles_per_seq = seq // tm
    n_steps = m // tm
    n_cast = D_FF // CAST_TF
    assert D_FF % CAST_TF == 0 and n_cast <= n_steps
    row = lambda i: (i, 0)
    chunk = lambda i: jnp.minimum(i, n_cast - 1)
    cast_cols = pl.BlockSpec((D_MODEL, CAST_TF), lambda i: (0, chunk(i)))
    cast_rows = pl.BlockSpec((CAST_TF, D_MODEL), lambda i: (chunk(i), 0))
    per_ff = FFN_TF // CAST_TF
    assert FFN_TF % CAST_TF == 0
    cast_cols_out = pl.BlockSpec((None, D_MODEL, CAST_TF),
                                 lambda i: (chunk(i) // per_ff, 0, chunk(i) % per_ff))
    chunked = jax.ShapeDtypeStruct((D_FF // FFN_TF, D_MODEL, FFN_TF), BF16)
    return pl.pallas_call(
        functools.partial(_proj_prompt_kernel, tm=tm, tiles_per_seq=tiles_per_seq),
        grid=(n_steps,),
        in_specs=[pl.BlockSpec((tm, D_MODEL), row), _resident((1, D_MODEL)),
                  _resident((D_MODEL, ATTN_DIM)), _resident((D_MODEL, R_END)),
                  _resident((3, CONV_DIM)), _resident((1, CONV_DIM)), _resident((8, CONV_DIM)),
                  cast_cols, cast_cols, cast_rows],
        out_specs=[pl.BlockSpec((tm, ATTN_DIM), row), pl.BlockSpec((tm, KV_DIM), row),
                   pl.BlockSpec((tm, KV_DIM), row), pl.BlockSpec((tm, CONV_DIM), row),
                   pl.BlockSpec((1, 2, CONV_DIM), lambda i: (i // tiles_per_seq, 0, 0)),
                   cast_cols_out, cast_cols_out, cast_rows],
        out_shape=[jax.ShapeDtypeStruct((m, ATTN_DIM), BF16),
                   jax.ShapeDtypeStruct((m, KV_DIM), F32),
                   jax.ShapeDtypeStruct((m, KV_DIM), F32),
                   jax.ShapeDtypeStruct((m, CONV_DIM), BF16),
                   jax.ShapeDtypeStruct((m // seq, 2, CONV_DIM), F32),
                   chunked, chunked, jax.ShapeDtypeStruct(ffn_w[2].shape, BF16)],
        scratch_shapes=[pltpu.VMEM((tm + 8, CONV_DIM), F32)],
        compiler_params=pltpu.CompilerParams(
            dimension_semantics=("arbitrary",), vmem_limit_bytes=VMEM_LIMIT),
        name="proj_prompt",
    )(x, g, wq, wr, wc, gc, umeta, *ffn_w)


def _proj_tail_kernel(x_ref, g_ref, wq_ref, wr_ref, wc_ref, gc_ref, h0_ref, h1_ref,
                      q_ref, k_ref, v_ref, u_ref, cn_ref):
    xn, gate_b, u = _project_conv(x_ref, g_ref, wr_ref)
    q, k, v = _project_qkv(xn, wq_ref, wr_ref)
    q_ref[...] = q
    k_ref[...] = k
    v_ref[...] = v
    u_ref[...] = u
    y = wc_ref[0:1, :] * h0_ref[...] + wc_ref[1:2, :] * h1_ref[...] + wc_ref[2:3, :] * u
    cn_ref[...] = _rms(gate_b * y, gc_ref[...]).astype(BF16)


def _proj_tail(x, g, wq, wr, wc, gc, h0, h1):
    m = x.shape[0]
    full = lambda shape: pl.BlockSpec(shape, lambda i: (0,) * len(shape))
    return pl.pallas_call(
        _proj_tail_kernel,
        grid=(1,),
        in_specs=[full((m, D_MODEL)), full((1, D_MODEL)), _resident((D_MODEL, ATTN_DIM)),
                  _resident((D_MODEL, R_END)),
                  full((3, CONV_DIM)), full((1, CONV_DIM)), full((m, CONV_DIM)),
                  full((m, CONV_DIM))],
        out_specs=[full((m, ATTN_DIM)), full((m, KV_DIM)), full((m, KV_DIM)),
                   full((m, CONV_DIM)), full((m, CONV_DIM))],
        out_shape=[jax.ShapeDtypeStruct((m, ATTN_DIM), F32),
                   jax.ShapeDtypeStruct((m, KV_DIM), F32),
                   jax.ShapeDtypeStruct((m, KV_DIM), F32),
                   jax.ShapeDtypeStruct((m, CONV_DIM), F32),
                   jax.ShapeDtypeStruct((m, CONV_DIM), BF16)],
        compiler_params=pltpu.CompilerParams(
            dimension_semantics=("arbitrary",), vmem_limit_bytes=VMEM_LIMIT),
        name="proj_tail",
    )(x, g, wq, wr, wc, gc, h0, h1)


class _BandedAttention:
    def __init__(self, first, sink_ref, q_ref, kprev, kcur, vprev, vcur, ga_ref, o_ref, nsub):
        self.sink_ref, self.q_ref, self.ga_ref, self.o_ref = sink_ref, q_ref, ga_ref, o_ref
        self.n_pairs = N_KV_HEADS // 2
        self.units = [(sb, pair) for sb in range(nsub) for pair in range(self.n_pairs)]
        kall = jnp.concatenate([kprev, kcur], axis=0)
        vall = jnp.concatenate([vprev, vcur], axis=0)
        c = lax.broadcasted_iota(jnp.int32, (2 * WINDOW, WINDOW), 0)
        r = lax.broadcasted_iota(jnp.int32, (2 * WINDOW, WINDOW), 1)
        self.band = (c >= r) & (c <= r + WINDOW)
        self.band_first = self.band & ((c >= WINDOW - N_META) | jnp.logical_not(first))
        low = lax.broadcasted_iota(jnp.int32, kall.shape[:1] + (LANES,), 1) < HALF
        self.k_half, self.vt_half = [], []
        for pair in range(self.n_pairs):
            kcol = kall[:, pair * LANES:(pair + 1) * LANES]
            vcol = vall[:, pair * LANES:(pair + 1) * LANES]
            self.k_half.append((jnp.where(low, kcol, 0.0).astype(BF16),
                                jnp.where(low, 0.0, kcol).astype(BF16)))
            self.vt_half.append((jnp.where(low, vcol, 0.0).T.astype(BF16),
                                 jnp.where(low, 0.0, vcol).T.astype(BF16)))
        self.outs = {}

    def scores(self, unit):
        sb, pair = unit
        rows = slice(sb * WINDOW, (sb + 1) * WINDOW)
        keys = slice(sb * WINDOW, (sb + 2) * WINDOW)
        kst = jnp.concatenate([self.k_half[pair][0][keys], self.k_half[pair][1][keys]], axis=0)
        qst = jnp.concatenate(
            [self.q_ref[rows, (grp * self.n_pairs + pair) * LANES:
                        (grp * self.n_pairs + pair + 1) * LANES]
             for grp in range(GQA_GROUP)], axis=0)
        return _dot_t(kst, qst)

    def softmax(self, unit, st_all):
        sb, pair = unit
        valid = self.band_first if sb == 0 else self.band
        pt_rows = []
        for half in range(2):
            pt_cols = []
            for grp in range(GQA_GROUP):
                sink = self.sink_ref[(2 * pair + half) * GQA_GROUP + grp]
                st = st_all[half * 2 * WINDOW:(half + 1) * 2 * WINDOW,
                            grp * WINDOW:(grp + 1) * WINDOW]
                st = jnp.where(valid, st, NEG)
                mx = jnp.maximum(jnp.max(st, axis=0, keepdims=True), sink)
                e = jnp.exp(st - mx)
                den = jnp.sum(e, axis=0, keepdims=True) + jnp.exp(sink - mx)
                pt_cols.append((e * (1.0 / den)).astype(BF16))
            pt_rows.append(jnp.concatenate(pt_cols, axis=1))
        return jnp.concatenate(pt_rows, axis=0)

    def values(self, unit, pt):
        sb, pair = unit
        keys = slice(sb * WINDOW, (sb + 2) * WINDOW)
        vst = jnp.concatenate([self.vt_half[pair][0][:, keys], self.vt_half[pair][1][:, keys]],
                              axis=1)
        ot_all = _dot(vst, pt)
        for grp in range(GQA_GROUP):
            self.outs[(sb, grp * self.n_pairs + pair)] = ot_all[:, grp * WINDOW:(grp + 1) * WINDOW]
        if pair == self.n_pairs - 1:
            self._normalise(sb)

    def _normalise(self, sb):
        rows = slice(sb * WINDOW, (sb + 1) * WINDOW)
        cols = range(ATTN_DIM // LANES)
        ssq = sum(jnp.sum(self.outs[(sb, col)] ** 2, axis=0, keepdims=True) for col in cols)
        inv = lax.rsqrt(ssq * (1.0 / ATTN_DIM) + EPS)
        for col in cols:
            sl = slice(col * LANES, (col + 1) * LANES)
            self.o_ref[rows, sl] = ((self.outs.pop((sb, col)) * inv).T
                                    * self.ga_ref[:, sl]).astype(BF16)


def _attn_merge_kernel(sink_ref, q_ref, kp_ref, kc_ref, vp_ref, vc_ref, km_ref, vm_ref, ga_ref,
                       x_ref, c_ref, woa_ref, woc_ref, g_ref, x2_ref, xn_ref,
                       an_new_ref, an_old_ref, *, nsub, tiles_per_seq, n_tiles):
    t = pl.program_id(0)

    @pl.when(t == 0)
    def _():
        an_new_ref[...] = jnp.zeros(an_new_ref.shape, an_new_ref.dtype)

    an_old_ref[...] = an_new_ref[...]

    first = (jnp.minimum(t, n_tiles - 1) % tiles_per_seq) == 0
    attn = _BandedAttention(first, sink_ref, q_ref,
                            jnp.where(first, km_ref[...], kp_ref[...]), kc_ref[...],
                            jnp.where(first, vm_ref[...], vp_ref[...]), vc_ref[...],
                            ga_ref, an_new_ref, nsub)
    units = attn.units
    n_chunks = len(units)
    cw = D_MODEL // n_chunks
    ssq = jnp.zeros((x_ref.shape[0], 1), F32)
    st_next = attn.scores(units[0])
    for u, unit in enumerate(units):
        cols = slice(u * cw, (u + 1) * cw)
        xc = x_ref[:, cols] + (_dot(an_old_ref[...], woa_ref[:, cols]) +
                               _dot(c_ref[...], woc_ref[:, cols]))
        x2_ref[:, cols] = xc
        ssq = ssq + jnp.sum(xc * xc, axis=-1, keepdims=True)
        st_all, st_next = st_next, (attn.scores(units[u + 1]) if u + 1 < n_chunks else None)
        attn.values(unit, attn.softmax(unit, st_all))
    inv = lax.rsqrt(ssq * (1.0 / D_MODEL) + EPS)
    for u in range(n_chunks):
        cols = slice(u * cw, (u + 1) * cw)
        xn_ref[:, cols] = (x2_ref[:, cols] * inv * g_ref[:, cols]).astype(BF16)


def _attn_merge(sinks, q, k, v, kmeta, vmeta, ga, x, c, woa, woc, g, *, seq, nsub):
    m = x.shape[0]
    tm = nsub * WINDOW
    n_tiles = m // tm
    tiles_per_seq = seq // tm
    cur = lambda t: (jnp.minimum(t, n_tiles - 1), 0)
    prev = lambda t: (jnp.maximum(jnp.minimum(t, n_tiles - 1) * nsub - 1, 0), 0)
    lag = lambda t: (jnp.maximum(t - 1, 0), 0)
    return pl.pallas_call(
        functools.partial(_attn_merge_kernel, nsub=nsub, tiles_per_seq=tiles_per_seq,
                          n_tiles=n_tiles),
        grid=(n_tiles + 1,),
        in_specs=[pl.BlockSpec(memory_space=pltpu.SMEM),
                  pl.BlockSpec((tm, ATTN_DIM), cur),
                  pl.BlockSpec((WINDOW, KV_DIM), prev), pl.BlockSpec((tm, KV_DIM), cur),
                  pl.BlockSpec((WINDOW, KV_DIM), prev), pl.BlockSpec((tm, KV_DIM), cur),
                  _resident((WINDOW, KV_DIM)), _resident((WINDOW, KV_DIM)),
                  _resident((1, ATTN_DIM)),
                  pl.BlockSpec((tm, D_MODEL), lag), pl.BlockSpec((tm, CONV_DIM), lag),
                  _resident((ATTN_DIM, D_MODEL)), _resident((CONV_DIM, D_MODEL)),
                  _resident((1, D_MODEL))],
        out_specs=[pl.BlockSpec((tm, D_MODEL), lag), pl.BlockSpec((tm, D_MODEL), lag)],
        out_shape=[jax.ShapeDtypeStruct((m, D_MODEL), F32),
                   jax.ShapeDtypeStruct((m, D_MODEL), BF16)],
        scratch_shapes=[pltpu.VMEM((tm, ATTN_DIM), BF16), pltpu.VMEM((tm, ATTN_DIM), BF16)],
        compiler_params=pltpu.CompilerParams(
            dimension_semantics=("arbitrary",), vmem_limit_bytes=VMEM_LIMIT),
        name="attn_merge_prompt",
    )(sinks, q, k, k, v, v, kmeta, vmeta, ga, x, c, woa, woc, g)


def _attn_sample_kernel(sink_ref, q_ref, kn_ref, vn_ref, ck_ref, cv_ref, ga_ref,
                        ok_ref, ov_ref, an_ref, *, bs):
    n_pairs = N_KV_HEADS // 2
    heads_per_col = 2 * GQA_GROUP
    last_key = lax.broadcasted_iota(jnp.int32, (KV_DIM, WINDOW), 1) == WINDOW - 1
    pad = jnp.zeros((WINDOW - bs, KV_DIM), F32)
    knt = jnp.concatenate([kn_ref[...], pad], axis=0).T
    vnt = jnp.concatenate([vn_ref[...], pad], axis=0).T
    for s in range(bs):
        ok_ref[s] = jnp.where(last_key, knt[:, s:s + 1], pltpu.roll(ck_ref[s], WINDOW - 1, 1))
        ov_ref[s] = jnp.where(last_key, vnt[:, s:s + 1], pltpu.roll(cv_ref[s], WINDOW - 1, 1))

    n_rows = heads_per_col * bs
    row = lax.broadcasted_iota(jnp.int32, (n_rows, bs * WINDOW), 0)
    lane = lax.broadcasted_iota(jnp.int32, (n_rows, bs * WINDOW), 1)
    log2 = lambda n: n.bit_length() - 1
    assert bs == 1 << log2(bs) and WINDOW == 1 << log2(WINDOW)
    same_seq = (row & (bs - 1)) == (lane >> log2(WINDOW))
    head_row = lax.broadcasted_iota(jnp.int32, (n_rows, 1), 0) >> log2(bs)
    low = lax.broadcasted_iota(jnp.int32, (bs, LANES), 1) < HALF

    cols = [None] * (ATTN_DIM // LANES)
    for pair in range(n_pairs):
        sl = slice(pair * LANES, (pair + 1) * LANES)
        pieces = []
        sk = jnp.zeros((n_rows, 1), F32)
        for grp in range(GQA_GROUP):
            col = grp * n_pairs + pair
            qcol = q_ref[:, col * LANES:(col + 1) * LANES]
            pieces += [jnp.where(low, qcol, 0.0), jnp.where(low, 0.0, qcol)]
            for half in range(2):
                sink = sink_ref[(2 * pair + half) * GQA_GROUP + grp]
                sk = jnp.where(head_row == 2 * grp + half, sink, sk)
        lq = jnp.concatenate(pieces, axis=0).astype(BF16)
        kstack = jnp.concatenate([ck_ref[s, sl, :] for s in range(bs)], axis=1).astype(BF16)
        vstack = jnp.concatenate([cv_ref[s, sl, :] for s in range(bs)], axis=1).astype(BF16)
        s = jnp.where(same_seq, _dot(lq, kstack), NEG)
        k_new = jnp.concatenate([kn_ref[:, sl]] * heads_per_col, axis=0).astype(BF16)
        v_new = jnp.concatenate([vn_ref[:, sl]] * heads_per_col, axis=0).astype(BF16)
        s_new = jnp.sum(lq.astype(F32) * k_new.astype(F32), axis=-1, keepdims=True)
        mx = jnp.maximum(jnp.maximum(jnp.max(s, axis=-1, keepdims=True), s_new), sk)
        e = jnp.exp(s - mx)
        e_new = jnp.exp(s_new - mx)
        rden = 1.0 / (jnp.sum(e, axis=-1, keepdims=True) + e_new + jnp.exp(sk - mx))
        p = (e * rden).astype(BF16)
        p_new = (e_new * rden).astype(BF16).astype(F32)
        o = _dot_t(p, vstack) + p_new * v_new.astype(F32)
        for grp in range(GQA_GROUP):
            lo = o[(2 * grp) * bs:(2 * grp + 1) * bs]
            hi = o[(2 * grp + 1) * bs:(2 * grp + 2) * bs]
            cols[grp * n_pairs + pair] = jnp.where(low, lo, hi)
    an_ref[...] = _rms(jnp.concatenate(cols, axis=1), ga_ref[...])


def _attn_sample(sinks, q, kn, vn, ck, cv, ga, *, bs):
    nseq = q.shape[0]
    row = lambda i: (i, 0)
    blk3 = pl.BlockSpec((bs, KV_DIM, WINDOW), lambda i: (i, 0, 0))
    return pl.pallas_call(
        functools.partial(_attn_sample_kernel, bs=bs),
        grid=(nseq // bs,),
        in_specs=[pl.BlockSpec(memory_space=pltpu.SMEM),
                  pl.BlockSpec((bs, ATTN_DIM), row), pl.BlockSpec((bs, KV_DIM), row),
                  pl.BlockSpec((bs, KV_DIM), row), blk3, blk3,
                  pl.BlockSpec((1, ATTN_DIM), lambda i: (0, 0))],
        out_specs=[blk3, blk3, pl.BlockSpec((bs, ATTN_DIM), row)],
        out_shape=[jax.ShapeDtypeStruct(ck.shape, F32), jax.ShapeDtypeStruct(cv.shape, F32),
                   jax.ShapeDtypeStruct((nseq, ATTN_DIM), F32)],
        compiler_params=pltpu.CompilerParams(
            dimension_semantics=("parallel",), vmem_limit_bytes=VMEM_LIMIT),
        name="attn_sample",
    )(sinks, q, kn, vn, ck, cv, ga)


def _group_major(w, axis):
    shape = w.shape
    w = w.reshape(shape[:axis] + (N_KV_HEADS, GQA_GROUP, HEAD_DIM) + shape[axis + 1:])
    return jnp.swapaxes(w, axis, axis + 1).reshape(shape)


def kernel(x_prompt, x_sample, cache_swa_k, cache_swa_v, state_conv, meta_tokens, g_ffn1, w1_gate, w1_up, w1_down, g_mix, w_in, attn_sinks, w_conv, g_attn_out, g_conv_out, w_out, g_ffn2, w2_gate, w2_up, w2_down, g_final):
    assert g_ffn1.shape[0] == 1, "one layer"
    batch, seq, _ = x_prompt.shape
    n_dec = x_sample.shape[0]
    assert x_sample.shape[1] == 1 and cache_swa_k.shape[2] == WINDOW

    row2d = lambda a: a.reshape(1, -1).astype(F32)
    wq = _group_major(w_in[0, :, :ATTN_DIM], 1).astype(BF16)
    wo_a = _group_major(w_out[0, :ATTN_DIM], 0).astype(BF16)
    wo_c = w_out[0, ATTN_DIM:].astype(BF16)
    ga = row2d(_group_major(g_attn_out[0], 0))
    gc = row2d(g_conv_out[0])
    sinks = attn_sinks[0].astype(F32)
    wc = w_conv[0].astype(F32)

    xp = x_prompt.reshape(batch * seq, D_MODEL)
    xt = jnp.concatenate([x_sample[:, 0, :], meta_tokens.astype(x_prompt.dtype)], axis=0)
    n_tail = xt.shape[0]

    xt1, w1g, w1u, w1d = _ffn(xt, row2d(g_ffn1[0]), w1_gate[0], w1_up[0], w1_down[0],
                              tm=n_tail, tf=FFN_TF, name="ffn1_tail")
    xp1, wr = _ffn(xp, row2d(g_ffn1[0]), w1g, w1u, w1d, tm=FFN_TM, tf=FFN_TF,
                   side_cast=(w_in[0], ATTN_DIM), name="ffn1_prompt")

    st = state_conv[0]
    zpad = jnp.zeros((N_META, CONV_DIM), F32)
    h0 = jnp.concatenate([st[:, 0, :], zpad], axis=0)
    h1 = jnp.concatenate([st[:, 1, :], zpad], axis=0)
    qt, kt, vt, ut, cnt = _proj_tail(xt1, row2d(g_mix[0]), wq, wr, wc, gc, h0, h1)
    kmeta = jnp.pad(kt[n_dec:], ((WINDOW - N_META, 0), (0, 0)))
    vmeta = jnp.pad(vt[n_dec:], ((WINDOW - N_META, 0), (0, 0)))
    umeta = ut[n_tail - 8:]

    qp, kp, vp, cnp, cstate, w2g, w2u, w2d = _proj_prompt(
        xp1, row2d(g_mix[0]), wq, wr, wc, gc, umeta, (w2_gate[0], w2_up[0], w2_down[0]),
        tm=PROJ_TM, seq=seq)
    xp2, xpn = _attn_merge(sinks, qp, kp, vp, kmeta, vmeta, ga, xp1, cnp, wo_a, wo_c,
                           row2d(g_ffn2[0]), seq=seq, nsub=ATTN_NSUB)

    to_channel_major = lambda c: jnp.transpose(c[0], (0, 2, 3, 1)).reshape(n_dec, KV_DIM, WINDOW)
    from_channel_major = lambda c: jnp.transpose(
        c.reshape(n_dec, N_KV_HEADS, HEAD_DIM, WINDOW), (0, 3, 1, 2))[None]
    nk, nv, ans = _attn_sample(sinks, qt[:n_dec], kt[:n_dec], vt[:n_dec],
                               to_channel_major(cache_swa_k), to_channel_major(cache_swa_v), ga,
                               bs=SAMPLE_BS)

    xs2, xsn = _merge(xt1[:n_dec], ans, cnt[:n_dec], wo_a, wo_c, row2d(g_ffn2[0]), tm=n_dec,
                      name="merge_tail")
    ys = _ffn(xs2, xsn, w2g, w2u, w2d, tm=n_dec, tf=FFN_TF, g_final=row2d(g_final),
              name="ffn2_tail")
    yp = _ffn(xp2, xpn, w2g, w2u, w2d, tm=FFN_TM, tf=FFN_TF, g_final=row2d(g_final),
              name="ffn2_prompt")

    kv_shape = (1, batch, WINDOW, N_KV_HEADS, HEAD_DIM)
    new_k_prompt = kp.reshape(batch, seq, KV_DIM)[:, seq - WINDOW:].reshape(kv_shape)
    new_v_prompt = vp.reshape(batch, seq, KV_DIM)[:, seq - WINDOW:].reshape(kv_shape)
    new_conv_sample = jnp.stack([st[:, 1, :], ut[:n_dec]], axis=1)[None]
    return (yp.reshape(batch, seq, D_MODEL), ys.reshape(n_dec, 1, D_MODEL),
            new_k_prompt, new_v_prompt, cstate[None],
            from_channel_major(nk), from_channel_major(nv), new_conv_sample)
```

```python
import functools

import jax
import jax.numpy as jnp
from jax import lax
from jax.experimental import pallas as pl
from jax.experimental.pallas import tpu as pltpu

D_MODEL = 2048
D_FF = 5632
N_META = 16
ATTN_DIM = 1024
CONV_DIM = 1024
HEAD_DIM = 64
N_HEADS = 16
N_KV_HEADS = 4
GQA_GROUP = 4
KV_DIM = 256
WINDOW = 128
IN_DIM = ATTN_DIM + 2 * KV_DIM + 3 * CONV_DIM
EPS = 1e-5
NEG = -1e30

LANES = 128
HALF = LANES // 2
VMEM_LIMIT = 60 * 1024 * 1024
FFN_TM, FFN_TF = 1024, 512
HEAD_TF = 256
FFN_EDGE_ROWS = 256
PROJ_TM = 512
PROJ_ROW_CHUNKS = 2
CAST_TF = 256
ATTN_NSUB = 4
PREP_ROWS = 512
SAMPLE_BS = 16

F32 = jnp.float32
BF16 = jnp.bfloat16


def _rms(x, g):
    return x * lax.rsqrt(jnp.mean(x * x, axis=-1, keepdims=True) + EPS) * g


def _dot(a, b):
    return jnp.dot(a, b, preferred_element_type=F32)


def _dot_t(a, b):
    return lax.dot_general(a, b, (((1,), (1,)), ((), ())), preferred_element_type=F32)


def _resident(shape):
    return pl.BlockSpec(shape, lambda *_: (0,) * len(shape), pipeline_mode=pl.Buffered(1))


def _ffn_kernel(*refs, n_ff, normed_input, has_final, cast_weights, side_chunks, prefetch_x,
                head_given):
    refs = list(refs)
    x_ref, norm_ref, wg_ref, wu_ref, wd_ref = refs[:5]
    refs = refs[5:]
    if has_final:
        gf_ref = refs.pop(0)
    if side_chunks:
        side_src_ref = refs.pop(0)
    if head_given:
        head_hbm = refs.pop(0)
    out_ref = refs.pop(0)
    if cast_weights:
        wgb_ref, wub_ref, wdb_ref = refs[:3]
        refs = refs[3:]
    if side_chunks:
        side_dst_ref = refs.pop(0)
    xn_ref = norm_ref if normed_input else refs.pop(0)
    i = pl.program_id(0)
    j = pl.program_id(1)

    if prefetch_x:
        x_hbm, x_ref, x_sem = x_ref, refs.pop(0), refs.pop(0)
        tm = x_ref.shape[0]

        def x_copy(tile):
            return pltpu.make_async_copy(x_hbm.at[pl.ds(tile * tm, tm), :], x_ref, x_sem)

        @pl.when(j == 0)
        def _():
            pl.when(i == 0)(lambda: x_copy(0).start())
            x_copy(i).wait()

        pl.when((j == 1) & (i + 1 < pl.num_programs(0)))(lambda: x_copy(i + 1).start())

    if head_given:
        head_sem = refs.pop(0)

        @pl.when((i == 0) & (j == 0))
        def _():
            head_copy = pltpu.make_async_copy(head_hbm, out_ref, head_sem)
            head_copy.start()
            head_copy.wait()

    if side_chunks:
        @pl.when(pl.program_id(0) * n_ff + j < side_chunks)
        def _():
            side_dst_ref[...] = side_src_ref[...].astype(BF16)

    if cast_weights:
        wgb_ref[...] = wg_ref[...].astype(BF16)
        wub_ref[...] = wu_ref[...].astype(BF16)
        wdb_ref[...] = wd_ref[...].astype(BF16)
        wg_ref, wu_ref, wd_ref = wgb_ref, wub_ref, wdb_ref

    def chunk_step(first, last, chunks):
        cm = out_ref.shape[0] // chunks
        for c in range(chunks):
            rows = slice(c * cm, (c + 1) * cm)
            if first and not normed_input:
                xn = _rms(x_ref[rows, :], norm_ref[...]).astype(BF16)
                xn_ref[rows, :] = xn
            else:
                xn = xn_ref[rows, :]
            gate = _dot(xn, wg_ref[...])
            up = _dot(xn, wu_ref[...])
            h = (gate * jax.nn.sigmoid(gate) * up * 0.5).astype(BF16)
            acc = (x_ref[rows, :] if first else out_ref[rows, :]) + _dot(h, wd_ref[...])
            if last and has_final:
                acc = _rms(acc, gf_ref[...])
            out_ref[rows, :] = acc

    row_chunks = max(1, out_ref.shape[0] // FFN_EDGE_ROWS)
    first_chunks = 1 if normed_input else row_chunks
    last_chunks = row_chunks if has_final else 1
    if n_ff == 1:
        chunk_step(True, True, row_chunks)
        return
    active = (i > 0) if head_given else True
    pl.when(active & (j == 0))(lambda: chunk_step(True, False, first_chunks))
    if has_final:
        pl.when(active & (j > 0) & (j < n_ff - 1))(lambda: chunk_step(False, False, 1))
        pl.when(active & (j == n_ff - 1))(lambda: chunk_step(False, True, last_chunks))
    else:
        pl.when(active & (j > 0))(lambda: chunk_step(False, False, 1))


def _ffn(x, norm, wg, wu, wd, *, tm, tf, g_final=None, side_cast=None, head=None, name):
    m = x.shape[0]
    assert m % tm == 0 and D_FF % tf == 0 and FFN_TF % tf == 0
    n_ff = D_FF // tf
    normed_input = norm.shape[0] == m
    cast_weights = wg.dtype == F32
    assert not cast_weights or m == tm
    assert head is None or (head.shape == (tm, D_MODEL) and m > tm)
    side_src, side_col0 = side_cast if side_cast is not None else (None, 0)
    side_chunks = 0 if side_src is None else (side_src.shape[1] - side_col0) // CAST_TF
    assert side_chunks <= (m // tm) * n_ff and side_col0 % CAST_TF == 0
    row = lambda i, j: (i, 0)
    side_chunk = lambda i, j: jnp.minimum(i * n_ff + j, side_chunks - 1)
    side_in = pl.BlockSpec((D_MODEL, CAST_TF),
                           lambda i, j: (0, side_col0 // CAST_TF + side_chunk(i, j)))
    side_out = pl.BlockSpec((D_MODEL, CAST_TF), lambda i, j: (0, side_chunk(i, j)))
    wj = (lambda i, j: jnp.where(i == 0, 0, j)) if head is not None else (lambda i, j: j)
    per_ff = FFN_TF // tf
    wd_spec = pl.BlockSpec((tf, D_MODEL), lambda i, j: (wj(i, j), 0))
    w_f32_specs = [pl.BlockSpec((D_MODEL, tf), lambda i, j: (0, j))] * 2 + [wd_spec]
    w_bf16_specs = [pl.BlockSpec((None, D_MODEL, tf),
                                 lambda i, j: (wj(i, j) // per_ff, 0, wj(i, j) % per_ff))] * 2
    w_bf16_specs += [wd_spec]
    prefetch_x = m // tm > 1 and n_ff > 1
    in_specs = [pl.BlockSpec(memory_space=pl.ANY) if prefetch_x
                else pl.BlockSpec((tm, D_MODEL), row, pipeline_mode=pl.Buffered(1)),
                pl.BlockSpec((tm, D_MODEL), row) if normed_input else _resident((1, D_MODEL)),
                *(w_f32_specs if cast_weights else w_bf16_specs)]
    args = [x, norm, wg, wu, wd]
    if g_final is not None:
        in_specs.append(_resident((1, D_MODEL)))
        args.append(g_final)
    if side_chunks:
        in_specs.append(side_in)
        args.append(side_src)
    if head is not None:
        in_specs.append(pl.BlockSpec(memory_space=pl.ANY))
        args.append(head)
    out_specs = [pl.BlockSpec((tm, D_MODEL), row)]
    out_shape = [jax.ShapeDtypeStruct((m, D_MODEL), F32)]
    if cast_weights:
        out_specs += w_bf16_specs
        out_shape += [jax.ShapeDtypeStruct((D_FF // FFN_TF, D_MODEL, FFN_TF), BF16)] * 2
        out_shape += [jax.ShapeDtypeStruct(wd.shape, BF16)]
    if side_chunks:
        out_specs.append(side_out)
        out_shape.append(jax.ShapeDtypeStruct((D_MODEL, side_chunks * CAST_TF), BF16))
    scratch_shapes = [] if normed_input else [pltpu.VMEM((tm, D_MODEL), BF16)]
    if prefetch_x:
        scratch_shapes += [pltpu.VMEM((tm, D_MODEL), F32), pltpu.SemaphoreType.DMA(())]
    if head is not None:
        scratch_shapes += [pltpu.SemaphoreType.DMA(())]
    ordered_rows = bool(side_chunks) or prefetch_x
    outs = pl.pallas_call(
        functools.partial(_ffn_kernel, n_ff=n_ff, normed_input=normed_input,
                          has_final=g_final is not None, cast_weights=cast_weights,
                          side_chunks=side_chunks, prefetch_x=prefetch_x,
                          head_given=head is not None),
        grid=(m // tm, n_ff),
        in_specs=in_specs,
        out_specs=out_specs,
        out_shape=out_shape,
        scratch_shapes=scratch_shapes,
        compiler_params=pltpu.CompilerParams(
            dimension_semantics=("arbitrary" if ordered_rows else "parallel", "arbitrary"),
            vmem_limit_bytes=VMEM_LIMIT),
        name=name,
    )(*args)
    return outs if len(outs) > 1 else outs[0]


def _merge_kernel(x_ref, a_ref, c_ref, woa_ref, woc_ref, g_ref, x2_ref, xn_ref):
    x = x_ref[...] + (_dot(a_ref[...].astype(BF16), woa_ref[...]) +
                      _dot(c_ref[...].astype(BF16), woc_ref[...]))
    x2_ref[...] = x
    xn_ref[...] = _rms(x, g_ref[...]).astype(BF16)


def _merge(x, a, c, woa, woc, g, *, tm, name):
    m = x.shape[0]
    assert m % tm == 0
    row = lambda i: (i, 0)
    return pl.pallas_call(
        _merge_kernel,
        grid=(m // tm,),
        in_specs=[pl.BlockSpec((tm, D_MODEL), row), pl.BlockSpec((tm, ATTN_DIM), row),
                  pl.BlockSpec((tm, CONV_DIM), row), _resident((ATTN_DIM, D_MODEL)),
                  _resident((CONV_DIM, D_MODEL)), _resident((1, D_MODEL))],
        out_specs=[pl.BlockSpec((tm, D_MODEL), row), pl.BlockSpec((tm, D_MODEL), row)],
        out_shape=[jax.ShapeDtypeStruct((m, D_MODEL), F32),
                   jax.ShapeDtypeStruct((m, D_MODEL), BF16)],
        compiler_params=pltpu.CompilerParams(
            dimension_semantics=("parallel",), vmem_limit_bytes=VMEM_LIMIT),
        name=name,
    )(x, a, c, woa, woc, g)


K0, V0, B0, C0, H0, R_END = 0, 256, 512, 1536, 2560, 3584


def _project_conv(x_ref, g_ref, wr_ref):
    xn = _rms(x_ref[...], g_ref[...]).astype(BF16)
    gate_b = _dot(xn, wr_ref[:, B0:C0])
    u = _dot(xn, wr_ref[:, C0:H0]) * _dot(xn, wr_ref[:, H0:R_END])
    return xn, gate_b, u


def _project_qkv(xn, wq_ref, wr_ref):
    q = _dot(xn, wq_ref[...]) * (HEAD_DIM ** -0.5)
    return q, _dot(xn, wr_ref[:, K0:V0]), _dot(xn, wr_ref[:, V0:B0])


def _proj_prompt_kernel(x_ref, g_ref, wq_ref, wr_ref, wc_ref, gc_ref, umeta_ref,
                        fg_ref, fu_ref, fd_ref,
                        q_ref, k_ref, v_ref, cn_ref, cst_ref, bg_ref, bu_ref, bd_ref,
                        us_ref, *, tm, tiles_per_seq):
    t = pl.program_id(0) % tiles_per_seq

    @pl.when(t == 0)
    def _():
        us_ref[0:8, :] = umeta_ref[...]

    @pl.when(t != 0)
    def _():
        us_ref[0:8, :] = us_ref[tm:tm + 8, :]

    hm = tm // PROJ_ROW_CHUNKS
    casts = ((fg_ref, bg_ref), (fu_ref, bu_ref), (fd_ref, bd_ref))
    for h in range(PROJ_ROW_CHUNKS):
        rows = slice(h * hm, (h + 1) * hm)
        for src, dst in casts[h::PROJ_ROW_CHUNKS]:
            dst[...] = src[...].astype(BF16)
        xn, gate_b, u = _project_conv(x_ref.at[rows], g_ref, wr_ref)
        us_ref[8 + h * hm:8 + (h + 1) * hm, :] = u
        y = (wc_ref[0:1, :] * us_ref[6 + h * hm:6 + (h + 1) * hm, :]
             + wc_ref[1:2, :] * us_ref[7 + h * hm:7 + (h + 1) * hm, :] + wc_ref[2:3, :] * u)
        cn_ref[rows, :] = _rms(gate_b * y, gc_ref[...]).astype(BF16)
        q, k, v = _project_qkv(xn, wq_ref, wr_ref)
        q_ref[rows, :] = q.astype(BF16)
        k_ref[rows, :] = k
        v_ref[rows, :] = v
    cst_ref[0] = us_ref[tm + 6:tm + 8, :]


def _proj_prompt(x, g, wq, wr, wc, gc, umeta, ffn_w, *, tm, seq):
    m = x.shape[0]
    tiles_per_seq = seq // tm
    n_steps = m // tm
    n_cast = D_FF // CAST_TF
    assert D_FF % CAST_TF == 0 and n_cast <= n_steps
    row = lambda i: (i, 0)
    chunk = lambda i: jnp.minimum(i, n_cast - 1)
    cast_cols = pl.BlockSpec((D_MODEL, CAST_TF), lambda i: (0, chunk(i)))
    cast_rows = pl.BlockSpec((CAST_TF, D_MODEL), lambda i: (chunk(i), 0))
    per_ff = FFN_TF // CAST_TF
    assert FFN_TF % CAST_TF == 0
    cast_cols_out = pl.BlockSpec((None, D_MODEL, CAST_TF),
                                 lambda i: (chunk(i) // per_ff, 0, chunk(i) % per_ff))
    chunked = jax.ShapeDtypeStruct((D_FF // FFN_TF, D_MODEL, FFN_TF), BF16)
    return pl.pallas_call(
        functools.partial(_proj_prompt_kernel, tm=tm, tiles_per_seq=tiles_per_seq),
        grid=(n_steps,),
        in_specs=[pl.BlockSpec((tm, D_MODEL), row), _resident((1, D_MODEL)),
                  _resident((D_MODEL, ATTN_DIM)), _resident((D_MODEL, R_END)),
                  _resident((3, CONV_DIM)), _resident((1, CONV_DIM)), _resident((8, CONV_DIM)),
                  cast_cols, cast_cols, cast_rows],
        out_specs=[pl.BlockSpec((tm, ATTN_DIM), row), pl.BlockSpec((tm, KV_DIM), row),
                   pl.BlockSpec((tm, KV_DIM), row), pl.BlockSpec((tm, CONV_DIM), row),
                   pl.BlockSpec((1, 2, CONV_DIM), lambda i: (i // tiles_per_seq, 0, 0)),
                   cast_cols_out, cast_cols_out, cast_rows],
        out_shape=[jax.ShapeDtypeStruct((m, ATTN_DIM), BF16),
                   jax.ShapeDtypeStruct((m, KV_DIM), F32),
                   jax.ShapeDtypeStruct((m, KV_DIM), F32),
                   jax.ShapeDtypeStruct((m, CONV_DIM), BF16),
                   jax.ShapeDtypeStruct((m // seq, 2, CONV_DIM), F32),
                   chunked, chunked, jax.ShapeDtypeStruct(ffn_w[2].shape, BF16)],
        scratch_shapes=[pltpu.VMEM((tm + 8, CONV_DIM), F32)],
        compiler_params=pltpu.CompilerParams(
            dimension_semantics=("arbitrary",), vmem_limit_bytes=VMEM_LIMIT),
        name="proj_prompt",
    )(x, g, wq, wr, wc, gc, umeta, *ffn_w)


def _proj_tail_kernel(x_ref, g_ref, wq_ref, wr_ref, wc_ref, gc_ref, h0_ref, h1_ref,
                      q_ref, k_ref, v_ref, u_ref, cn_ref):
    xn, gate_b, u = _project_conv(x_ref, g_ref, wr_ref)
    q, k, v = _project_qkv(xn, wq_ref, wr_ref)
    q_ref[...] = q
    k_ref[...] = k
    v_ref[...] = v
    u_ref[...] = u
    y = wc_ref[0:1, :] * h0_ref[...] + wc_ref[1:2, :] * h1_ref[...] + wc_ref[2:3, :] * u
    cn_ref[...] = _rms(gate_b * y, gc_ref[...]).astype(BF16)


def _proj_tail(x, g, wq, wr, wc, gc, h0, h1):
    m = x.shape[0]
    full = lambda shape: pl.BlockSpec(shape, lambda i: (0,) * len(shape))
    return pl.pallas_call(
        _proj_tail_kernel,
        grid=(1,),
        in_specs=[full((m, D_MODEL)), full((1, D_MODEL)), _resident((D_MODEL, ATTN_DIM)),
                  _resident((D_MODEL, R_END)),
                  full((3, CONV_DIM)), full((1, CONV_DIM)), full((m, CONV_DIM)),
                  full((m, CONV_DIM))],
        out_specs=[full((m, ATTN_DIM)), full((m, KV_DIM)), full((m, KV_DIM)),
                   full((m, CONV_DIM)), full((m, CONV_DIM))],
        out_shape=[jax.ShapeDtypeStruct((m, ATTN_DIM), F32),
                   jax.ShapeDtypeStruct((m, KV_DIM), F32),
                   jax.ShapeDtypeStruct((m, KV_DIM), F32),
                   jax.ShapeDtypeStruct((m, CONV_DIM), F32),
                   jax.ShapeDtypeStruct((m, CONV_DIM), BF16)],
        compiler_params=pltpu.CompilerParams(
            dimension_semantics=("arbitrary",), vmem_limit_bytes=VMEM_LIMIT),
        name="proj_tail",
    )(x, g, wq, wr, wc, gc, h0, h1)


class _BandedAttention:
    def __init__(self, first, sink_ref, q_ref, kprev, kcur, vprev, vcur, ga_ref, o_ref, nsub):
        self.sink_ref, self.q_ref, self.ga_ref, self.o_ref = sink_ref, q_ref, ga_ref, o_ref
        self.n_pairs = N_KV_HEADS // 2
        self.units = [(sb, pair) for sb in range(nsub) for pair in range(self.n_pairs)]
        kall = jnp.concatenate([kprev, kcur], axis=0)
        vall = jnp.concatenate([vprev, vcur], axis=0)
        c = lax.broadcasted_iota(jnp.int32, (2 * WINDOW, WINDOW), 0)
        r = lax.broadcasted_iota(jnp.int32, (2 * WINDOW, WINDOW), 1)
        self.band = (c >= r) & (c <= r + WINDOW)
        self.band_first = self.band & ((c >= WINDOW - N_META) | jnp.logical_not(first))
        low = lax.broadcasted_iota(jnp.int32, kall.shape[:1] + (LANES,), 1) < HALF
        self.k_half, self.vt_half = [], []
        for pair in range(self.n_pairs):
            kcol = kall[:, pair * LANES:(pair + 1) * LANES]
            vcol = vall[:, pair * LANES:(pair + 1) * LANES]
            self.k_half.append((jnp.where(low, kcol, 0.0).astype(BF16),
                                jnp.where(low, 0.0, kcol).astype(BF16)))
            self.vt_half.append((jnp.where(low, vcol, 0.0).T.astype(BF16),
                                 jnp.where(low, 0.0, vcol).T.astype(BF16)))
        self.outs = {}

    def scores(self, unit):
        sb, pair = unit
        rows = slice(sb * WINDOW, (sb + 1) * WINDOW)
        keys = slice(sb * WINDOW, (sb + 2) * WINDOW)
        kst = jnp.concatenate([self.k_half[pair][0][keys], self.k_half[pair][1][keys]], axis=0)
        qst = jnp.concatenate(
            [self.q_ref[rows, (grp * self.n_pairs + pair) * LANES:
                        (grp * self.n_pairs + pair + 1) * LANES]
             for grp in range(GQA_GROUP)], axis=0)
        return _dot_t(kst, qst)

    def softmax(self, unit, st_all):
        sb, pair = unit
        valid = self.band_first if sb == 0 else self.band
        pt_rows = []
        for half in range(2):
            pt_cols = []
            for grp in range(GQA_GROUP):
                sink = self.sink_ref[(2 * pair + half) * GQA_GROUP + grp]
                st = st_all[half * 2 * WINDOW:(half + 1) * 2 * WINDOW,
                            grp * WINDOW:(grp + 1) * WINDOW]
                st = jnp.where(valid, st, NEG)
                mx = jnp.maximum(jnp.max(st, axis=0, keepdims=True), sink)
                e = jnp.exp(st - mx)
                den = jnp.sum(e, axis=0, keepdims=True) + jnp.exp(sink - mx)
                pt_cols.append((e * (1.0 / den)).astype(BF16))
            pt_rows.append(jnp.concatenate(pt_cols, axis=1))
        return jnp.concatenate(pt_rows, axis=0)

    def values(self, unit, pt):
        sb, pair = unit
        keys = slice(sb * WINDOW, (sb + 2) * WINDOW)
        vst = jnp.concatenate([self.vt_half[pair][0][:, keys], self.vt_half[pair][1][:, keys]],
                              axis=1)
        ot_all = _dot(vst, pt)
        for grp in range(GQA_GROUP):
            self.outs[(sb, grp * self.n_pairs + pair)] = ot_all[:, grp * WINDOW:(grp + 1) * WINDOW]
        if pair == self.n_pairs - 1:
            self._normalise(sb)

    def _normalise(self, sb):
        rows = slice(sb * WINDOW, (sb + 1) * WINDOW)
        cols = range(ATTN_DIM // LANES)
        ssq = sum(jnp.sum(self.outs[(sb, col)] ** 2, axis=0, keepdims=True) for col in cols)
        inv = lax.rsqrt(ssq * (1.0 / ATTN_DIM) + EPS)
        for col in cols:
            sl = slice(col * LANES, (col + 1) * LANES)
            self.o_ref[rows, sl] = ((self.outs.pop((sb, col)) * inv).T
                                    * self.ga_ref[:, sl]).astype(BF16)


def _attn_merge_kernel(sink_ref, q_ref, kp_ref, kc_ref, vp_ref, vc_ref, km_ref, vm_ref, ga_ref,
                       x_ref, c_ref, woa_ref, woc_ref, g_ref, x2_ref, xn_ref,
                       an_new_ref, an_old_ref, *, nsub, tiles_per_seq, n_tiles):
    t = pl.program_id(0)

    @pl.when(t == 0)
    def _():
        an_new_ref[...] = jnp.zeros(an_new_ref.shape, an_new_ref.dtype)

    an_old_ref[...] = an_new_ref[...]

    first = (jnp.minimum(t, n_tiles - 1) % tiles_per_seq) == 0
    attn = _BandedAttention(first, sink_ref, q_ref,
                            jnp.where(first, km_ref[...], kp_ref[...]), kc_ref[...],
                            jnp.where(first, vm_ref[...], vp_ref[...]), vc_ref[...],
                            ga_ref, an_new_ref, nsub)
    units = attn.units
    n_chunks = len(units)
    cw = D_MODEL // n_chunks
    ssq = jnp.zeros((x_ref.shape[0], 1), F32)
    st_next = attn.scores(units[0])
    for u, unit in enumerate(units):
        cols = slice(u * cw, (u + 1) * cw)
        xc = x_ref[:, cols] + (_dot(an_old_ref[...], woa_ref[:, cols]) +
                               _dot(c_ref[...], woc_ref[:, cols]))
        x2_ref[:, cols] = xc
        ssq = ssq + jnp.sum(xc * xc, axis=-1, keepdims=True)
        st_all, st_next = st_next, (attn.scores(units[u + 1]) if u + 1 < n_chunks else None)
        attn.values(unit, attn.softmax(unit, st_all))
    inv = lax.rsqrt(ssq * (1.0 / D_MODEL) + EPS)
    for u in range(n_chunks):
        cols = slice(u * cw, (u + 1) * cw)
        xn_ref[:, cols] = (x2_ref[:, cols] * inv * g_ref[:, cols]).astype(BF16)


def _attn_merge(sinks, q, k, v, kmeta, vmeta, ga, x, c, woa, woc, g, *, seq, nsub):
    m = x.shape[0]
    tm = nsub * WINDOW
    n_tiles = m // tm
    tiles_per_seq = seq // tm
    cur = lambda t: (jnp.minimum(t, n_tiles - 1), 0)
    prev = lambda t: (jnp.maximum(jnp.minimum(t, n_tiles - 1) * nsub - 1, 0), 0)
    lag = lambda t: (jnp.maximum(t - 1, 0), 0)
    return pl.pallas_call(
        functools.partial(_attn_merge_kernel, nsub=nsub, tiles_per_seq=tiles_per_seq,
                          n_tiles=n_tiles),
        grid=(n_tiles + 1,),
        in_specs=[pl.BlockSpec(memory_space=pltpu.SMEM),
                  pl.BlockSpec((tm, ATTN_DIM), cur),
                  pl.BlockSpec((WINDOW, KV_DIM), prev), pl.BlockSpec((tm, KV_DIM), cur),
                  pl.BlockSpec((WINDOW, KV_DIM), prev), pl.BlockSpec((tm, KV_DIM), cur),
                  _resident((WINDOW, KV_DIM)), _resident((WINDOW, KV_DIM)),
                  _resident((1, ATTN_DIM)),
                  pl.BlockSpec((tm, D_MODEL), lag), pl.BlockSpec((tm, CONV_DIM), lag),
                  _resident((ATTN_DIM, D_MODEL)), _resident((CONV_DIM, D_MODEL)),
                  _resident((1, D_MODEL))],
        out_specs=[pl.BlockSpec((tm, D_MODEL), lag), pl.BlockSpec((tm, D_MODEL), lag)],
        out_shape=[jax.ShapeDtypeStruct((m, D_MODEL), F32),
                   jax.ShapeDtypeStruct((m, D_MODEL), BF16)],
        scratch_shapes=[pltpu.VMEM((tm, ATTN_DIM), BF16), pltpu.VMEM((tm, ATTN_DIM), BF16)],
        compiler_params=pltpu.CompilerParams(
            dimension_semantics=("arbitrary",), vmem_limit_bytes=VMEM_LIMIT),
        name="attn_merge_prompt",
    )(sinks, q, k, k, v, v, kmeta, vmeta, ga, x, c, woa, woc, g)


def _attn_sample_kernel(sink_ref, q_ref, kn_ref, vn_ref, ck_ref, cv_ref, ga_ref,
                        ok_ref, ov_ref, an_ref, *, bs):
    n_pairs = N_KV_HEADS // 2
    heads_per_col = 2 * GQA_GROUP
    last_key = lax.broadcasted_iota(jnp.int32, (KV_DIM, WINDOW), 1) == WINDOW - 1
    pad = jnp.zeros((WINDOW - bs, KV_DIM), F32)
    knt = jnp.concatenate([kn_ref[...], pad], axis=0).T
    vnt = jnp.concatenate([vn_ref[...], pad], axis=0).T
    for s in range(bs):
        ok_ref[s] = jnp.where(last_key, knt[:, s:s + 1], pltpu.roll(ck_ref[s], WINDOW - 1, 1))
        ov_ref[s] = jnp.where(last_key, vnt[:, s:s + 1], pltpu.roll(cv_ref[s], WINDOW - 1, 1))

    n_rows = heads_per_col * bs
    row = lax.broadcasted_iota(jnp.int32, (n_rows, bs * WINDOW), 0)
    lane = lax.broadcasted_iota(jnp.int32, (n_rows, bs * WINDOW), 1)
    log2 = lambda n: n.bit_length() - 1
    assert bs == 1 << log2(bs) and WINDOW == 1 << log2(WINDOW)
    same_seq = (row & (bs - 1)) == (lane >> log2(WINDOW))
    head_row = lax.broadcasted_iota(jnp.int32, (n_rows, 1), 0) >> log2(bs)
    low = lax.broadcasted_iota(jnp.int32, (bs, LANES), 1) < HALF

    cols = [None] * (ATTN_DIM // LANES)
    for pair in range(n_pairs):
        sl = slice(pair * LANES, (pair + 1) * LANES)
        pieces = []
        sk = jnp.zeros((n_rows, 1), F32)
        for grp in range(GQA_GROUP):
            col = grp * n_pairs + pair
            qcol = q_ref[:, col * LANES:(col + 1) * LANES]
            pieces += [jnp.where(low, qcol, 0.0), jnp.where(low, 0.0, qcol)]
            for half in range(2):
                sink = sink_ref[(2 * pair + half) * GQA_GROUP + grp]
                sk = jnp.where(head_row == 2 * grp + half, sink, sk)
        lq = jnp.concatenate(pieces, axis=0).astype(BF16)
        kstack = jnp.concatenate([ck_ref[s, sl, :] for s in range(bs)], axis=1).astype(BF16)
        vstack = jnp.concatenate([cv_ref[s, sl, :] for s in range(bs)], axis=1).astype(BF16)
        s = jnp.where(same_seq, _dot(lq, kstack), NEG)
        k_new = jnp.concatenate([kn_ref[:, sl]] * heads_per_col, axis=0).astype(BF16)
        v_new = jnp.concatenate([vn_ref[:, sl]] * heads_per_col, axis=0).astype(BF16)
        s_new = jnp.sum(lq.astype(F32) * k_new.astype(F32), axis=-1, keepdims=True)
        mx = jnp.maximum(jnp.maximum(jnp.max(s, axis=-1, keepdims=True), s_new), sk)
        e = jnp.exp(s - mx)
        e_new = jnp.exp(s_new - mx)
        rden = 1.0 / (jnp.sum(e, axis=-1, keepdims=True) + e_new + jnp.exp(sk - mx))
        p = (e * rden).astype(BF16)
        p_new = (e_new * rden).astype(BF16).astype(F32)
        o = _dot_t(p, vstack) + p_new * v_new.astype(F32)
        for grp in range(GQA_GROUP):
            lo = o[(2 * grp) * bs:(2 * grp + 1) * bs]
            hi = o[(2 * grp + 1) * bs:(2 * grp + 2) * bs]
            cols[grp * n_pairs + pair] = jnp.where(low, lo, hi)
    an_ref[...] = _rms(jnp.concatenate(cols, axis=1), ga_ref[...])


def _attn_sample(sinks, q, kn, vn, ck, cv, ga, *, bs):
    nseq = q.shape[0]
    row = lambda i: (i, 0)
    blk3 = pl.BlockSpec((bs, KV_DIM, WINDOW), lambda i: (i, 0, 0))
    return pl.pallas_call(
        functools.partial(_attn_sample_kernel, bs=bs),
        grid=(nseq // bs,),
        in_specs=[pl.BlockSpec(memory_space=pltpu.SMEM),
                  pl.BlockSpec((bs, ATTN_DIM), row), pl.BlockSpec((bs, KV_DIM), row),
                  pl.BlockSpec((bs, KV_DIM), row), blk3, blk3,
                  pl.BlockSpec((1, ATTN_DIM), lambda i: (0, 0))],
        out_specs=[blk3, blk3, pl.BlockSpec((bs, ATTN_DIM), row)],
        out_shape=[jax.ShapeDtypeStruct(ck.shape, F32), jax.ShapeDtypeStruct(cv.shape, F32),
                   jax.ShapeDtypeStruct((nseq, ATTN_DIM), F32)],
        compiler_params=pltpu.CompilerParams(
            dimension_semantics=("parallel",), vmem_limit_bytes=VMEM_LIMIT),
        name="attn_sample",
    )(sinks, q, kn, vn, ck, cv, ga)


def _prep_wq_kernel(w_ref, wq_ref):
    low = lax.broadcasted_iota(jnp.int32, (w_ref.shape[0], LANES), 1) < HALF
    for grp in range(GQA_GROUP):
        def piece(kv, want_low):
            col = kv * (GQA_GROUP // 2) + grp // 2
            blk = w_ref[:, col * LANES:(col + 1) * LANES]
            return blk if (grp % 2 == 0) == want_low else pltpu.roll(blk, HALF, 1)
        for pair in range(N_KV_HEADS // 2):
            col = grp * (N_KV_HEADS // 2) + pair
            wq_ref[:, col * LANES:(col + 1) * LANES] = jnp.where(
                low, piece(2 * pair, True), piece(2 * pair + 1, False)).astype(BF16)


def _prep_wq(w_in2d, *, tr):
    rows = w_in2d.shape[0]
    return pl.pallas_call(
        _prep_wq_kernel,
        grid=(rows // tr,),
        in_specs=[pl.BlockSpec((tr, ATTN_DIM), lambda i: (i, 0))],
        out_specs=pl.BlockSpec((tr, ATTN_DIM), lambda i: (i, 0)),
        out_shape=jax.ShapeDtypeStruct((rows, ATTN_DIM), BF16),
        compiler_params=pltpu.CompilerParams(
            dimension_semantics=("parallel",), vmem_limit_bytes=VMEM_LIMIT),
        name="prep_wq",
    )(w_in2d)


def _group_major(w, axis):
    shape = w.shape
    w = w.reshape(shape[:axis] + (N_KV_HEADS, GQA_GROUP, HEAD_DIM) + shape[axis + 1:])
    return jnp.swapaxes(w, axis, axis + 1).reshape(shape)


def kernel(x_prompt, x_sample, cache_swa_k, cache_swa_v, state_conv, meta_tokens, g_ffn1, w1_gate, w1_up, w1_down, g_mix, w_in, attn_sinks, w_conv, g_attn_out, g_conv_out, w_out, g_ffn2, w2_gate, w2_up, w2_down, g_final):
    assert g_ffn1.shape[0] == 1, "one layer"
    batch, seq, _ = x_prompt.shape
    n_dec = x_sample.shape[0]
    assert x_sample.shape[1] == 1 and cache_swa_k.shape[2] == WINDOW

    row2d = lambda a: a.reshape(1, -1).astype(F32)
    wq = _prep_wq(w_in[0], tr=PREP_ROWS)
    wo_a = _group_major(w_out[0, :ATTN_DIM], 0).astype(BF16)
    wo_c = w_out[0, ATTN_DIM:].astype(BF16)
    ga = row2d(_group_major(g_attn_out[0], 0))
    gc = row2d(g_conv_out[0])
    sinks = attn_sinks[0].astype(F32)
    wc = w_conv[0].astype(F32)

    xp = x_prompt.reshape(batch * seq, D_MODEL)
    xt = jnp.concatenate([x_sample[:, 0, :], meta_tokens.astype(x_prompt.dtype)], axis=0)
    n_tail = xt.shape[0]

    xh1, w1g, w1u, w1d = _ffn(xp[:FFN_TM], row2d(g_ffn1[0]), w1_gate[0], w1_up[0], w1_down[0],
                              tm=FFN_TM, tf=HEAD_TF, name="ffn1_head")
    xt1 = _ffn(xt, row2d(g_ffn1[0]), w1g, w1u, w1d, tm=n_tail, tf=FFN_TF, name="ffn1_tail")
    xp1, wr = _ffn(xp, row2d(g_ffn1[0]), w1g, w1u, w1d, tm=FFN_TM, tf=FFN_TF,
                   side_cast=(w_in[0], ATTN_DIM), head=xh1, name="ffn1_prompt")

    st = state_conv[0]
    zpad = jnp.zeros((N_META, CONV_DIM), F32)
    h0 = jnp.concatenate([st[:, 0, :], zpad], axis=0)
    h1 = jnp.concatenate([st[:, 1, :], zpad], axis=0)
    qt, kt, vt, ut, cnt = _proj_tail(xt1, row2d(g_mix[0]), wq, wr, wc, gc, h0, h1)
    kmeta = jnp.pad(kt[n_dec:], ((WINDOW - N_META, 0), (0, 0)))
    vmeta = jnp.pad(vt[n_dec:], ((WINDOW - N_META, 0), (0, 0)))
    umeta = ut[n_tail - 8:]

    qp, kp, vp, cnp, cstate, w2g, w2u, w2d = _proj_prompt(
        xp1, row2d(g_mix[0]), wq, wr, wc, gc, umeta, (w2_gate[0], w2_up[0], w2_down[0]),
        tm=PROJ_TM, seq=seq)
    xp2, xpn = _attn_merge(sinks, qp, kp, vp, kmeta, vmeta, ga, xp1, cnp, wo_a, wo_c,
                           row2d(g_ffn2[0]), seq=seq, nsub=ATTN_NSUB)

    to_channel_major = lambda c: jnp.transpose(c[0], (0, 2, 3, 1)).reshape(n_dec, KV_DIM, WINDOW)
    from_channel_major = lambda c: jnp.transpose(
        c.reshape(n_dec, N_KV_HEADS, HEAD_DIM, WINDOW), (0, 3, 1, 2))[None]
    nk, nv, ans = _attn_sample(sinks, qt[:n_dec], kt[:n_dec], vt[:n_dec],
                               to_channel_major(cache_swa_k), to_channel_major(cache_swa_v), ga,
                               bs=SAMPLE_BS)

    xs2, xsn = _merge(xt1[:n_dec], ans, cnt[:n_dec], wo_a, wo_c, row2d(g_ffn2[0]), tm=n_dec,
                      name="merge_tail")
    ys = _ffn(xs2, xsn, w2g, w2u, w2d, tm=n_dec, tf=FFN_TF, g_final=row2d(g_final),
              name="ffn2_tail")
    yp = _ffn(xp2, xpn, w2g, w2u, w2d, tm=FFN_TM, tf=FFN_TF, g_final=row2d(g_final),
              name="ffn2_prompt")

    kv_shape = (1, batch, WINDOW, N_KV_HEADS, HEAD_DIM)
    new_k_prompt = kp.reshape(batch, seq, KV_DIM)[:, seq - WINDOW:].reshape(kv_shape)
    new_v_prompt = vp.reshape(batch, seq, KV_DIM)[:, seq - WINDOW:].reshape(kv_shape)
    new_conv_sample = jnp.stack([st[:, 1, :], ut[:n_dec]], axis=1)[None]
    return (yp.reshape(batch, seq, D_MODEL), ys.reshape(n_dec, 1, D_MODEL),
            new_k_prompt, new_v_prompt, cstate[None],
            from_channel_major(nk), from_channel_major(nv), new_conv_sample)
```

```python
import functools

import jax
import jax.numpy as jnp
from jax import lax
from jax.experimental import pallas as pl
from jax.experimental.pallas import tpu as pltpu

D_MODEL = 2048
D_FF = 5632
N_META = 16
ATTN_DIM = 1024
CONV_DIM = 1024
HEAD_DIM = 64
N_HEADS = 16
N_KV_HEADS = 4
GQA_GROUP = 4
KV_DIM = 256
WINDOW = 128
IN_DIM = ATTN_DIM + 2 * KV_DIM + 3 * CONV_DIM
EPS = 1e-5
NEG = -1e30

LANES = 128
HALF = LANES // 2
VMEM_LIMIT = 60 * 1024 * 1024
FFN_TM, FFN_TF = 1024, 512
HEAD_TF = 256
FFN_EDGE_ROWS = 256
PROJ_TM = 512
PROJ_ROW_CHUNKS = 2
CAST_TF = 256
ATTN_NSUB = 4
PREP_ROWS = 512
SAMPLE_BS = 16

F32 = jnp.float32
BF16 = jnp.bfloat16


def _rms(x, g):
    return x * lax.rsqrt(jnp.mean(x * x, axis=-1, keepdims=True) + EPS) * g


def _dot(a, b):
    return jnp.dot(a, b, preferred_element_type=F32)


def _dot_t(a, b):
    return lax.dot_general(a, b, (((1,), (1,)), ((), ())), preferred_element_type=F32)


def _resident(shape):
    return pl.BlockSpec(shape, lambda *_: (0,) * len(shape), pipeline_mode=pl.Buffered(1))


def _ffn_kernel(*refs, n_ff, normed_input, has_final, cast_weights, side_chunks, prefetch_x,
                head_given):
    refs = list(refs)
    x_ref, norm_ref, wg_ref, wu_ref, wd_ref = refs[:5]
    refs = refs[5:]
    if has_final:
        gf_ref = refs.pop(0)
    if side_chunks:
        side_src_ref = refs.pop(0)
    if head_given:
        head_hbm = refs.pop(0)
    out_ref = refs.pop(0)
    if cast_weights:
        wgb_ref, wub_ref, wdb_ref = refs[:3]
        refs = refs[3:]
    if side_chunks:
        side_dst_ref = refs.pop(0)
    xn_ref = norm_ref if normed_input else refs.pop(0)
    i = pl.program_id(0)
    j = pl.program_id(1)

    if prefetch_x:
        x_hbm, x_ref, x_sem = x_ref, refs.pop(0), refs.pop(0)
        tm = x_ref.shape[0]

        def x_copy(tile):
            return pltpu.make_async_copy(x_hbm.at[pl.ds(tile * tm, tm), :], x_ref, x_sem)

        @pl.when(j == 0)
        def _():
            pl.when(i == 0)(lambda: x_copy(0).start())
            x_copy(i).wait()

        pl.when((j == 1) & (i + 1 < pl.num_programs(0)))(lambda: x_copy(i + 1).start())

    if head_given:
        head_sem = refs.pop(0)

        @pl.when((i == 0) & (j == 0))
        def _():
            head_copy = pltpu.make_async_copy(head_hbm, out_ref, head_sem)
            head_copy.start()
            head_copy.wait()

    if side_chunks:
        @pl.when(pl.program_id(0) * n_ff + j < side_chunks)
        def _():
            side_dst_ref[...] = side_src_ref[...].astype(BF16)

    def chunk_step(first, last, chunks):
        wg, wu, wd = wg_ref, wu_ref, wd_ref
        if cast_weights:
            wgb_ref[...] = wg_ref[...].astype(BF16)
            wub_ref[...] = wu_ref[...].astype(BF16)
            wdb_ref[...] = wd_ref[...].astype(BF16)
            wg, wu, wd = wgb_ref, wub_ref, wdb_ref
        cm = out_ref.shape[0] // chunks
        for c in range(chunks):
            rows = slice(c * cm, (c + 1) * cm)
            if first and not normed_input:
                xn = _rms(x_ref[rows, :], norm_ref[...]).astype(BF16)
                xn_ref[rows, :] = xn
            else:
                xn = xn_ref[rows, :]
            gate = _dot(xn, wg[...])
            up = _dot(xn, wu[...])
            h = (gate * jax.nn.sigmoid(gate) * up * 0.5).astype(BF16)
            acc = (x_ref[rows, :] if first else out_ref[rows, :]) + _dot(h, wd[...])
            if last and has_final:
                acc = _rms(acc, gf_ref[...])
            out_ref[rows, :] = acc

    row_chunks = max(1, out_ref.shape[0] // FFN_EDGE_ROWS)
    first_chunks = 1 if normed_input else row_chunks
    last_chunks = row_chunks if has_final else 1
    if n_ff == 1:
        chunk_step(True, True, row_chunks)
        return
    active = (i > 0) if head_given else True
    pl.when(active & (j == 0))(lambda: chunk_step(True, False, first_chunks))
    if has_final:
        pl.when(active & (j > 0) & (j < n_ff - 1))(lambda: chunk_step(False, False, 1))
        pl.when(active & (j == n_ff - 1))(lambda: chunk_step(False, True, last_chunks))
    else:
        pl.when(active & (j > 0))(lambda: chunk_step(False, False, 1))


def _ffn(x, norm, wg, wu, wd, *, tm, tf, g_final=None, side_cast=None, head=None, n_rows=None,
         name):
    m = n_rows or x.shape[0]
    assert m % tm == 0 and D_FF % tf == 0 and FFN_TF % tf == 0
    n_ff = D_FF // tf
    normed_input = norm.shape[0] == m
    cast_weights = wg.dtype == F32
    assert not cast_weights or m == tm
    assert head is None or (head.shape == (tm, D_MODEL) and m > tm)
    side_src, side_col0 = side_cast if side_cast is not None else (None, 0)
    side_chunks = 0 if side_src is None else (side_src.shape[1] - side_col0) // CAST_TF
    assert side_chunks <= (m // tm) * n_ff and side_col0 % CAST_TF == 0
    row = lambda i, j: (i, 0)
    side_chunk = lambda i, j: jnp.minimum(i * n_ff + j, side_chunks - 1)
    side_in = pl.BlockSpec((D_MODEL, CAST_TF),
                           lambda i, j: (0, side_col0 // CAST_TF + side_chunk(i, j)))
    side_out = pl.BlockSpec((D_MODEL, CAST_TF), lambda i, j: (0, side_chunk(i, j)))
    wj = (lambda i, j: jnp.where(i == 0, 0, j)) if head is not None else (lambda i, j: j)
    per_ff = FFN_TF // tf
    wd_spec = pl.BlockSpec((tf, D_MODEL), lambda i, j: (wj(i, j), 0))
    w_f32_specs = [pl.BlockSpec((D_MODEL, tf), lambda i, j: (0, j))] * 2 + [wd_spec]
    w_bf16_specs = [pl.BlockSpec((None, D_MODEL, tf),
                                 lambda i, j: (wj(i, j) // per_ff, 0, wj(i, j) % per_ff))] * 2
    w_bf16_specs += [wd_spec]
    prefetch_x = m // tm > 1 and n_ff > 1
    in_specs = [pl.BlockSpec(memory_space=pl.ANY) if prefetch_x
                else pl.BlockSpec((tm, D_MODEL), row, pipeline_mode=pl.Buffered(1)),
                pl.BlockSpec((tm, D_MODEL), row) if normed_input else _resident((1, D_MODEL)),
                *(w_f32_specs if cast_weights else w_bf16_specs)]
    args = [x, norm, wg, wu, wd]
    if g_final is not None:
        in_specs.append(_resident((1, D_MODEL)))
        args.append(g_final)
    if side_chunks:
        in_specs.append(side_in)
        args.append(side_src)
    if head is not None:
        in_specs.append(pl.BlockSpec(memory_space=pl.ANY))
        args.append(head)
    out_specs = [pl.BlockSpec((tm, D_MODEL), row)]
    out_shape = [jax.ShapeDtypeStruct((m, D_MODEL), F32)]
    if cast_weights:
        out_specs += w_bf16_specs
        out_shape += [jax.ShapeDtypeStruct((D_FF // FFN_TF, D_MODEL, FFN_TF), BF16)] * 2
        out_shape += [jax.ShapeDtypeStruct(wd.shape, BF16)]
    if side_chunks:
        out_specs.append(side_out)
        out_shape.append(jax.ShapeDtypeStruct((D_MODEL, side_chunks * CAST_TF), BF16))
    scratch_shapes = [] if normed_input else [pltpu.VMEM((tm, D_MODEL), BF16)]
    if prefetch_x:
        scratch_shapes += [pltpu.VMEM((tm, D_MODEL), F32), pltpu.SemaphoreType.DMA(())]
    if head is not None:
        scratch_shapes += [pltpu.SemaphoreType.DMA(())]
    ordered_rows = bool(side_chunks) or prefetch_x
    outs = pl.pallas_call(
        functools.partial(_ffn_kernel, n_ff=n_ff, normed_input=normed_input,
                          has_final=g_final is not None, cast_weights=cast_weights,
                          side_chunks=side_chunks, prefetch_x=prefetch_x,
                          head_given=head is not None),
        grid=(m // tm, n_ff),
        in_specs=in_specs,
        out_specs=out_specs,
        out_shape=out_shape,
        scratch_shapes=scratch_shapes,
        compiler_params=pltpu.CompilerParams(
            dimension_semantics=("arbitrary" if ordered_rows else "parallel", "arbitrary"),
            vmem_limit_bytes=VMEM_LIMIT),
        name=name,
    )(*args)
    return outs if len(outs) > 1 else outs[0]


def _merge_kernel(x_ref, a_ref, c_ref, woa_ref, woc_ref, g_ref, x2_ref, xn_ref):
    x = x_ref[...] + (_dot(a_ref[...].astype(BF16), woa_ref[...]) +
                      _dot(c_ref[...].astype(BF16), woc_ref[...]))
    x2_ref[...] = x
    xn_ref[...] = _rms(x, g_ref[...]).astype(BF16)


def _merge(x, a, c, woa, woc, g, *, tm, name):
    m = x.shape[0]
    assert m % tm == 0
    row = lambda i: (i, 0)
    return pl.pallas_call(
        _merge_kernel,
        grid=(m // tm,),
        in_specs=[pl.BlockSpec((tm, D_MODEL), row), pl.BlockSpec((tm, ATTN_DIM), row),
                  pl.BlockSpec((tm, CONV_DIM), row), _resident((ATTN_DIM, D_MODEL)),
                  _resident((CONV_DIM, D_MODEL)), _resident((1, D_MODEL))],
        out_specs=[pl.BlockSpec((tm, D_MODEL), row), pl.BlockSpec((tm, D_MODEL), row)],
        out_shape=[jax.ShapeDtypeStruct((m, D_MODEL), F32),
                   jax.ShapeDtypeStruct((m, D_MODEL), BF16)],
        compiler_params=pltpu.CompilerParams(
            dimension_semantics=("parallel",), vmem_limit_bytes=VMEM_LIMIT),
        name=name,
    )(x, a, c, woa, woc, g)


K0, V0, B0, C0, H0, R_END = 0, 256, 512, 1536, 2560, 3584


def _project_conv(x_ref, g_ref, wr_ref):
    xn = _rms(x_ref[...], g_ref[...]).astype(BF16)
    gate_b = _dot(xn, wr_ref[:, B0:C0])
    u = _dot(xn, wr_ref[:, C0:H0]) * _dot(xn, wr_ref[:, H0:R_END])
    return xn, gate_b, u


def _project_qkv(xn, wq_ref, wr_ref):
    q = _dot(xn, wq_ref[...]) * (HEAD_DIM ** -0.5)
    return q, _dot(xn, wr_ref[:, K0:V0]), _dot(xn, wr_ref[:, V0:B0])


def _proj_prompt_kernel(x_ref, g_ref, wq_ref, wr_ref, wc_ref, gc_ref, umeta_ref,
                        fg_ref, fu_ref, fd_ref,
                        q_ref, k_ref, v_ref, cn_ref, cst_ref, bg_ref, bu_ref, bd_ref,
                        us_ref, *, tm, tiles_per_seq):
    t = pl.program_id(0) % tiles_per_seq

    @pl.when(t == 0)
    def _():
        us_ref[0:8, :] = umeta_ref[...]

    @pl.when(t != 0)
    def _():
        us_ref[0:8, :] = us_ref[tm:tm + 8, :]

    hm = tm // PROJ_ROW_CHUNKS
    casts = ((fg_ref, bg_ref), (fu_ref, bu_ref), (fd_ref, bd_ref))
    for h in range(PROJ_ROW_CHUNKS):
        rows = slice(h * hm, (h + 1) * hm)
        for src, dst in casts[h::PROJ_ROW_CHUNKS]:
            dst[...] = src[...].astype(BF16)
        xn, gate_b, u = _project_conv(x_ref.at[rows], g_ref, wr_ref)
        us_ref[8 + h * hm:8 + (h + 1) * hm, :] = u
        y = (wc_ref[0:1, :] * us_ref[6 + h * hm:6 + (h + 1) * hm, :]
             + wc_ref[1:2, :] * us_ref[7 + h * hm:7 + (h + 1) * hm, :] + wc_ref[2:3, :] * u)
        cn_ref[rows, :] = _rms(gate_b * y, gc_ref[...]).astype(BF16)
        q, k, v = _project_qkv(xn, wq_ref, wr_ref)
        q_ref[rows, :] = q.astype(BF16)
        k_ref[rows, :] = k
        v_ref[rows, :] = v
    cst_ref[0] = us_ref[tm + 6:tm + 8, :]


def _proj_prompt(x, g, wq, wr, wc, gc, umeta, ffn_w, *, tm, seq):
    m = x.shape[0]
    tiles_per_seq = seq // tm
    n_steps = m // tm
    n_cast = D_FF // CAST_TF
    assert D_FF % CAST_TF == 0 and n_cast <= n_steps
    row = lambda i: (i, 0)
    chunk = lambda i: jnp.minimum(i, n_cast - 1)
    cast_cols = pl.BlockSpec((D_MODEL, CAST_TF), lambda i: (0, chunk(i)))
    cast_rows = pl.BlockSpec((CAST_TF, D_MODEL), lambda i: (chunk(i), 0))
    per_ff = FFN_TF // CAST_TF
    assert FFN_TF % CAST_TF == 0
    cast_cols_out = pl.BlockSpec((None, D_MODEL, CAST_TF),
                                 lambda i: (chunk(i) // per_ff, 0, chunk(i) % per_ff))
    chunked = jax.ShapeDtypeStruct((D_FF // FFN_TF, D_MODEL, FFN_TF), BF16)
    return pl.pallas_call(
        functools.partial(_proj_prompt_kernel, tm=tm, tiles_per_seq=tiles_per_seq),
        grid=(n_steps,),
        in_specs=[pl.BlockSpec((tm, D_MODEL), row), _resident((1, D_MODEL)),
                  _resident((D_MODEL, ATTN_DIM)), _resident((D_MODEL, R_END)),
                  _resident((3, CONV_DIM)), _resident((1, CONV_DIM)), _resident((8, CONV_DIM)),
                  cast_cols, cast_cols, cast_rows],
        out_specs=[pl.BlockSpec((tm, ATTN_DIM), row), pl.BlockSpec((tm, KV_DIM), row),
                   pl.BlockSpec((tm, KV_DIM), row), pl.BlockSpec((tm, CONV_DIM), row),
                   pl.BlockSpec((1, 2, CONV_DIM), lambda i: (i // tiles_per_seq, 0, 0)),
                   cast_cols_out, cast_cols_out, cast_rows],
        out_shape=[jax.ShapeDtypeStruct((m, ATTN_DIM), BF16),
                   jax.ShapeDtypeStruct((m, KV_DIM), F32),
                   jax.ShapeDtypeStruct((m, KV_DIM), F32),
                   jax.ShapeDtypeStruct((m, CONV_DIM), BF16),
                   jax.ShapeDtypeStruct((m // seq, 2, CONV_DIM), F32),
                   chunked, chunked, jax.ShapeDtypeStruct(ffn_w[2].shape, BF16)],
        scratch_shapes=[pltpu.VMEM((tm + 8, CONV_DIM), F32)],
        compiler_params=pltpu.CompilerParams(
            dimension_semantics=("arbitrary",), vmem_limit_bytes=VMEM_LIMIT),
        name="proj_prompt",
    )(x, g, wq, wr, wc, gc, umeta, *ffn_w)


def _proj_tail_kernel(x_ref, g_ref, wq_ref, wr_ref, wc_ref, gc_ref, h0_ref, h1_ref,
                      q_ref, k_ref, v_ref, u_ref, cn_ref):
    xn, gate_b, u = _project_conv(x_ref, g_ref, wr_ref)
    q, k, v = _project_qkv(xn, wq_ref, wr_ref)
    q_ref[...] = q
    k_ref[...] = k
    v_ref[...] = v
    u_ref[...] = u
    y = wc_ref[0:1, :] * h0_ref[...] + wc_ref[1:2, :] * h1_ref[...] + wc_ref[2:3, :] * u
    cn_ref[...] = _rms(gate_b * y, gc_ref[...]).astype(BF16)


def _proj_tail(x, g, wq, wr, wc, gc, h0, h1):
    m = x.shape[0]
    full = lambda shape: pl.BlockSpec(shape, lambda i: (0,) * len(shape))
    return pl.pallas_call(
        _proj_tail_kernel,
        grid=(1,),
        in_specs=[full((m, D_MODEL)), full((1, D_MODEL)), _resident((D_MODEL, ATTN_DIM)),
                  _resident((D_MODEL, R_END)),
                  full((3, CONV_DIM)), full((1, CONV_DIM)), full((m, CONV_DIM)),
                  full((m, CONV_DIM))],
        out_specs=[full((m, ATTN_DIM)), full((m, KV_DIM)), full((m, KV_DIM)),
                   full((m, CONV_DIM)), full((m, CONV_DIM))],
        out_shape=[jax.ShapeDtypeStruct((m, ATTN_DIM), F32),
                   jax.ShapeDtypeStruct((m, KV_DIM), F32),
                   jax.ShapeDtypeStruct((m, KV_DIM), F32),
                   jax.ShapeDtypeStruct((m, CONV_DIM), F32),
                   jax.ShapeDtypeStruct((m, CONV_DIM), BF16)],
        compiler_params=pltpu.CompilerParams(
            dimension_semantics=("arbitrary",), vmem_limit_bytes=VMEM_LIMIT),
        name="proj_tail",
    )(x, g, wq, wr, wc, gc, h0, h1)


class _BandedAttention:
    def __init__(self, first, sink_ref, q_ref, kprev, kcur, vprev, vcur, ga_ref, o_ref, nsub):
        self.sink_ref, self.q_ref, self.ga_ref, self.o_ref = sink_ref, q_ref, ga_ref, o_ref
        self.n_pairs = N_KV_HEADS // 2
        self.units = [(sb, pair) for sb in range(nsub) for pair in range(self.n_pairs)]
        kall = jnp.concatenate([kprev, kcur], axis=0)
        vall = jnp.concatenate([vprev, vcur], axis=0)
        c = lax.broadcasted_iota(jnp.int32, (2 * WINDOW, WINDOW), 0)
        r = lax.broadcasted_iota(jnp.int32, (2 * WINDOW, WINDOW), 1)
        self.band = (c >= r) & (c <= r + WINDOW)
        self.band_first = self.band & ((c >= WINDOW - N_META) | jnp.logical_not(first))
        low = lax.broadcasted_iota(jnp.int32, kall.shape[:1] + (LANES,), 1) < HALF
        self.k_half, self.vt_half = [], []
        for pair in range(self.n_pairs):
            kcol = kall[:, pair * LANES:(pair + 1) * LANES]
            vcol = vall[:, pair * LANES:(pair + 1) * LANES]
            self.k_half.append((jnp.where(low, kcol, 0.0).astype(BF16),
                                jnp.where(low, 0.0, kcol).astype(BF16)))
            self.vt_half.append((jnp.where(low, vcol, 0.0).T.astype(BF16),
                                 jnp.where(low, 0.0, vcol).T.astype(BF16)))
        self.outs = {}

    def scores(self, unit):
        sb, pair = unit
        rows = slice(sb * WINDOW, (sb + 1) * WINDOW)
        keys = slice(sb * WINDOW, (sb + 2) * WINDOW)
        kst = jnp.concatenate([self.k_half[pair][0][keys], self.k_half[pair][1][keys]], axis=0)
        qst = jnp.concatenate(
            [self.q_ref[rows, (grp * self.n_pairs + pair) * LANES:
                        (grp * self.n_pairs + pair + 1) * LANES]
             for grp in range(GQA_GROUP)], axis=0)
        return _dot_t(kst, qst)

    def softmax(self, unit, st_all):
        sb, pair = unit
        valid = self.band_first if sb == 0 else self.band
        pt_rows = []
        for half in range(2):
            pt_cols = []
            for grp in range(GQA_GROUP):
                sink = self.sink_ref[(2 * pair + half) * GQA_GROUP + grp]
                st = st_all[half * 2 * WINDOW:(half + 1) * 2 * WINDOW,
                            grp * WINDOW:(grp + 1) * WINDOW]
                st = jnp.where(valid, st, NEG)
                mx = jnp.maximum(jnp.max(st, axis=0, keepdims=True), sink)
                e = jnp.exp(st - mx)
                den = jnp.sum(e, axis=0, keepdims=True) + jnp.exp(sink - mx)
                pt_cols.append((e * (1.0 / den)).astype(BF16))
            pt_rows.append(jnp.concatenate(pt_cols, axis=1))
        return jnp.concatenate(pt_rows, axis=0)

    def values(self, unit, pt):
        sb, pair = unit
        keys = slice(sb * WINDOW, (sb + 2) * WINDOW)
        vst = jnp.concatenate([self.vt_half[pair][0][:, keys], self.vt_half[pair][1][:, keys]],
                              axis=1)
        ot_all = _dot(vst, pt)
        for grp in range(GQA_GROUP):
            self.outs[(sb, grp * self.n_pairs + pair)] = ot_all[:, grp * WINDOW:(grp + 1) * WINDOW]
        if pair == self.n_pairs - 1:
            self._normalise(sb)

    def _normalise(self, sb):
        rows = slice(sb * WINDOW, (sb + 1) * WINDOW)
        cols = range(ATTN_DIM // LANES)
        ssq = sum(jnp.sum(self.outs[(sb, col)] ** 2, axis=0, keepdims=True) for col in cols)
        inv = lax.rsqrt(ssq * (1.0 / ATTN_DIM) + EPS)
        for col in cols:
            sl = slice(col * LANES, (col + 1) * LANES)
            self.o_ref[rows, sl] = ((self.outs.pop((sb, col)) * inv).T
                                    * self.ga_ref[:, sl]).astype(BF16)


def _attn_merge_kernel(sink_ref, q_ref, kp_ref, kc_ref, vp_ref, vc_ref, km_ref, vm_ref, ga_ref,
                       x_ref, c_ref, woa_ref, woc_ref, g_ref, x2_ref, xn_ref,
                       an_new_ref, an_old_ref, *, nsub, tiles_per_seq, n_tiles):
    t = pl.program_id(0)

    @pl.when(t == 0)
    def _():
        an_new_ref[...] = jnp.zeros(an_new_ref.shape, an_new_ref.dtype)

    an_old_ref[...] = an_new_ref[...]

    first = (jnp.minimum(t, n_tiles - 1) % tiles_per_seq) == 0
    attn = _BandedAttention(first, sink_ref, q_ref,
                            jnp.where(first, km_ref[...], kp_ref[...]), kc_ref[...],
                            jnp.where(first, vm_ref[...], vp_ref[...]), vc_ref[...],
                            ga_ref, an_new_ref, nsub)
    units = attn.units
    n_chunks = len(units)
    cw = D_MODEL // n_chunks
    ssq = jnp.zeros((x_ref.shape[0], 1), F32)
    st_next = attn.scores(units[0])
    for u, unit in enumerate(units):
        cols = slice(u * cw, (u + 1) * cw)
        xc = x_ref[:, cols] + (_dot(an_old_ref[...], woa_ref[:, cols]) +
                               _dot(c_ref[...], woc_ref[:, cols]))
        x2_ref[:, cols] = xc
        ssq = ssq + jnp.sum(xc * xc, axis=-1, keepdims=True)
        st_all, st_next = st_next, (attn.scores(units[u + 1]) if u + 1 < n_chunks else None)
        attn.values(unit, attn.softmax(unit, st_all))
    inv = lax.rsqrt(ssq * (1.0 / D_MODEL) + EPS)
    for u in range(n_chunks):
        cols = slice(u * cw, (u + 1) * cw)
        xn_ref[:, cols] = (x2_ref[:, cols] * inv * g_ref[:, cols]).astype(BF16)


def _attn_merge(sinks, q, k, v, kmeta, vmeta, ga, x, c, woa, woc, g, *, seq, nsub):
    m = x.shape[0]
    tm = nsub * WINDOW
    n_tiles = m // tm
    tiles_per_seq = seq // tm
    cur = lambda t: (jnp.minimum(t, n_tiles - 1), 0)
    prev = lambda t: (jnp.maximum(jnp.minimum(t, n_tiles - 1) * nsub - 1, 0), 0)
    lag = lambda t: (jnp.maximum(t - 1, 0), 0)
    return pl.pallas_call(
        functools.partial(_attn_merge_kernel, nsub=nsub, tiles_per_seq=tiles_per_seq,
                          n_tiles=n_tiles),
        grid=(n_tiles + 1,),
        in_specs=[pl.BlockSpec(memory_space=pltpu.SMEM),
                  pl.BlockSpec((tm, ATTN_DIM), cur),
                  pl.BlockSpec((WINDOW, KV_DIM), prev), pl.BlockSpec((tm, KV_DIM), cur),
                  pl.BlockSpec((WINDOW, KV_DIM), prev), pl.BlockSpec((tm, KV_DIM), cur),
                  _resident((WINDOW, KV_DIM)), _resident((WINDOW, KV_DIM)),
                  _resident((1, ATTN_DIM)),
                  pl.BlockSpec((tm, D_MODEL), lag), pl.BlockSpec((tm, CONV_DIM), lag),
                  _resident((ATTN_DIM, D_MODEL)), _resident((CONV_DIM, D_MODEL)),
                  _resident((1, D_MODEL))],
        out_specs=[pl.BlockSpec((tm, D_MODEL), lag), pl.BlockSpec((tm, D_MODEL), lag)],
        out_shape=[jax.ShapeDtypeStruct((m, D_MODEL), F32),
                   jax.ShapeDtypeStruct((m, D_MODEL), BF16)],
        scratch_shapes=[pltpu.VMEM((tm, ATTN_DIM), BF16), pltpu.VMEM((tm, ATTN_DIM), BF16)],
        compiler_params=pltpu.CompilerParams(
            dimension_semantics=("arbitrary",), vmem_limit_bytes=VMEM_LIMIT),
        name="attn_merge_prompt",
    )(sinks, q, k, k, v, v, kmeta, vmeta, ga, x, c, woa, woc, g)


def _attn_sample_kernel(sink_ref, q_ref, kn_ref, vn_ref, ck_ref, cv_ref, ga_ref,
                        ok_ref, ov_ref, an_ref, *, bs):
    n_pairs = N_KV_HEADS // 2
    heads_per_col = 2 * GQA_GROUP
    last_key = lax.broadcasted_iota(jnp.int32, (KV_DIM, WINDOW), 1) == WINDOW - 1
    pad = jnp.zeros((WINDOW - bs, KV_DIM), F32)
    knt = jnp.concatenate([kn_ref[...], pad], axis=0).T
    vnt = jnp.concatenate([vn_ref[...], pad], axis=0).T
    for s in range(bs):
        ok_ref[s] = jnp.where(last_key, knt[:, s:s + 1], pltpu.roll(ck_ref[s], WINDOW - 1, 1))
        ov_ref[s] = jnp.where(last_key, vnt[:, s:s + 1], pltpu.roll(cv_ref[s], WINDOW - 1, 1))

    n_rows = heads_per_col * bs
    row = lax.broadcasted_iota(jnp.int32, (n_rows, bs * WINDOW), 0)
    lane = lax.broadcasted_iota(jnp.int32, (n_rows, bs * WINDOW), 1)
    log2 = lambda n: n.bit_length() - 1
    assert bs == 1 << log2(bs) and WINDOW == 1 << log2(WINDOW)
    same_seq = (row & (bs - 1)) == (lane >> log2(WINDOW))
    head_row = lax.broadcasted_iota(jnp.int32, (n_rows, 1), 0) >> log2(bs)
    low = lax.broadcasted_iota(jnp.int32, (bs, LANES), 1) < HALF

    cols = [None] * (ATTN_DIM // LANES)
    for pair in range(n_pairs):
        sl = slice(pair * LANES, (pair + 1) * LANES)
        pieces = []
        sk = jnp.zeros((n_rows, 1), F32)
        for grp in range(GQA_GROUP):
            col = grp * n_pairs + pair
            qcol = q_ref[:, col * LANES:(col + 1) * LANES]
            pieces += [jnp.where(low, qcol, 0.0), jnp.where(low, 0.0, qcol)]
            for half in range(2):
                sink = sink_ref[(2 * pair + half) * GQA_GROUP + grp]
                sk = jnp.where(head_row == 2 * grp + half, sink, sk)
        lq = jnp.concatenate(pieces, axis=0).astype(BF16)
        kstack = jnp.concatenate([ck_ref[s, sl, :] for s in range(bs)], axis=1).astype(BF16)
        vstack = jnp.concatenate([cv_ref[s, sl, :] for s in range(bs)], axis=1).astype(BF16)
        s = jnp.where(same_seq, _dot(lq, kstack), NEG)
        k_new = jnp.concatenate([kn_ref[:, sl]] * heads_per_col, axis=0).astype(BF16)
        v_new = jnp.concatenate([vn_ref[:, sl]] * heads_per_col, axis=0).astype(BF16)
        s_new = jnp.sum(lq.astype(F32) * k_new.astype(F32), axis=-1, keepdims=True)
        mx = jnp.maximum(jnp.maximum(jnp.max(s, axis=-1, keepdims=True), s_new), sk)
        e = jnp.exp(s - mx)
        e_new = jnp.exp(s_new - mx)
        rden = 1.0 / (jnp.sum(e, axis=-1, keepdims=True) + e_new + jnp.exp(sk - mx))
        p = (e * rden).astype(BF16)
        p_new = (e_new * rden).astype(BF16).astype(F32)
        o = _dot_t(p, vstack) + p_new * v_new.astype(F32)
        for grp in range(GQA_GROUP):
            lo = o[(2 * grp) * bs:(2 * grp + 1) * bs]
            hi = o[(2 * grp + 1) * bs:(2 * grp + 2) * bs]
            cols[grp * n_pairs + pair] = jnp.where(low, lo, hi)
    an_ref[...] = _rms(jnp.concatenate(cols, axis=1), ga_ref[...])


def _attn_sample(sinks, q, kn, vn, ck, cv, ga, *, bs):
    nseq = q.shape[0]
    row = lambda i: (i, 0)
    blk3 = pl.BlockSpec((bs, KV_DIM, WINDOW), lambda i: (i, 0, 0))
    return pl.pallas_call(
        functools.partial(_attn_sample_kernel, bs=bs),
        grid=(nseq // bs,),
        in_specs=[pl.BlockSpec(memory_space=pltpu.SMEM),
                  pl.BlockSpec((bs, ATTN_DIM), row), pl.BlockSpec((bs, KV_DIM), row),
                  pl.BlockSpec((bs, KV_DIM), row), blk3, blk3,
                  pl.BlockSpec((1, ATTN_DIM), lambda i: (0, 0))],
        out_specs=[blk3, blk3, pl.BlockSpec((bs, ATTN_DIM), row)],
        out_shape=[jax.ShapeDtypeStruct(ck.shape, F32), jax.ShapeDtypeStruct(cv.shape, F32),
                   jax.ShapeDtypeStruct((nseq, ATTN_DIM), F32)],
        compiler_params=pltpu.CompilerParams(
            dimension_semantics=("parallel",), vmem_limit_bytes=VMEM_LIMIT),
        name="attn_sample",
    )(sinks, q, kn, vn, ck, cv, ga)


def _prep_wq_kernel(w_ref, wq_ref):
    low = lax.broadcasted_iota(jnp.int32, (w_ref.shape[0], LANES), 1) < HALF
    for grp in range(GQA_GROUP):
        def piece(kv, want_low):
            col = kv * (GQA_GROUP // 2) + grp // 2
            blk = w_ref[:, col * LANES:(col + 1) * LANES]
            return blk if (grp % 2 == 0) == want_low else pltpu.roll(blk, HALF, 1)
        for pair in range(N_KV_HEADS // 2):
            col = grp * (N_KV_HEADS // 2) + pair
            wq_ref[:, col * LANES:(col + 1) * LANES] = jnp.where(
                low, piece(2 * pair, True), piece(2 * pair + 1, False)).astype(BF16)


def _prep_wq(w_in2d, *, tr):
    rows = w_in2d.shape[0]
    return pl.pallas_call(
        _prep_wq_kernel,
        grid=(rows // tr,),
        in_specs=[pl.BlockSpec((tr, ATTN_DIM), lambda i: (i, 0))],
        out_specs=pl.BlockSpec((tr, ATTN_DIM), lambda i: (i, 0)),
        out_shape=jax.ShapeDtypeStruct((rows, ATTN_DIM), BF16),
        compiler_params=pltpu.CompilerParams(
            dimension_semantics=("parallel",), vmem_limit_bytes=VMEM_LIMIT),
        name="prep_wq",
    )(w_in2d)


def _group_major(w, axis):
    shape = w.shape
    w = w.reshape(shape[:axis] + (N_KV_HEADS, GQA_GROUP, HEAD_DIM) + shape[axis + 1:])
    return jnp.swapaxes(w, axis, axis + 1).reshape(shape)


def kernel(x_prompt, x_sample, cache_swa_k, cache_swa_v, state_conv, meta_tokens, g_ffn1, w1_gate, w1_up, w1_down, g_mix, w_in, attn_sinks, w_conv, g_attn_out, g_conv_out, w_out, g_ffn2, w2_gate, w2_up, w2_down, g_final):
    assert g_ffn1.shape[0] == 1, "one layer"
    batch, seq, _ = x_prompt.shape
    n_dec = x_sample.shape[0]
    assert x_sample.shape[1] == 1 and cache_swa_k.shape[2] == WINDOW

    row2d = lambda a: a.reshape(1, -1).astype(F32)
    wq = _prep_wq(w_in[0], tr=PREP_ROWS)
    wo_a = _group_major(w_out[0, :ATTN_DIM], 0).astype(BF16)
    wo_c = w_out[0, ATTN_DIM:].astype(BF16)
    ga = row2d(_group_major(g_attn_out[0], 0))
    gc = row2d(g_conv_out[0])
    sinks = attn_sinks[0].astype(F32)
    wc = w_conv[0].astype(F32)

    xp = x_prompt.reshape(batch * seq, D_MODEL)
    xt = jnp.concatenate([x_sample[:, 0, :], meta_tokens.astype(x_prompt.dtype)], axis=0)
    n_tail = xt.shape[0]

    xh1, w1g, w1u, w1d = _ffn(xp, row2d(g_ffn1[0]), w1_gate[0], w1_up[0], w1_down[0],
                              tm=FFN_TM, tf=HEAD_TF, n_rows=FFN_TM, name="ffn1_head")
    xt1 = _ffn(xt, row2d(g_ffn1[0]), w1g, w1u, w1d, tm=n_tail, tf=FFN_TF, name="ffn1_tail")
    xp1, wr = _ffn(xp, row2d(g_ffn1[0]), w1g, w1u, w1d, tm=FFN_TM, tf=FFN_TF,
                   side_cast=(w_in[0], ATTN_DIM), head=xh1, name="ffn1_prompt")

    st = state_conv[0]
    zpad = jnp.zeros((N_META, CONV_DIM), F32)
    h0 = jnp.concatenate([st[:, 0, :], zpad], axis=0)
    h1 = jnp.concatenate([st[:, 1, :], zpad], axis=0)
    qt, kt, vt, ut, cnt = _proj_tail(xt1, row2d(g_mix[0]), wq, wr, wc, gc, h0, h1)
    kmeta = jnp.pad(kt[n_dec:], ((WINDOW - N_META, 0), (0, 0)))
    vmeta = jnp.pad(vt[n_dec:], ((WINDOW - N_META, 0), (0, 0)))
    umeta = ut[n_tail - 8:]

    qp, kp, vp, cnp, cstate, w2g, w2u, w2d = _proj_prompt(
        xp1, row2d(g_mix[0]), wq, wr, wc, gc, umeta, (w2_gate[0], w2_up[0], w2_down[0]),
        tm=PROJ_TM, seq=seq)
    xp2, xpn = _attn_merge(sinks, qp, kp, vp, kmeta, vmeta, ga, xp1, cnp, wo_a, wo_c,
                           row2d(g_ffn2[0]), seq=seq, nsub=ATTN_NSUB)

    to_channel_major = lambda c: jnp.transpose(c[0], (0, 2, 3, 1)).reshape(n_dec, KV_DIM, WINDOW)
    from_channel_major = lambda c: jnp.transpose(
        c.reshape(n_dec, N_KV_HEADS, HEAD_DIM, WINDOW), (0, 3, 1, 2))[None]
    nk, nv, ans = _attn_sample(sinks, qt[:n_dec], kt[:n_dec], vt[:n_dec],
                               to_channel_major(cache_swa_k), to_channel_major(cache_swa_v), ga,
                               bs=SAMPLE_BS)

    xs2, xsn = _merge(xt1[:n_dec], ans, cnt[:n_dec], wo_a, wo_c, row2d(g_ffn2[0]), tm=n_dec,
                      name="merge_tail")
    ys = _ffn(xs2, xsn, w2g, w2u, w2d, tm=n_dec, tf=FFN_TF, g_final=row2d(g_final),
              name="ffn2_tail")
    yp = _ffn(xp2, xpn, w2g, w2u, w2d, tm=FFN_TM, tf=FFN_TF, g_final=row2d(g_final),
              name="ffn2_prompt")

    kv_shape = (1, batch, WINDOW, N_KV_HEADS, HEAD_DIM)
    new_k_prompt = kp.reshape(batch, seq, KV_DIM)[:, seq - WINDOW:].reshape(kv_shape)
    new_v_prompt = vp.reshape(batch, seq, KV_DIM)[:, seq - WINDOW:].reshape(kv_shape)
    new_conv_sample = jnp.stack([st[:, 1, :], ut[:n_dec]], axis=1)[None]
    return (yp.reshape(batch, seq, D_MODEL), ys.reshape(n_dec, 1, D_MODEL),
            new_k_prompt, new_v_prompt, cstate[None],
            from_channel_major(nk), from_channel_major(nv), new_conv_sample)
```

```python
import functools

import jax
import jax.numpy as jnp
from jax import lax
from jax.experimental import pallas as pl
from jax.experimental.pallas import tpu as pltpu

D_MODEL = 2048
D_FF = 5632
N_META = 16
ATTN_DIM = 1024
CONV_DIM = 1024
HEAD_DIM = 64
N_HEADS = 16
N_KV_HEADS = 4
GQA_GROUP = 4
KV_DIM = 256
WINDOW = 128
IN_DIM = ATTN_DIM + 2 * KV_DIM + 3 * CONV_DIM
EPS = 1e-5
NEG = -1e30

LANES = 128
HALF = LANES // 2
VMEM_LIMIT = 60 * 1024 * 1024
FFN_TM, FFN_TF = 1024, 512
HEAD_TF = 256
FFN_EDGE_ROWS = 256
PROJ_TM = 512
PROJ_ROW_CHUNKS = 2
CAST_TF = 256
ATTN_NSUB = 4
PREP_ROWS = 512
SAMPLE_BS = 16

F32 = jnp.float32
BF16 = jnp.bfloat16


def _rms(x, g):
    return x * lax.rsqrt(jnp.mean(x * x, axis=-1, keepdims=True) + EPS) * g


def _dot(a, b):
    return jnp.dot(a, b, preferred_element_type=F32)


def _dot_t(a, b):
    return lax.dot_general(a, b, (((1,), (1,)), ((), ())), preferred_element_type=F32)


def _resident(shape):
    return pl.BlockSpec(shape, lambda *_: (0,) * len(shape), pipeline_mode=pl.Buffered(1))


def _ffn_kernel(*refs, n_ff, normed_input, has_final, cast_weights, side_chunks, prefetch_x,
                head_given, extra_rows):
    refs = list(refs)
    x_ref, norm_ref, wg_ref, wu_ref, wd_ref = refs[:5]
    refs = refs[5:]
    if has_final:
        gf_ref = refs.pop(0)
    if side_chunks:
        side_src_ref = refs.pop(0)
    if head_given:
        head_hbm = refs.pop(0)
    if extra_rows:
        xe_ref = refs.pop(0)
    out_ref = refs.pop(0)
    if cast_weights:
        wgb_ref, wub_ref, wdb_ref = refs[:3]
        refs = refs[3:]
    if side_chunks:
        side_dst_ref = refs.pop(0)
    if extra_rows:
        oute_ref = refs.pop(0)
    xn_ref = norm_ref if normed_input else refs.pop(0)
    if extra_rows:
        xne_ref = refs.pop(0)
    i = pl.program_id(0)
    j = pl.program_id(1)

    if prefetch_x:
        x_hbm, x_ref, x_sem = x_ref, refs.pop(0), refs.pop(0)
        tm = x_ref.shape[0]

        def x_copy(tile):
            return pltpu.make_async_copy(x_hbm.at[pl.ds(tile * tm, tm), :], x_ref, x_sem)

        @pl.when(j == 0)
        def _():
            pl.when(i == 0)(lambda: x_copy(0).start())
            x_copy(i).wait()

        pl.when((j == 1) & (i + 1 < pl.num_programs(0)))(lambda: x_copy(i + 1).start())

    if head_given:
        head_sem = refs.pop(0)

        @pl.when((i == 0) & (j == 0))
        def _():
            head_copy = pltpu.make_async_copy(head_hbm, out_ref, head_sem)
            head_copy.start()
            head_copy.wait()

    if side_chunks:
        @pl.when(pl.program_id(0) * n_ff + j < side_chunks)
        def _():
            side_dst_ref[...] = side_src_ref[...].astype(BF16)

    def chunk_step(first, last, chunks):
        wg, wu, wd = wg_ref, wu_ref, wd_ref
        if cast_weights:
            wgb_ref[...] = wg_ref[...].astype(BF16)
            wub_ref[...] = wu_ref[...].astype(BF16)
            wdb_ref[...] = wd_ref[...].astype(BF16)
            wg, wu, wd = wgb_ref, wub_ref, wdb_ref
        cm = out_ref.shape[0] // chunks
        groups = [(x_ref, xn_ref, out_ref, slice(c * cm, (c + 1) * cm)) for c in range(chunks)]
        if extra_rows:
            groups.append((xe_ref, xne_ref, oute_ref, slice(None)))
        for xr, xnr, outr, rows in groups:
            if first and not normed_input:
                xn = _rms(xr[rows, :], norm_ref[...]).astype(BF16)
                xnr[rows, :] = xn
            else:
                xn = xnr[rows, :]
            gate = _dot(xn, wg[...])
            up = _dot(xn, wu[...])
            h = (gate * jax.nn.sigmoid(gate) * up * 0.5).astype(BF16)
            acc = (xr[rows, :] if first else outr[rows, :]) + _dot(h, wd[...])
            if last and has_final:
                acc = _rms(acc, gf_ref[...])
            outr[rows, :] = acc

    row_chunks = max(1, out_ref.shape[0] // FFN_EDGE_ROWS)
    first_chunks = 1 if normed_input else row_chunks
    last_chunks = row_chunks if has_final else 1
    if n_ff == 1:
        chunk_step(True, True, row_chunks)
        return
    active = (i > 0) if head_given else True
    pl.when(active & (j == 0))(lambda: chunk_step(True, False, first_chunks))
    if has_final:
        pl.when(active & (j > 0) & (j < n_ff - 1))(lambda: chunk_step(False, False, 1))
        pl.when(active & (j == n_ff - 1))(lambda: chunk_step(False, True, last_chunks))
    else:
        pl.when(active & (j > 0))(lambda: chunk_step(False, False, 1))


def _ffn(x, norm, wg, wu, wd, *, tm, tf, g_final=None, side_cast=None, head=None, n_rows=None,
         extra=None, name):
    m = n_rows or x.shape[0]
    assert extra is None or (m == tm and norm.shape[0] == 1)
    assert m % tm == 0 and D_FF % tf == 0 and FFN_TF % tf == 0
    n_ff = D_FF // tf
    normed_input = norm.shape[0] == m
    cast_weights = wg.dtype == F32
    assert not cast_weights or m == tm
    assert head is None or (head.shape == (tm, D_MODEL) and m > tm)
    side_src, side_col0 = side_cast if side_cast is not None else (None, 0)
    side_chunks = 0 if side_src is None else (side_src.shape[1] - side_col0) // CAST_TF
    assert side_chunks <= (m // tm) * n_ff and side_col0 % CAST_TF == 0
    row = lambda i, j: (i, 0)
    side_chunk = lambda i, j: jnp.minimum(i * n_ff + j, side_chunks - 1)
    side_in = pl.BlockSpec((D_MODEL, CAST_TF),
                           lambda i, j: (0, side_col0 // CAST_TF + side_chunk(i, j)))
    side_out = pl.BlockSpec((D_MODEL, CAST_TF), lambda i, j: (0, side_chunk(i, j)))
    wj = (lambda i, j: jnp.where(i == 0, 0, j)) if head is not None else (lambda i, j: j)
    per_ff = FFN_TF // tf
    wd_spec = pl.BlockSpec((tf, D_MODEL), lambda i, j: (wj(i, j), 0))
    w_f32_specs = [pl.BlockSpec((D_MODEL, tf), lambda i, j: (0, j))] * 2 + [wd_spec]
    w_bf16_specs = [pl.BlockSpec((None, D_MODEL, tf),
                                 lambda i, j: (wj(i, j) // per_ff, 0, wj(i, j) % per_ff))] * 2
    w_bf16_specs += [wd_spec]
    prefetch_x = m // tm > 1 and n_ff > 1
    in_specs = [pl.BlockSpec(memory_space=pl.ANY) if prefetch_x
                else pl.BlockSpec((tm, D_MODEL), row, pipeline_mode=pl.Buffered(1)),
                pl.BlockSpec((tm, D_MODEL), row) if normed_input else _resident((1, D_MODEL)),
                *(w_f32_specs if cast_weights else w_bf16_specs)]
    args = [x, norm, wg, wu, wd]
    if g_final is not None:
        in_specs.append(_resident((1, D_MODEL)))
        args.append(g_final)
    if side_chunks:
        in_specs.append(side_in)
        args.append(side_src)
    if head is not None:
        in_specs.append(pl.BlockSpec(memory_space=pl.ANY))
        args.append(head)
    if extra is not None:
        in_specs.append(_resident(extra.shape))
        args.append(extra)
    out_specs = [pl.BlockSpec((tm, D_MODEL), row)]
    out_shape = [jax.ShapeDtypeStruct((m, D_MODEL), F32)]
    if cast_weights:
        out_specs += w_bf16_specs
        out_shape += [jax.ShapeDtypeStruct((D_FF // FFN_TF, D_MODEL, FFN_TF), BF16)] * 2
        out_shape += [jax.ShapeDtypeStruct(wd.shape, BF16)]
    if side_chunks:
        out_specs.append(side_out)
        out_shape.append(jax.ShapeDtypeStruct((D_MODEL, side_chunks * CAST_TF), BF16))
    if extra is not None:
        out_specs.append(pl.BlockSpec(extra.shape, lambda i, j: (0, 0)))
        out_shape.append(jax.ShapeDtypeStruct(extra.shape, F32))
    scratch_shapes = [] if normed_input else [pltpu.VMEM((tm, D_MODEL), BF16)]
    if extra is not None:
        scratch_shapes += [pltpu.VMEM(extra.shape, BF16)]
    if prefetch_x:
        scratch_shapes += [pltpu.VMEM((tm, D_MODEL), F32), pltpu.SemaphoreType.DMA(())]
    if head is not None:
        scratch_shapes += [pltpu.SemaphoreType.DMA(())]
    ordered_rows = bool(side_chunks) or prefetch_x
    outs = pl.pallas_call(
        functools.partial(_ffn_kernel, n_ff=n_ff, normed_input=normed_input,
                          has_final=g_final is not None, cast_weights=cast_weights,
                          side_chunks=side_chunks, prefetch_x=prefetch_x,
                          head_given=head is not None, extra_rows=extra is not None),
        grid=(m // tm, n_ff),
        in_specs=in_specs,
        out_specs=out_specs,
        out_shape=out_shape,
        scratch_shapes=scratch_shapes,
        compiler_params=pltpu.CompilerParams(
            dimension_semantics=("arbitrary" if ordered_rows else "parallel", "arbitrary"),
            vmem_limit_bytes=VMEM_LIMIT),
        name=name,
    )(*args)
    return outs if len(outs) > 1 else outs[0]


def _merge_kernel(x_ref, a_ref, c_ref, woa_ref, woc_ref, g_ref, x2_ref, xn_ref):
    x = x_ref[...] + (_dot(a_ref[...].astype(BF16), woa_ref[...]) +
                      _dot(c_ref[...].astype(BF16), woc_ref[...]))
    x2_ref[...] = x
    xn_ref[...] = _rms(x, g_ref[...]).astype(BF16)


def _merge(x, a, c, woa, woc, g, *, tm, name):
    m = x.shape[0]
    assert m % tm == 0
    row = lambda i: (i, 0)
    return pl.pallas_call(
        _merge_kernel,
        grid=(m // tm,),
        in_specs=[pl.BlockSpec((tm, D_MODEL), row), pl.BlockSpec((tm, ATTN_DIM), row),
                  pl.BlockSpec((tm, CONV_DIM), row), _resident((ATTN_DIM, D_MODEL)),
                  _resident((CONV_DIM, D_MODEL)), _resident((1, D_MODEL))],
        out_specs=[pl.BlockSpec((tm, D_MODEL), row), pl.BlockSpec((tm, D_MODEL), row)],
        out_shape=[jax.ShapeDtypeStruct((m, D_MODEL), F32),
                   jax.ShapeDtypeStruct((m, D_MODEL), BF16)],
        compiler_params=pltpu.CompilerParams(
            dimension_semantics=("parallel",), vmem_limit_bytes=VMEM_LIMIT),
        name=name,
    )(x, a, c, woa, woc, g)


K0, V0, B0, C0, H0, R_END = 0, 256, 512, 1536, 2560, 3584


def _project_conv(x_ref, g_ref, wr_ref):
    xn = _rms(x_ref[...], g_ref[...]).astype(BF16)
    gate_b = _dot(xn, wr_ref[:, B0:C0])
    u = _dot(xn, wr_ref[:, C0:H0]) * _dot(xn, wr_ref[:, H0:R_END])
    return xn, gate_b, u


def _project_qkv(xn, wq_ref, wr_ref):
    q = _dot(xn, wq_ref[...]) * (HEAD_DIM ** -0.5)
    return q, _dot(xn, wr_ref[:, K0:V0]), _dot(xn, wr_ref[:, V0:B0])


def _proj_prompt_kernel(x_ref, g_ref, wq_ref, wr_ref, wc_ref, gc_ref, umeta_ref,
                        fg_ref, fu_ref, fd_ref,
                        q_ref, k_ref, v_ref, cn_ref, cst_ref, bg_ref, bu_ref, bd_ref,
                        us_ref, *, tm, tiles_per_seq):
    t = pl.program_id(0) % tiles_per_seq

    @pl.when(t == 0)
    def _():
        us_ref[0:8, :] = umeta_ref[...]

    @pl.when(t != 0)
    def _():
        us_ref[0:8, :] = us_ref[tm:tm + 8, :]

    hm = tm // PROJ_ROW_CHUNKS
    casts = ((fg_ref, bg_ref), (fu_ref, bu_ref), (fd_ref, bd_ref))
    for h in range(PROJ_ROW_CHUNKS):
        rows = slice(h * hm, (h + 1) * hm)
        for src, dst in casts[h::PROJ_ROW_CHUNKS]:
            dst[...] = src[...].astype(BF16)
        xn, gate_b, u = _project_conv(x_ref.at[rows], g_ref, wr_ref)
        us_ref[8 + h * hm:8 + (h + 1) * hm, :] = u
        y = (wc_ref[0:1, :] * us_ref[6 + h * hm:6 + (h + 1) * hm, :]
             + wc_ref[1:2, :] * us_ref[7 + h * hm:7 + (h + 1) * hm, :] + wc_ref[2:3, :] * u)
        cn_ref[rows, :] = _rms(gate_b * y, gc_ref[...]).astype(BF16)
        q, k, v = _project_qkv(xn, wq_ref, wr_ref)
        q_ref[rows, :] = q.astype(BF16)
        k_ref[rows, :] = k
        v_ref[rows, :] = v
    cst_ref[0] = us_ref[tm + 6:tm + 8, :]


def _proj_prompt(x, g, wq, wr, wc, gc, umeta, ffn_w, *, tm, seq):
    m = x.shape[0]
    tiles_per_seq = seq // tm
    n_steps = m // tm
    n_cast = D_FF // CAST_TF
    assert D_FF % CAST_TF == 0 and n_cast <= n_steps
    row = lambda i: (i, 0)
    chunk = lambda i: jnp.minimum(i, n_cast - 1)
    cast_cols = pl.BlockSpec((D_MODEL, CAST_TF), lambda i: (0, chunk(i)))
    cast_rows = pl.BlockSpec((CAST_TF, D_MODEL), lambda i: (chunk(i), 0))
    per_ff = FFN_TF // CAST_TF
    assert FFN_TF % CAST_TF == 0
    cast_cols_out = pl.BlockSpec((None, D_MODEL, CAST_TF),
                                 lambda i: (chunk(i) // per_ff, 0, chunk(i) % per_ff))
    chunked = jax.ShapeDtypeStruct((D_FF // FFN_TF, D_MODEL, FFN_TF), BF16)
    return pl.pallas_call(
        functools.partial(_proj_prompt_kernel, tm=tm, tiles_per_seq=tiles_per_seq),
        grid=(n_steps,),
        in_specs=[pl.BlockSpec((tm, D_MODEL), row), _resident((1, D_MODEL)),
                  _resident((D_MODEL, ATTN_DIM)), _resident((D_MODEL, R_END)),
                  _resident((3, CONV_DIM)), _resident((1, CONV_DIM)), _resident((8, CONV_DIM)),
                  cast_cols, cast_cols, cast_rows],
        out_specs=[pl.BlockSpec((tm, ATTN_DIM), row), pl.BlockSpec((tm, KV_DIM), row),
                   pl.BlockSpec((tm, KV_DIM), row), pl.BlockSpec((tm, CONV_DIM), row),
                   pl.BlockSpec((1, 2, CONV_DIM), lambda i: (i // tiles_per_seq, 0, 0)),
                   cast_cols_out, cast_cols_out, cast_rows],
        out_shape=[jax.ShapeDtypeStruct((m, ATTN_DIM), BF16),
                   jax.ShapeDtypeStruct((m, KV_DIM), F32),
                   jax.ShapeDtypeStruct((m, KV_DIM), F32),
                   jax.ShapeDtypeStruct((m, CONV_DIM), BF16),
                   jax.ShapeDtypeStruct((m // seq, 2, CONV_DIM), F32),
                   chunked, chunked, jax.ShapeDtypeStruct(ffn_w[2].shape, BF16)],
        scratch_shapes=[pltpu.VMEM((tm + 8, CONV_DIM), F32)],
        compiler_params=pltpu.CompilerParams(
            dimension_semantics=("arbitrary",), vmem_limit_bytes=VMEM_LIMIT),
        name="proj_prompt",
    )(x, g, wq, wr, wc, gc, umeta, *ffn_w)


def _proj_tail_kernel(x_ref, g_ref, wq_ref, wr_ref, wc_ref, gc_ref, h0_ref, h1_ref,
                      q_ref, k_ref, v_ref, u_ref, cn_ref):
    xn, gate_b, u = _project_conv(x_ref, g_ref, wr_ref)
    q, k, v = _project_qkv(xn, wq_ref, wr_ref)
    q_ref[...] = q
    k_ref[...] = k
    v_ref[...] = v
    u_ref[...] = u
    y = wc_ref[0:1, :] * h0_ref[...] + wc_ref[1:2, :] * h1_ref[...] + wc_ref[2:3, :] * u
    cn_ref[...] = _rms(gate_b * y, gc_ref[...]).astype(BF16)


def _proj_tail(x, g, wq, wr, wc, gc, h0, h1):
    m = x.shape[0]
    full = lambda shape: pl.BlockSpec(shape, lambda i: (0,) * len(shape))
    return pl.pallas_call(
        _proj_tail_kernel,
        grid=(1,),
        in_specs=[full((m, D_MODEL)), full((1, D_MODEL)), _resident((D_MODEL, ATTN_DIM)),
                  _resident((D_MODEL, R_END)),
                  full((3, CONV_DIM)), full((1, CONV_DIM)), full((m, CONV_DIM)),
                  full((m, CONV_DIM))],
        out_specs=[full((m, ATTN_DIM)), full((m, KV_DIM)), full((m, KV_DIM)),
                   full((m, CONV_DIM)), full((m, CONV_DIM))],
        out_shape=[jax.ShapeDtypeStruct((m, ATTN_DIM), F32),
                   jax.ShapeDtypeStruct((m, KV_DIM), F32),
                   jax.ShapeDtypeStruct((m, KV_DIM), F32),
                   jax.ShapeDtypeStruct((m, CONV_DIM), F32),
                   jax.ShapeDtypeStruct((m, CONV_DIM), BF16)],
        compiler_params=pltpu.CompilerParams(
            dimension_semantics=("arbitrary",), vmem_limit_bytes=VMEM_LIMIT),
        name="proj_tail",
    )(x, g, wq, wr, wc, gc, h0, h1)


class _BandedAttention:
    def __init__(self, first, sink_ref, q_ref, kprev, kcur, vprev, vcur, ga_ref, o_ref, nsub):
        self.sink_ref, self.q_ref, self.ga_ref, self.o_ref = sink_ref, q_ref, ga_ref, o_ref
        self.n_pairs = N_KV_HEADS // 2
        self.units = [(sb, pair) for sb in range(nsub) for pair in range(self.n_pairs)]
        kall = jnp.concatenate([kprev, kcur], axis=0)
        vall = jnp.concatenate([vprev, vcur], axis=0)
        c = lax.broadcasted_iota(jnp.int32, (2 * WINDOW, WINDOW), 0)
        r = lax.broadcasted_iota(jnp.int32, (2 * WINDOW, WINDOW), 1)
        self.band = (c >= r) & (c <= r + WINDOW)
        self.band_first = self.band & ((c >= WINDOW - N_META) | jnp.logical_not(first))
        low = lax.broadcasted_iota(jnp.int32, kall.shape[:1] + (LANES,), 1) < HALF
        self.k_half, self.vt_half = [], []
        for pair in range(self.n_pairs):
            kcol = kall[:, pair * LANES:(pair + 1) * LANES]
            vcol = vall[:, pair * LANES:(pair + 1) * LANES]
            self.k_half.append((jnp.where(low, kcol, 0.0).astype(BF16),
                                jnp.where(low, 0.0, kcol).astype(BF16)))
            self.vt_half.append((jnp.where(low, vcol, 0.0).T.astype(BF16),
                                 jnp.where(low, 0.0, vcol).T.astype(BF16)))
        self.outs = {}

    def scores(self, unit):
        sb, pair = unit
        rows = slice(sb * WINDOW, (sb + 1) * WINDOW)
        keys = slice(sb * WINDOW, (sb + 2) * WINDOW)
        kst = jnp.concatenate([self.k_half[pair][0][keys], self.k_half[pair][1][keys]], axis=0)
        qst = jnp.concatenate(
            [self.q_ref[rows, (grp * self.n_pairs + pair) * LANES:
                        (grp * self.n_pairs + pair + 1) * LANES]
             for grp in range(GQA_GROUP)], axis=0)
        return _dot_t(kst, qst)

    def softmax(self, unit, st_all):
        sb, pair = unit
        valid = self.band_first if sb == 0 else self.band
        pt_rows = []
        for half in range(2):
            pt_cols = []
            for grp in range(GQA_GROUP):
                sink = self.sink_ref[(2 * pair + half) * GQA_GROUP + grp]
                st = st_all[half * 2 * WINDOW:(half + 1) * 2 * WINDOW,
                            grp * WINDOW:(grp + 1) * WINDOW]
                st = jnp.where(valid, st, NEG)
                mx = jnp.maximum(jnp.max(st, axis=0, keepdims=True), sink)
                e = jnp.exp(st - mx)
                den = jnp.sum(e, axis=0, keepdims=True) + jnp.exp(sink - mx)
                pt_cols.append((e * (1.0 / den)).astype(BF16))
            pt_rows.append(jnp.concatenate(pt_cols, axis=1))
        return jnp.concatenate(pt_rows, axis=0)

    def values(self, unit, pt):
        sb, pair = unit
        keys = slice(sb * WINDOW, (sb + 2) * WINDOW)
        vst = jnp.concatenate([self.vt_half[pair][0][:, keys], self.vt_half[pair][1][:, keys]],
                              axis=1)
        ot_all = _dot(vst, pt)
        for grp in range(GQA_GROUP):
            self.outs[(sb, grp * self.n_pairs + pair)] = ot_all[:, grp * WINDOW:(grp + 1) * WINDOW]
        if pair == self.n_pairs - 1:
            self._normalise(sb)

    def _normalise(self, sb):
        rows = slice(sb * WINDOW, (sb + 1) * WINDOW)
        cols = range(ATTN_DIM // LANES)
        ssq = sum(jnp.sum(self.outs[(sb, col)] ** 2, axis=0, keepdims=True) for col in cols)
        inv = lax.rsqrt(ssq * (1.0 / ATTN_DIM) + EPS)
        for col in cols:
            sl = slice(col * LANES, (col + 1) * LANES)
            self.o_ref[rows, sl] = ((self.outs.pop((sb, col)) * inv).T
                                    * self.ga_ref[:, sl]).astype(BF16)


def _attn_merge_kernel(sink_ref, q_ref, kp_ref, kc_ref, vp_ref, vc_ref, km_ref, vm_ref, ga_ref,
                       x_ref, c_ref, woa_ref, woc_ref, g_ref, x2_ref, xn_ref,
                       an_new_ref, an_old_ref, *, nsub, tiles_per_seq, n_tiles):
    t = pl.program_id(0)

    @pl.when(t == 0)
    def _():
        an_new_ref[...] = jnp.zeros(an_new_ref.shape, an_new_ref.dtype)

    an_old_ref[...] = an_new_ref[...]

    first = (jnp.minimum(t, n_tiles - 1) % tiles_per_seq) == 0
    attn = _BandedAttention(first, sink_ref, q_ref,
                            jnp.where(first, km_ref[...], kp_ref[...]), kc_ref[...],
                            jnp.where(first, vm_ref[...], vp_ref[...]), vc_ref[...],
                            ga_ref, an_new_ref, nsub)
    units = attn.units
    n_chunks = len(units)
    cw = D_MODEL // n_chunks
    ssq = jnp.zeros((x_ref.shape[0], 1), F32)
    st_next = attn.scores(units[0])
    for u, unit in enumerate(units):
        cols = slice(u * cw, (u + 1) * cw)
        xc = x_ref[:, cols] + (_dot(an_old_ref[...], woa_ref[:, cols]) +
                               _dot(c_ref[...], woc_ref[:, cols]))
        x2_ref[:, cols] = xc
        ssq = ssq + jnp.sum(xc * xc, axis=-1, keepdims=True)
        st_all, st_next = st_next, (attn.scores(units[u + 1]) if u + 1 < n_chunks else None)
        attn.values(unit, attn.softmax(unit, st_all))
    inv = lax.rsqrt(ssq * (1.0 / D_MODEL) + EPS)
    for u in range(n_chunks):
        cols = slice(u * cw, (u + 1) * cw)
        xn_ref[:, cols] = (x2_ref[:, cols] * inv * g_ref[:, cols]).astype(BF16)


def _attn_merge(sinks, q, k, v, kmeta, vmeta, ga, x, c, woa, woc, g, *, seq, nsub):
    m = x.shape[0]
    tm = nsub * WINDOW
    n_tiles = m // tm
    tiles_per_seq = seq // tm
    cur = lambda t: (jnp.minimum(t, n_tiles - 1), 0)
    prev = lambda t: (jnp.maximum(jnp.minimum(t, n_tiles - 1) * nsub - 1, 0), 0)
    lag = lambda t: (jnp.maximum(t - 1, 0), 0)
    return pl.pallas_call(
        functools.partial(_attn_merge_kernel, nsub=nsub, tiles_per_seq=tiles_per_seq,
                          n_tiles=n_tiles),
        grid=(n_tiles + 1,),
        in_specs=[pl.BlockSpec(memory_space=pltpu.SMEM),
                  pl.BlockSpec((tm, ATTN_DIM), cur),
                  pl.BlockSpec((WINDOW, KV_DIM), prev), pl.BlockSpec((tm, KV_DIM), cur),
                  pl.BlockSpec((WINDOW, KV_DIM), prev), pl.BlockSpec((tm, KV_DIM), cur),
                  _resident((WINDOW, KV_DIM)), _resident((WINDOW, KV_DIM)),
                  _resident((1, ATTN_DIM)),
                  pl.BlockSpec((tm, D_MODEL), lag), pl.BlockSpec((tm, CONV_DIM), lag),
                  _resident((ATTN_DIM, D_MODEL)), _resident((CONV_DIM, D_MODEL)),
                  _resident((1, D_MODEL))],
        out_specs=[pl.BlockSpec((tm, D_MODEL), lag), pl.BlockSpec((tm, D_MODEL), lag)],
        out_shape=[jax.ShapeDtypeStruct((m, D_MODEL), F32),
                   jax.ShapeDtypeStruct((m, D_MODEL), BF16)],
        scratch_shapes=[pltpu.VMEM((tm, ATTN_DIM), BF16), pltpu.VMEM((tm, ATTN_DIM), BF16)],
        compiler_params=pltpu.CompilerParams(
            dimension_semantics=("arbitrary",), vmem_limit_bytes=VMEM_LIMIT),
        name="attn_merge_prompt",
    )(sinks, q, k, k, v, v, kmeta, vmeta, ga, x, c, woa, woc, g)


def _attn_sample_kernel(sink_ref, q_ref, kn_ref, vn_ref, ck_ref, cv_ref, ga_ref,
                        ok_ref, ov_ref, an_ref, *, bs):
    n_pairs = N_KV_HEADS // 2
    heads_per_col = 2 * GQA_GROUP
    last_key = lax.broadcasted_iota(jnp.int32, (KV_DIM, WINDOW), 1) == WINDOW - 1
    pad = jnp.zeros((WINDOW - bs, KV_DIM), F32)
    knt = jnp.concatenate([kn_ref[...], pad], axis=0).T
    vnt = jnp.concatenate([vn_ref[...], pad], axis=0).T
    for s in range(bs):
        ok_ref[s] = jnp.where(last_key, knt[:, s:s + 1], pltpu.roll(ck_ref[s], WINDOW - 1, 1))
        ov_ref[s] = jnp.where(last_key, vnt[:, s:s + 1], pltpu.roll(cv_ref[s], WINDOW - 1, 1))

    n_rows = heads_per_col * bs
    row = lax.broadcasted_iota(jnp.int32, (n_rows, bs * WINDOW), 0)
    lane = lax.broadcasted_iota(jnp.int32, (n_rows, bs * WINDOW), 1)
    log2 = lambda n: n.bit_length() - 1
    assert bs == 1 << log2(bs) and WINDOW == 1 << log2(WINDOW)
    same_seq = (row & (bs - 1)) == (lane >> log2(WINDOW))
    head_row = lax.broadcasted_iota(jnp.int32, (n_rows, 1), 0) >> log2(bs)
    low = lax.broadcasted_iota(jnp.int32, (bs, LANES), 1) < HALF

    cols = [None] * (ATTN_DIM // LANES)
    for pair in range(n_pairs):
        sl = slice(pair * LANES, (pair + 1) * LANES)
        pieces = []
        sk = jnp.zeros((n_rows, 1), F32)
        for grp in range(GQA_GROUP):
            col = grp * n_pairs + pair
            qcol = q_ref[:, col * LANES:(col + 1) * LANES]
            pieces += [jnp.where(low, qcol, 0.0), jnp.where(low, 0.0, qcol)]
            for half in range(2):
                sink = sink_ref[(2 * pair + half) * GQA_GROUP + grp]
                sk = jnp.where(head_row == 2 * grp + half, sink, sk)
        lq = jnp.concatenate(pieces, axis=0).astype(BF16)
        kstack = jnp.concatenate([ck_ref[s, sl, :] for s in range(bs)], axis=1).astype(BF16)
        vstack = jnp.concatenate([cv_ref[s, sl, :] for s in range(bs)], axis=1).astype(BF16)
        s = jnp.where(same_seq, _dot(lq, kstack), NEG)
        k_new = jnp.concatenate([kn_ref[:, sl]] * heads_per_col, axis=0).astype(BF16)
        v_new = jnp.concatenate([vn_ref[:, sl]] * heads_per_col, axis=0).astype(BF16)
        s_new = jnp.sum(lq.astype(F32) * k_new.astype(F32), axis=-1, keepdims=True)
        mx = jnp.maximum(jnp.maximum(jnp.max(s, axis=-1, keepdims=True), s_new), sk)
        e = jnp.exp(s - mx)
        e_new = jnp.exp(s_new - mx)
        rden = 1.0 / (jnp.sum(e, axis=-1, keepdims=True) + e_new + jnp.exp(sk - mx))
        p = (e * rden).astype(BF16)
        p_new = (e_new * rden).astype(BF16).astype(F32)
        o = _dot_t(p, vstack) + p_new * v_new.astype(F32)
        for grp in range(GQA_GROUP):
            lo = o[(2 * grp) * bs:(2 * grp + 1) * bs]
            hi = o[(2 * grp + 1) * bs:(2 * grp + 2) * bs]
            cols[grp * n_pairs + pair] = jnp.where(low, lo, hi)
    an_ref[...] = _rms(jnp.concatenate(cols, axis=1), ga_ref[...])


def _attn_sample(sinks, q, kn, vn, ck, cv, ga, *, bs):
    nseq = q.shape[0]
    row = lambda i: (i, 0)
    blk3 = pl.BlockSpec((bs, KV_DIM, WINDOW), lambda i: (i, 0, 0))
    return pl.pallas_call(
        functools.partial(_attn_sample_kernel, bs=bs),
        grid=(nseq // bs,),
        in_specs=[pl.BlockSpec(memory_space=pltpu.SMEM),
                  pl.BlockSpec((bs, ATTN_DIM), row), pl.BlockSpec((bs, KV_DIM), row),
                  pl.BlockSpec((bs, KV_DIM), row), blk3, blk3,
                  pl.BlockSpec((1, ATTN_DIM), lambda i: (0, 0))],
        out_specs=[blk3, blk3, pl.BlockSpec((bs, ATTN_DIM), row)],
        out_shape=[jax.ShapeDtypeStruct(ck.shape, F32), jax.ShapeDtypeStruct(cv.shape, F32),
                   jax.ShapeDtypeStruct((nseq, ATTN_DIM), F32)],
        compiler_params=pltpu.CompilerParams(
            dimension_semantics=("parallel",), vmem_limit_bytes=VMEM_LIMIT),
        name="attn_sample",
    )(sinks, q, kn, vn, ck, cv, ga)


def _prep_wq_kernel(w_ref, wq_ref):
    low = lax.broadcasted_iota(jnp.int32, (w_ref.shape[0], LANES), 1) < HALF
    for grp in range(GQA_GROUP):
        def piece(kv, want_low):
            col = kv * (GQA_GROUP // 2) + grp // 2
            blk = w_ref[:, col * LANES:(col + 1) * LANES]
            return blk if (grp % 2 == 0) == want_low else pltpu.roll(blk, HALF, 1)
        for pair in range(N_KV_HEADS // 2):
            col = grp * (N_KV_HEADS // 2) + pair
            wq_ref[:, col * LANES:(col + 1) * LANES] = jnp.where(
                low, piece(2 * pair, True), piece(2 * pair + 1, False)).astype(BF16)


def _prep_wq(w_in2d, *, tr):
    rows = w_in2d.shape[0]
    return pl.pallas_call(
        _prep_wq_kernel,
        grid=(rows // tr,),
        in_specs=[pl.BlockSpec((tr, ATTN_DIM), lambda i: (i, 0))],
        out_specs=pl.BlockSpec((tr, ATTN_DIM), lambda i: (i, 0)),
        out_shape=jax.ShapeDtypeStruct((rows, ATTN_DIM), BF16),
        compiler_params=pltpu.CompilerParams(
            dimension_semantics=("parallel",), vmem_limit_bytes=VMEM_LIMIT),
        name="prep_wq",
    )(w_in2d)


def _group_major(w, axis):
    shape = w.shape
    w = w.reshape(shape[:axis] + (N_KV_HEADS, GQA_GROUP, HEAD_DIM) + shape[axis + 1:])
    return jnp.swapaxes(w, axis, axis + 1).reshape(shape)


def kernel(x_prompt, x_sample, cache_swa_k, cache_swa_v, state_conv, meta_tokens, g_ffn1, w1_gate, w1_up, w1_down, g_mix, w_in, attn_sinks, w_conv, g_attn_out, g_conv_out, w_out, g_ffn2, w2_gate, w2_up, w2_down, g_final):
    assert g_ffn1.shape[0] == 1, "one layer"
    batch, seq, _ = x_prompt.shape
    n_dec = x_sample.shape[0]
    assert x_sample.shape[1] == 1 and cache_swa_k.shape[2] == WINDOW

    row2d = lambda a: a.reshape(1, -1).astype(F32)
    wq = _prep_wq(w_in[0], tr=PREP_ROWS)
    wo_a = _group_major(w_out[0, :ATTN_DIM], 0).astype(BF16)
    wo_c = w_out[0, ATTN_DIM:].astype(BF16)
    ga = row2d(_group_major(g_attn_out[0], 0))
    gc = row2d(g_conv_out[0])
    sinks = attn_sinks[0].astype(F32)
    wc = w_conv[0].astype(F32)

    xp = x_prompt.reshape(batch * seq, D_MODEL)
    xt = jnp.concatenate([x_sample[:, 0, :], meta_tokens.astype(x_prompt.dtype)], axis=0)
    n_tail = xt.shape[0]

    xh1, w1g, w1u, w1d, xt1 = _ffn(xp, row2d(g_ffn1[0]), w1_gate[0], w1_up[0], w1_down[0],
                                   tm=FFN_TM, tf=HEAD_TF, n_rows=FFN_TM, extra=xt,
                                   name="ffn1_head")
    xp1, wr = _ffn(xp, row2d(g_ffn1[0]), w1g, w1u, w1d, tm=FFN_TM, tf=FFN_TF,
                   side_cast=(w_in[0], ATTN_DIM), head=xh1, name="ffn1_prompt")

    st = state_conv[0]
    zpad = jnp.zeros((N_META, CONV_DIM), F32)
    h0 = jnp.concatenate([st[:, 0, :], zpad], axis=0)
    h1 = jnp.concatenate([st[:, 1, :], zpad], axis=0)
    qt, kt, vt, ut, cnt = _proj_tail(xt1, row2d(g_mix[0]), wq, wr, wc, gc, h0, h1)
    kmeta = jnp.pad(kt[n_dec:], ((WINDOW - N_META, 0), (0, 0)))
    vmeta = jnp.pad(vt[n_dec:], ((WINDOW - N_META, 0), (0, 0)))
    umeta = ut[n_tail - 8:]

    qp, kp, vp, cnp, cstate, w2g, w2u, w2d = _proj_prompt(
        xp1, row2d(g_mix[0]), wq, wr, wc, gc, umeta, (w2_gate[0], w2_up[0], w2_down[0]),
        tm=PROJ_TM, seq=seq)
    xp2, xpn = _attn_merge(sinks, qp, kp, vp, kmeta, vmeta, ga, xp1, cnp, wo_a, wo_c,
                           row2d(g_ffn2[0]), seq=seq, nsub=ATTN_NSUB)

    to_channel_major = lambda c: jnp.transpose(c[0], (0, 2, 3, 1)).reshape(n_dec, KV_DIM, WINDOW)
    from_channel_major = lambda c: jnp.transpose(
        c.reshape(n_dec, N_KV_HEADS, HEAD_DIM, WINDOW), (0, 3, 1, 2))[None]
    nk, nv, ans = _attn_sample(sinks, qt[:n_dec], kt[:n_dec], vt[:n_dec],
                               to_channel_major(cache_swa_k), to_channel_major(cache_swa_v), ga,
                               bs=SAMPLE_BS)

    xs2, xsn = _merge(xt1[:n_dec], ans, cnt[:n_dec], wo_a, wo_c, row2d(g_ffn2[0]), tm=n_dec,
                      name="merge_tail")
    ys = _ffn(xs2, xsn, w2g, w2u, w2d, tm=n_dec, tf=FFN_TF, g_final=row2d(g_final),
              name="ffn2_tail")
    yp = _ffn(xp2, xpn, w2g, w2u, w2d, tm=FFN_TM, tf=FFN_TF, g_final=row2d(g_final),
              name="ffn2_prompt")

    kv_shape = (1, batch, WINDOW, N_KV_HEADS, HEAD_DIM)
    new_k_prompt = kp.reshape(batch, seq, KV_DIM)[:, seq - WINDOW:].reshape(kv_shape)
    new_v_prompt = vp.reshape(batch, seq, KV_DIM)[:, seq - WINDOW:].reshape(kv_shape)
    new_conv_sample = jnp.stack([st[:, 1, :], ut[:n_dec]], axis=1)[None]
    return (yp.reshape(batch, seq, D_MODEL), ys.reshape(n_dec, 1, D_MODEL),
            new_k_prompt, new_v_prompt, cstate[None],
            from_channel_major(nk), from_channel_major(nv), new_conv_sample)
```

```python
import functools

import jax
import jax.numpy as jnp
from jax import lax
from jax.experimental import pallas as pl
from jax.experimental.pallas import tpu as pltpu

D_MODEL = 2048
D_FF = 5632
N_META = 16
ATTN_DIM = 1024
CONV_DIM = 1024
HEAD_DIM = 64
N_HEADS = 16
N_KV_HEADS = 4
GQA_GROUP = 4
KV_DIM = 256
WINDOW = 128
IN_DIM = ATTN_DIM + 2 * KV_DIM + 3 * CONV_DIM
EPS = 1e-5
NEG = -1e30

LANES = 128
HALF = LANES // 2
VMEM_LIMIT = 60 * 1024 * 1024
FFN_TM, FFN_TF = 1024, 512
HEAD_TF = 256
FFN_EDGE_ROWS = 256
PROJ_TM = 512
PROJ_ROW_CHUNKS = 2
CAST_TF = 256
ATTN_NSUB = 4
PREP_ROWS = 512
SAMPLE_BS = 16

F32 = jnp.float32
BF16 = jnp.bfloat16


def _rms(x, g):
    return x * lax.rsqrt(jnp.mean(x * x, axis=-1, keepdims=True) + EPS) * g


def _dot(a, b):
    return jnp.dot(a, b, preferred_element_type=F32)


def _dot_t(a, b):
    return lax.dot_general(a, b, (((1,), (1,)), ((), ())), preferred_element_type=F32)


def _resident(shape):
    return pl.BlockSpec(shape, lambda *_: (0,) * len(shape), pipeline_mode=pl.Buffered(1))


def _ffn_kernel(*refs, n_ff, normed_input, has_final, cast_weights, side_chunks, prefetch_x,
                head_given, extra_rows):
    refs = list(refs)
    x_ref, norm_ref, wg_ref, wu_ref, wd_ref = refs[:5]
    refs = refs[5:]
    if has_final:
        gf_ref = refs.pop(0)
    if side_chunks:
        side_src_ref = refs.pop(0)
    if head_given:
        head_hbm = refs.pop(0)
    if extra_rows:
        xe_ref = refs.pop(0)
        if normed_input:
            xne_ref = refs.pop(0)
    out_ref = refs.pop(0)
    if cast_weights:
        wgb_ref, wub_ref, wdb_ref = refs[:3]
        refs = refs[3:]
    if side_chunks:
        side_dst_ref = refs.pop(0)
    if extra_rows:
        oute_ref = refs.pop(0)
    xn_ref = norm_ref if normed_input else refs.pop(0)
    if extra_rows and not normed_input:
        xne_ref = refs.pop(0)
    i = pl.program_id(0)
    j = pl.program_id(1)

    if prefetch_x:
        x_hbm, x_ref, x_sem = x_ref, refs.pop(0), refs.pop(0)
        tm = x_ref.shape[0]

        def x_copy(tile):
            return pltpu.make_async_copy(x_hbm.at[pl.ds(tile * tm, tm), :], x_ref, x_sem)

        @pl.when(j == 0)
        def _():
            pl.when(i == 0)(lambda: x_copy(0).start())
            x_copy(i).wait()

        pl.when((j == 1) & (i + 1 < pl.num_programs(0)))(lambda: x_copy(i + 1).start())

    if head_given:
        head_sem = refs.pop(0)

        @pl.when((i == 0) & (j == 0))
        def _():
            head_copy = pltpu.make_async_copy(head_hbm, out_ref, head_sem)
            head_copy.start()
            head_copy.wait()

    if side_chunks:
        @pl.when(pl.program_id(0) * n_ff + j < side_chunks)
        def _():
            side_dst_ref[...] = side_src_ref[...].astype(BF16)

    def chunk_step(first, last, chunks):
        wg, wu, wd = wg_ref, wu_ref, wd_ref
        if cast_weights:
            wgb_ref[...] = wg_ref[...].astype(BF16)
            wub_ref[...] = wu_ref[...].astype(BF16)
            wdb_ref[...] = wd_ref[...].astype(BF16)
            wg, wu, wd = wgb_ref, wub_ref, wdb_ref
        cm = out_ref.shape[0] // chunks

        def row_group(xr, xnr, outr, rows):
            if first and not normed_input:
                xn = _rms(xr[rows, :], norm_ref[...]).astype(BF16)
                xnr[rows, :] = xn
            else:
                xn = xnr[rows, :]
            gate = _dot(xn, wg[...])
            up = _dot(xn, wu[...])
            h = (gate * jax.nn.sigmoid(gate) * up * 0.5).astype(BF16)
            acc = (xr[rows, :] if first else outr[rows, :]) + _dot(h, wd[...])
            if last and has_final:
                acc = _rms(acc, gf_ref[...])
            outr[rows, :] = acc

        for c in range(chunks):
            row_group(x_ref, xn_ref, out_ref, slice(c * cm, (c + 1) * cm))
        if extra_rows:
            extra_group = lambda: row_group(xe_ref, xne_ref, oute_ref, slice(None))
            pl.when(i == 0)(extra_group) if extra_rows == "tile0" else extra_group()

    row_chunks = max(1, out_ref.shape[0] // FFN_EDGE_ROWS)
    first_chunks = 1 if normed_input else row_chunks
    last_chunks = row_chunks if has_final else 1
    if n_ff == 1:
        chunk_step(True, True, row_chunks)
        return
    active = (i > 0) if head_given else True
    pl.when(active & (j == 0))(lambda: chunk_step(True, False, first_chunks))
    if has_final:
        pl.when(active & (j > 0) & (j < n_ff - 1))(lambda: chunk_step(False, False, 1))
        pl.when(active & (j == n_ff - 1))(lambda: chunk_step(False, True, last_chunks))
    else:
        pl.when(active & (j > 0))(lambda: chunk_step(False, False, 1))


def _ffn(x, norm, wg, wu, wd, *, tm, tf, g_final=None, side_cast=None, head=None, n_rows=None,
         extra=None, name):
    m = n_rows or x.shape[0]
    xe, xne = extra if extra is not None else (None, None)
    assert extra is None or ((xne is not None) == (norm.shape[0] == m) and head is None)
    assert m % tm == 0 and D_FF % tf == 0 and FFN_TF % tf == 0
    n_ff = D_FF // tf
    normed_input = norm.shape[0] == m
    cast_weights = wg.dtype == F32
    assert not cast_weights or m == tm
    assert head is None or (head.shape == (tm, D_MODEL) and m > tm)
    side_src, side_col0 = side_cast if side_cast is not None else (None, 0)
    side_chunks = 0 if side_src is None else (side_src.shape[1] - side_col0) // CAST_TF
    assert side_chunks <= (m // tm) * n_ff and side_col0 % CAST_TF == 0
    row = lambda i, j: (i, 0)
    side_chunk = lambda i, j: jnp.minimum(i * n_ff + j, side_chunks - 1)
    side_in = pl.BlockSpec((D_MODEL, CAST_TF),
                           lambda i, j: (0, side_col0 // CAST_TF + side_chunk(i, j)))
    side_out = pl.BlockSpec((D_MODEL, CAST_TF), lambda i, j: (0, side_chunk(i, j)))
    wj = (lambda i, j: jnp.where(i == 0, 0, j)) if head is not None else (lambda i, j: j)
    per_ff = FFN_TF // tf
    wd_spec = pl.BlockSpec((tf, D_MODEL), lambda i, j: (wj(i, j), 0))
    w_f32_specs = [pl.BlockSpec((D_MODEL, tf), lambda i, j: (0, j))] * 2 + [wd_spec]
    w_bf16_specs = [pl.BlockSpec((None, D_MODEL, tf),
                                 lambda i, j: (wj(i, j) // per_ff, 0, wj(i, j) % per_ff))] * 2
    w_bf16_specs += [wd_spec]
    prefetch_x = m // tm > 1 and n_ff > 1
    in_specs = [pl.BlockSpec(memory_space=pl.ANY) if prefetch_x
                else pl.BlockSpec((tm, D_MODEL), row, pipeline_mode=pl.Buffered(1)),
                pl.BlockSpec((tm, D_MODEL), row) if normed_input else _resident((1, D_MODEL)),
                *(w_f32_specs if cast_weights else w_bf16_specs)]
    args = [x, norm, wg, wu, wd]
    if g_final is not None:
        in_specs.append(_resident((1, D_MODEL)))
        args.append(g_final)
    if side_chunks:
        in_specs.append(side_in)
        args.append(side_src)
    if head is not None:
        in_specs.append(pl.BlockSpec(memory_space=pl.ANY))
        args.append(head)
    if extra is not None:
        in_specs.append(_resident(xe.shape))
        args.append(xe)
        if xne is not None:
            in_specs.append(_resident(xne.shape))
            args.append(xne)
    out_specs = [pl.BlockSpec((tm, D_MODEL), row)]
    out_shape = [jax.ShapeDtypeStruct((m, D_MODEL), F32)]
    if cast_weights:
        out_specs += w_bf16_specs
        out_shape += [jax.ShapeDtypeStruct((D_FF // FFN_TF, D_MODEL, FFN_TF), BF16)] * 2
        out_shape += [jax.ShapeDtypeStruct(wd.shape, BF16)]
    if side_chunks:
        out_specs.append(side_out)
        out_shape.append(jax.ShapeDtypeStruct((D_MODEL, side_chunks * CAST_TF), BF16))
    if extra is not None:
        out_specs.append(pl.BlockSpec(xe.shape, lambda i, j: (0, 0)))
        out_shape.append(jax.ShapeDtypeStruct(xe.shape, F32))
    scratch_shapes = [] if normed_input else [pltpu.VMEM((tm, D_MODEL), BF16)]
    if extra is not None and xne is None:
        scratch_shapes += [pltpu.VMEM(xe.shape, BF16)]
    if prefetch_x:
        scratch_shapes += [pltpu.VMEM((tm, D_MODEL), F32), pltpu.SemaphoreType.DMA(())]
    if head is not None:
        scratch_shapes += [pltpu.SemaphoreType.DMA(())]
    ordered_rows = bool(side_chunks) or prefetch_x
    outs = pl.pallas_call(
        functools.partial(_ffn_kernel, n_ff=n_ff, normed_input=normed_input,
                          has_final=g_final is not None, cast_weights=cast_weights,
                          side_chunks=side_chunks, prefetch_x=prefetch_x,
                          head_given=head is not None,
                          extra_rows=(None if extra is None else "tile0" if m > tm else "all")),
        grid=(m // tm, n_ff),
        in_specs=in_specs,
        out_specs=out_specs,
        out_shape=out_shape,
        scratch_shapes=scratch_shapes,
        compiler_params=pltpu.CompilerParams(
            dimension_semantics=("arbitrary" if ordered_rows else "parallel", "arbitrary"),
            vmem_limit_bytes=VMEM_LIMIT),
        name=name,
    )(*args)
    return outs if len(outs) > 1 else outs[0]


def _merge_kernel(x_ref, a_ref, c_ref, woa_ref, woc_ref, g_ref, x2_ref, xn_ref):
    x = x_ref[...] + (_dot(a_ref[...].astype(BF16), woa_ref[...]) +
                      _dot(c_ref[...].astype(BF16), woc_ref[...]))
    x2_ref[...] = x
    xn_ref[...] = _rms(x, g_ref[...]).astype(BF16)


def _merge(x, a, c, woa, woc, g, *, tm, name):
    m = x.shape[0]
    assert m % tm == 0
    row = lambda i: (i, 0)
    return pl.pallas_call(
        _merge_kernel,
        grid=(m // tm,),
        in_specs=[pl.BlockSpec((tm, D_MODEL), row), pl.BlockSpec((tm, ATTN_DIM), row),
                  pl.BlockSpec((tm, CONV_DIM), row), _resident((ATTN_DIM, D_MODEL)),
                  _resident((CONV_DIM, D_MODEL)), _resident((1, D_MODEL))],
        out_specs=[pl.BlockSpec((tm, D_MODEL), row), pl.BlockSpec((tm, D_MODEL), row)],
        out_shape=[jax.ShapeDtypeStruct((m, D_MODEL), F32),
                   jax.ShapeDtypeStruct((m, D_MODEL), BF16)],
        compiler_params=pltpu.CompilerParams(
            dimension_semantics=("parallel",), vmem_limit_bytes=VMEM_LIMIT),
        name=name,
    )(x, a, c, woa, woc, g)


K0, V0, B0, C0, H0, R_END = 0, 256, 512, 1536, 2560, 3584


def _project_conv(x_ref, g_ref, wr_ref):
    xn = _rms(x_ref[...], g_ref[...]).astype(BF16)
    gate_b = _dot(xn, wr_ref[:, B0:C0])
    u = _dot(xn, wr_ref[:, C0:H0]) * _dot(xn, wr_ref[:, H0:R_END])
    return xn, gate_b, u


def _project_qkv(xn, wq_ref, wr_ref):
    q = _dot(xn, wq_ref[...]) * (HEAD_DIM ** -0.5)
    return q, _dot(xn, wr_ref[:, K0:V0]), _dot(xn, wr_ref[:, V0:B0])


def _proj_prompt_kernel(x_ref, g_ref, wq_ref, wr_ref, wc_ref, gc_ref, umeta_ref,
                        fg_ref, fu_ref, fd_ref,
                        q_ref, k_ref, v_ref, cn_ref, cst_ref, bg_ref, bu_ref, bd_ref,
                        us_ref, *, tm, tiles_per_seq):
    t = pl.program_id(0) % tiles_per_seq

    @pl.when(t == 0)
    def _():
        us_ref[0:8, :] = umeta_ref[...]

    @pl.when(t != 0)
    def _():
        us_ref[0:8, :] = us_ref[tm:tm + 8, :]

    hm = tm // PROJ_ROW_CHUNKS
    casts = ((fg_ref, bg_ref), (fu_ref, bu_ref), (fd_ref, bd_ref))
    for h in range(PROJ_ROW_CHUNKS):
        rows = slice(h * hm, (h + 1) * hm)
        for src, dst in casts[h::PROJ_ROW_CHUNKS]:
            dst[...] = src[...].astype(BF16)
        xn, gate_b, u = _project_conv(x_ref.at[rows], g_ref, wr_ref)
        us_ref[8 + h * hm:8 + (h + 1) * hm, :] = u
        y = (wc_ref[0:1, :] * us_ref[6 + h * hm:6 + (h + 1) * hm, :]
             + wc_ref[1:2, :] * us_ref[7 + h * hm:7 + (h + 1) * hm, :] + wc_ref[2:3, :] * u)
        cn_ref[rows, :] = _rms(gate_b * y, gc_ref[...]).astype(BF16)
        q, k, v = _project_qkv(xn, wq_ref, wr_ref)
        q_ref[rows, :] = q.astype(BF16)
        k_ref[rows, :] = k
        v_ref[rows, :] = v
    cst_ref[0] = us_ref[tm + 6:tm + 8, :]


def _proj_prompt(x, g, wq, wr, wc, gc, umeta, ffn_w, *, tm, seq):
    m = x.shape[0]
    tiles_per_seq = seq // tm
    n_steps = m // tm
    n_cast = D_FF // CAST_TF
    assert D_FF % CAST_TF == 0 and n_cast <= n_steps
    row = lambda i: (i, 0)
    chunk = lambda i: jnp.minimum(i, n_cast - 1)
    cast_cols = pl.BlockSpec((D_MODEL, CAST_TF), lambda i: (0, chunk(i)))
    cast_rows = pl.BlockSpec((CAST_TF, D_MODEL), lambda i: (chunk(i), 0))
    per_ff = FFN_TF // CAST_TF
    assert FFN_TF % CAST_TF == 0
    cast_cols_out = pl.BlockSpec((None, D_MODEL, CAST_TF),
                                 lambda i: (chunk(i) // per_ff, 0, chunk(i) % per_ff))
    chunked = jax.ShapeDtypeStruct((D_FF // FFN_TF, D_MODEL, FFN_TF), BF16)
    return pl.pallas_call(
        functools.partial(_proj_prompt_kernel, tm=tm, tiles_per_seq=tiles_per_seq),
        grid=(n_steps,),
        in_specs=[pl.BlockSpec((tm, D_MODEL), row), _resident((1, D_MODEL)),
                  _resident((D_MODEL, ATTN_DIM)), _resident((D_MODEL, R_END)),
                  _resident((3, CONV_DIM)), _resident((1, CONV_DIM)), _resident((8, CONV_DIM)),
                  cast_cols, cast_cols, cast_rows],
        out_specs=[pl.BlockSpec((tm, ATTN_DIM), row), pl.BlockSpec((tm, KV_DIM), row),
                   pl.BlockSpec((tm, KV_DIM), row), pl.BlockSpec((tm, CONV_DIM), row),
                   pl.BlockSpec((1, 2, CONV_DIM), lambda i: (i // tiles_per_seq, 0, 0)),
                   cast_cols_out, cast_cols_out, cast_rows],
        out_shape=[jax.ShapeDtypeStruct((m, ATTN_DIM), BF16),
                   jax.ShapeDtypeStruct((m, KV_DIM), F32),
                   jax.ShapeDtypeStruct((m, KV_DIM), F32),
                   jax.ShapeDtypeStruct((m, CONV_DIM), BF16),
                   jax.ShapeDtypeStruct((m // seq, 2, CONV_DIM), F32),
                   chunked, chunked, jax.ShapeDtypeStruct(ffn_w[2].shape, BF16)],
        scratch_shapes=[pltpu.VMEM((tm + 8, CONV_DIM), F32)],
        compiler_params=pltpu.CompilerParams(
            dimension_semantics=("arbitrary",), vmem_limit_bytes=VMEM_LIMIT),
        name="proj_prompt",
    )(x, g, wq, wr, wc, gc, umeta, *ffn_w)


def _proj_tail_kernel(x_ref, g_ref, wq_ref, wr_ref, wc_ref, gc_ref, h0_ref, h1_ref,
                      q_ref, k_ref, v_ref, u_ref, cn_ref):
    xn, gate_b, u = _project_conv(x_ref, g_ref, wr_ref)
    q, k, v = _project_qkv(xn, wq_ref, wr_ref)
    q_ref[...] = q
    k_ref[...] = k
    v_ref[...] = v
    u_ref[...] = u
    y = wc_ref[0:1, :] * h0_ref[...] + wc_ref[1:2, :] * h1_ref[...] + wc_ref[2:3, :] * u
    cn_ref[...] = _rms(gate_b * y, gc_ref[...]).astype(BF16)


def _proj_tail(x, g, wq, wr, wc, gc, h0, h1):
    m = x.shape[0]
    full = lambda shape: pl.BlockSpec(shape, lambda i: (0,) * len(shape))
    return pl.pallas_call(
        _proj_tail_kernel,
        grid=(1,),
        in_specs=[full((m, D_MODEL)), full((1, D_MODEL)), _resident((D_MODEL, ATTN_DIM)),
                  _resident((D_MODEL, R_END)),
                  full((3, CONV_DIM)), full((1, CONV_DIM)), full((m, CONV_DIM)),
                  full((m, CONV_DIM))],
        out_specs=[full((m, ATTN_DIM)), full((m, KV_DIM)), full((m, KV_DIM)),
                   full((m, CONV_DIM)), full((m, CONV_DIM))],
        out_shape=[jax.ShapeDtypeStruct((m, ATTN_DIM), F32),
                   jax.ShapeDtypeStruct((m, KV_DIM), F32),
                   jax.ShapeDtypeStruct((m, KV_DIM), F32),
                   jax.ShapeDtypeStruct((m, CONV_DIM), F32),
                   jax.ShapeDtypeStruct((m, CONV_DIM), BF16)],
        compiler_params=pltpu.CompilerParams(
            dimension_semantics=("arbitrary",), vmem_limit_bytes=VMEM_LIMIT),
        name="proj_tail",
    )(x, g, wq, wr, wc, gc, h0, h1)


class _BandedAttention:
    def __init__(self, first, sink_ref, q_ref, kprev, kcur, vprev, vcur, ga_ref, o_ref, nsub):
        self.sink_ref, self.q_ref, self.ga_ref, self.o_ref = sink_ref, q_ref, ga_ref, o_ref
        self.n_pairs = N_KV_HEADS // 2
        self.units = [(sb, pair) for sb in range(nsub) for pair in range(self.n_pairs)]
        kall = jnp.concatenate([kprev, kcur], axis=0)
        vall = jnp.concatenate([vprev, vcur], axis=0)
        c = lax.broadcasted_iota(jnp.int32, (2 * WINDOW, WINDOW), 0)
        r = lax.broadcasted_iota(jnp.int32, (2 * WINDOW, WINDOW), 1)
        self.band = (c >= r) & (c <= r + WINDOW)
        self.band_first = self.band & ((c >= WINDOW - N_META) | jnp.logical_not(first))
        low = lax.broadcasted_iota(jnp.int32, kall.shape[:1] + (LANES,), 1) < HALF
        self.k_half, self.vt_half = [], []
        for pair in range(self.n_pairs):
            kcol = kall[:, pair * LANES:(pair + 1) * LANES]
            vcol = vall[:, pair * LANES:(pair + 1) * LANES]
            self.k_half.append((jnp.where(low, kcol, 0.0).astype(BF16),
                                jnp.where(low, 0.0, kcol).astype(BF16)))
            self.vt_half.append((jnp.where(low, vcol, 0.0).T.astype(BF16),
                                 jnp.where(low, 0.0, vcol).T.astype(BF16)))
        self.outs = {}

    def scores(self, unit):
        sb, pair = unit
        rows = slice(sb * WINDOW, (sb + 1) * WINDOW)
        keys = slice(sb * WINDOW, (sb + 2) * WINDOW)
        kst = jnp.concatenate([self.k_half[pair][0][keys], self.k_half[pair][1][keys]], axis=0)
        qst = jnp.concatenate(
            [self.q_ref[rows, (grp * self.n_pairs + pair) * LANES:
                        (grp * self.n_pairs + pair + 1) * LANES]
             for grp in range(GQA_GROUP)], axis=0)
        return _dot_t(kst, qst)

    def softmax(self, unit, st_all):
        sb, pair = unit
        valid = self.band_first if sb == 0 else self.band
        pt_rows = []
        for half in range(2):
            pt_cols = []
            for grp in range(GQA_GROUP):
                sink = self.sink_ref[(2 * pair + half) * GQA_GROUP + grp]
                st = st_all[half * 2 * WINDOW:(half + 1) * 2 * WINDOW,
                            grp * WINDOW:(grp + 1) * WINDOW]
                st = jnp.where(valid, st, NEG)
                mx = jnp.maximum(jnp.max(st, axis=0, keepdims=True), sink)
                e = jnp.exp(st - mx)
                den = jnp.sum(e, axis=0, keepdims=True) + jnp.exp(sink - mx)
                pt_cols.append((e * (1.0 / den)).astype(BF16))
            pt_rows.append(jnp.concatenate(pt_cols, axis=1))
        return jnp.concatenate(pt_rows, axis=0)

    def values(self, unit, pt):
        sb, pair = unit
        keys = slice(sb * WINDOW, (sb + 2) * WINDOW)
        vst = jnp.concatenate([self.vt_half[pair][0][:, keys], self.vt_half[pair][1][:, keys]],
                              axis=1)
        ot_all = _dot(vst, pt)
        for grp in range(GQA_GROUP):
            self.outs[(sb, grp * self.n_pairs + pair)] = ot_all[:, grp * WINDOW:(grp + 1) * WINDOW]
        if pair == self.n_pairs - 1:
            self._normalise(sb)

    def _normalise(self, sb):
        rows = slice(sb * WINDOW, (sb + 1) * WINDOW)
        cols = range(ATTN_DIM // LANES)
        ssq = sum(jnp.sum(self.outs[(sb, col)] ** 2, axis=0, keepdims=True) for col in cols)
        inv = lax.rsqrt(ssq * (1.0 / ATTN_DIM) + EPS)
        for col in cols:
            sl = slice(col * LANES, (col + 1) * LANES)
            self.o_ref[rows, sl] = ((self.outs.pop((sb, col)) * inv).T
                                    * self.ga_ref[:, sl]).astype(BF16)


def _attn_merge_kernel(sink_ref, q_ref, kp_ref, kc_ref, vp_ref, vc_ref, km_ref, vm_ref, ga_ref,
                       x_ref, c_ref, woa_ref, woc_ref, g_ref, x2_ref, xn_ref,
                       an_new_ref, an_old_ref, *, nsub, tiles_per_seq, n_tiles):
    t = pl.program_id(0)

    @pl.when(t == 0)
    def _():
        an_new_ref[...] = jnp.zeros(an_new_ref.shape, an_new_ref.dtype)

    an_old_ref[...] = an_new_ref[...]

    first = (jnp.minimum(t, n_tiles - 1) % tiles_per_seq) == 0
    attn = _BandedAttention(first, sink_ref, q_ref,
                            jnp.where(first, km_ref[...], kp_ref[...]), kc_ref[...],
                            jnp.where(first, vm_ref[...], vp_ref[...]), vc_ref[...],
                            ga_ref, an_new_ref, nsub)
    units = attn.units
    n_chunks = len(units)
    cw = D_MODEL // n_chunks
    ssq = jnp.zeros((x_ref.shape[0], 1), F32)
    st_next = attn.scores(units[0])
    for u, unit in enumerate(units):
        cols = slice(u * cw, (u + 1) * cw)
        xc = x_ref[:, cols] + (_dot(an_old_ref[...], woa_ref[:, cols]) +
                               _dot(c_ref[...], woc_ref[:, cols]))
        x2_ref[:, cols] = xc
        ssq = ssq + jnp.sum(xc * xc, axis=-1, keepdims=True)
        st_all, st_next = st_next, (attn.scores(units[u + 1]) if u + 1 < n_chunks else None)
        attn.values(unit, attn.softmax(unit, st_all))
    inv = lax.rsqrt(ssq * (1.0 / D_MODEL) + EPS)
    for u in range(n_chunks):
        cols = slice(u * cw, (u + 1) * cw)
        xn_ref[:, cols] = (x2_ref[:, cols] * inv * g_ref[:, cols]).astype(BF16)


def _attn_merge(sinks, q, k, v, kmeta, vmeta, ga, x, c, woa, woc, g, *, seq, nsub):
    m = x.shape[0]
    tm = nsub * WINDOW
    n_tiles = m // tm
    tiles_per_seq = seq // tm
    cur = lambda t: (jnp.minimum(t, n_tiles - 1), 0)
    prev = lambda t: (jnp.maximum(jnp.minimum(t, n_tiles - 1) * nsub - 1, 0), 0)
    lag = lambda t: (jnp.maximum(t - 1, 0), 0)
    return pl.pallas_call(
        functools.partial(_attn_merge_kernel, nsub=nsub, tiles_per_seq=tiles_per_seq,
                          n_tiles=n_tiles),
        grid=(n_tiles + 1,),
        in_specs=[pl.BlockSpec(memory_space=pltpu.SMEM),
                  pl.BlockSpec((tm, ATTN_DIM), cur),
                  pl.BlockSpec((WINDOW, KV_DIM), prev), pl.BlockSpec((tm, KV_DIM), cur),
                  pl.BlockSpec((WINDOW, KV_DIM), prev), pl.BlockSpec((tm, KV_DIM), cur),
                  _resident((WINDOW, KV_DIM)), _resident((WINDOW, KV_DIM)),
                  _resident((1, ATTN_DIM)),
                  pl.BlockSpec((tm, D_MODEL), lag), pl.BlockSpec((tm, CONV_DIM), lag),
                  _resident((ATTN_DIM, D_MODEL)), _resident((CONV_DIM, D_MODEL)),
                  _resident((1, D_MODEL))],
        out_specs=[pl.BlockSpec((tm, D_MODEL), lag), pl.BlockSpec((tm, D_MODEL), lag)],
        out_shape=[jax.ShapeDtypeStruct((m, D_MODEL), F32),
                   jax.ShapeDtypeStruct((m, D_MODEL), BF16)],
        scratch_shapes=[pltpu.VMEM((tm, ATTN_DIM), BF16), pltpu.VMEM((tm, ATTN_DIM), BF16)],
        compiler_params=pltpu.CompilerParams(
            dimension_semantics=("arbitrary",), vmem_limit_bytes=VMEM_LIMIT),
        name="attn_merge_prompt",
    )(sinks, q, k, k, v, v, kmeta, vmeta, ga, x, c, woa, woc, g)


def _attn_sample_kernel(sink_ref, q_ref, kn_ref, vn_ref, ck_ref, cv_ref, ga_ref,
                        ok_ref, ov_ref, an_ref, *, bs):
    n_pairs = N_KV_HEADS // 2
    heads_per_col = 2 * GQA_GROUP
    last_key = lax.broadcasted_iota(jnp.int32, (KV_DIM, WINDOW), 1) == WINDOW - 1
    pad = jnp.zeros((WINDOW - bs, KV_DIM), F32)
    knt = jnp.concatenate([kn_ref[...], pad], axis=0).T
    vnt = jnp.concatenate([vn_ref[...], pad], axis=0).T
    for s in range(bs):
        ok_ref[s] = jnp.where(last_key, knt[:, s:s + 1], pltpu.roll(ck_ref[s], WINDOW - 1, 1))
        ov_ref[s] = jnp.where(last_key, vnt[:, s:s + 1], pltpu.roll(cv_ref[s], WINDOW - 1, 1))

    n_rows = heads_per_col * bs
    row = lax.broadcasted_iota(jnp.int32, (n_rows, bs * WINDOW), 0)
    lane = lax.broadcasted_iota(jnp.int32, (n_rows, bs * WINDOW), 1)
    log2 = lambda n: n.bit_length() - 1
    assert bs == 1 << log2(bs) and WINDOW == 1 << log2(WINDOW)
    same_seq = (row & (bs - 1)) == (lane >> log2(WINDOW))
    head_row = lax.broadcasted_iota(jnp.int32, (n_rows, 1), 0) >> log2(bs)
    low = lax.broadcasted_iota(jnp.int32, (bs, LANES), 1) < HALF

    cols = [None] * (ATTN_DIM // LANES)
    for pair in range(n_pairs):
        sl = slice(pair * LANES, (pair + 1) * LANES)
        pieces = []
        sk = jnp.zeros((n_rows, 1), F32)
        for grp in range(GQA_GROUP):
            col = grp * n_pairs + pair
            qcol = q_ref[:, col * LANES:(col + 1) * LANES]
            pieces += [jnp.where(low, qcol, 0.0), jnp.where(low, 0.0, qcol)]
            for half in range(2):
                sink = sink_ref[(2 * pair + half) * GQA_GROUP + grp]
                sk = jnp.where(head_row == 2 * grp + half, sink, sk)
        lq = jnp.concatenate(pieces, axis=0).astype(BF16)
        kstack = jnp.concatenate([ck_ref[s, sl, :] for s in range(bs)], axis=1).astype(BF16)
        vstack = jnp.concatenate([cv_ref[s, sl, :] for s in range(bs)], axis=1).astype(BF16)
        s = jnp.where(same_seq, _dot(lq, kstack), NEG)
        k_new = jnp.concatenate([kn_ref[:, sl]] * heads_per_col, axis=0).astype(BF16)
        v_new = jnp.concatenate([vn_ref[:, sl]] * heads_per_col, axis=0).astype(BF16)
        s_new = jnp.sum(lq.astype(F32) * k_new.astype(F32), axis=-1, keepdims=True)
        mx = jnp.maximum(jnp.maximum(jnp.max(s, axis=-1, keepdims=True), s_new), sk)
        e = jnp.exp(s - mx)
        e_new = jnp.exp(s_new - mx)
        rden = 1.0 / (jnp.sum(e, axis=-1, keepdims=True) + e_new + jnp.exp(sk - mx))
        p = (e * rden).astype(BF16)
        p_new = (e_new * rden).astype(BF16).astype(F32)
        o = _dot_t(p, vstack) + p_new * v_new.astype(F32)
        for grp in range(GQA_GROUP):
            lo = o[(2 * grp) * bs:(2 * grp + 1) * bs]
            hi = o[(2 * grp + 1) * bs:(2 * grp + 2) * bs]
            cols[grp * n_pairs + pair] = jnp.where(low, lo, hi)
    an_ref[...] = _rms(jnp.concatenate(cols, axis=1), ga_ref[...])


def _attn_sample(sinks, q, kn, vn, ck, cv, ga, *, bs):
    nseq = q.shape[0]
    row = lambda i: (i, 0)
    blk3 = pl.BlockSpec((bs, KV_DIM, WINDOW), lambda i: (i, 0, 0))
    return pl.pallas_call(
        functools.partial(_attn_sample_kernel, bs=bs),
        grid=(nseq // bs,),
        in_specs=[pl.BlockSpec(memory_space=pltpu.SMEM),
                  pl.BlockSpec((bs, ATTN_DIM), row), pl.BlockSpec((bs, KV_DIM), row),
                  pl.BlockSpec((bs, KV_DIM), row), blk3, blk3,
                  pl.BlockSpec((1, ATTN_DIM), lambda i: (0, 0))],
        out_specs=[blk3, blk3, pl.BlockSpec((bs, ATTN_DIM), row)],
        out_shape=[jax.ShapeDtypeStruct(ck.shape, F32), jax.ShapeDtypeStruct(cv.shape, F32),
                   jax.ShapeDtypeStruct((nseq, ATTN_DIM), F32)],
        compiler_params=pltpu.CompilerParams(
            dimension_semantics=("parallel",), vmem_limit_bytes=VMEM_LIMIT),
        name="attn_sample",
    )(sinks, q, kn, vn, ck, cv, ga)


def _prep_wq_kernel(w_ref, wq_ref):
    low = lax.broadcasted_iota(jnp.int32, (w_ref.shape[0], LANES), 1) < HALF
    for grp in range(GQA_GROUP):
        def piece(kv, want_low):
            col = kv * (GQA_GROUP // 2) + grp // 2
            blk = w_ref[:, col * LANES:(col + 1) * LANES]
            return blk if (grp % 2 == 0) == want_low else pltpu.roll(blk, HALF, 1)
        for pair in range(N_KV_HEADS // 2):
            col = grp * (N_KV_HEADS // 2) + pair
            wq_ref[:, col * LANES:(col + 1) * LANES] = jnp.where(
                low, piece(2 * pair, True), piece(2 * pair + 1, False)).astype(BF16)


def _prep_wq(w_in2d, *, tr):
    rows = w_in2d.shape[0]
    return pl.pallas_call(
        _prep_wq_kernel,
        grid=(rows // tr,),
        in_specs=[pl.BlockSpec((tr, ATTN_DIM), lambda i: (i, 0))],
        out_specs=pl.BlockSpec((tr, ATTN_DIM), lambda i: (i, 0)),
        out_shape=jax.ShapeDtypeStruct((rows, ATTN_DIM), BF16),
        compiler_params=pltpu.CompilerParams(
            dimension_semantics=("parallel",), vmem_limit_bytes=VMEM_LIMIT),
        name="prep_wq",
    )(w_in2d)


def _group_major(w, axis):
    shape = w.shape
    w = w.reshape(shape[:axis] + (N_KV_HEADS, GQA_GROUP, HEAD_DIM) + shape[axis + 1:])
    return jnp.swapaxes(w, axis, axis + 1).reshape(shape)


def kernel(x_prompt, x_sample, cache_swa_k, cache_swa_v, state_conv, meta_tokens, g_ffn1, w1_gate, w1_up, w1_down, g_mix, w_in, attn_sinks, w_conv, g_attn_out, g_conv_out, w_out, g_ffn2, w2_gate, w2_up, w2_down, g_final):
    assert g_ffn1.shape[0] == 1, "one layer"
    batch, seq, _ = x_prompt.shape
    n_dec = x_sample.shape[0]
    assert x_sample.shape[1] == 1 and cache_swa_k.shape[2] == WINDOW

    row2d = lambda a: a.reshape(1, -1).astype(F32)
    wq = _prep_wq(w_in[0], tr=PREP_ROWS)
    wo_a = _group_major(w_out[0, :ATTN_DIM], 0).astype(BF16)
    wo_c = w_out[0, ATTN_DIM:].astype(BF16)
    ga = row2d(_group_major(g_attn_out[0], 0))
    gc = row2d(g_conv_out[0])
    sinks = attn_sinks[0].astype(F32)
    wc = w_conv[0].astype(F32)

    xp = x_prompt.reshape(batch * seq, D_MODEL)
    xt = jnp.concatenate([x_sample[:, 0, :], meta_tokens.astype(x_prompt.dtype)], axis=0)
    n_tail = xt.shape[0]

    xh1, w1g, w1u, w1d, xt1 = _ffn(xp, row2d(g_ffn1[0]), w1_gate[0], w1_up[0], w1_down[0],
                                   tm=FFN_TM, tf=HEAD_TF, n_rows=FFN_TM, extra=(xt, None),
                                   name="ffn1_head")
    xp1, wr = _ffn(xp, row2d(g_ffn1[0]), w1g, w1u, w1d, tm=FFN_TM, tf=FFN_TF,
                   side_cast=(w_in[0], ATTN_DIM), head=xh1, name="ffn1_prompt")

    st = state_conv[0]
    zpad = jnp.zeros((N_META, CONV_DIM), F32)
    h0 = jnp.concatenate([st[:, 0, :], zpad], axis=0)
    h1 = jnp.concatenate([st[:, 1, :], zpad], axis=0)
    qt, kt, vt, ut, cnt = _proj_tail(xt1, row2d(g_mix[0]), wq, wr, wc, gc, h0, h1)
    kmeta = jnp.pad(kt[n_dec:], ((WINDOW - N_META, 0), (0, 0)))
    vmeta = jnp.pad(vt[n_dec:], ((WINDOW - N_META, 0), (0, 0)))
    umeta = ut[n_tail - 8:]

    qp, kp, vp, cnp, cstate, w2g, w2u, w2d = _proj_prompt(
        xp1, row2d(g_mix[0]), wq, wr, wc, gc, umeta, (w2_gate[0], w2_up[0], w2_down[0]),
        tm=PROJ_TM, seq=seq)
    xp2, xpn = _attn_merge(sinks, qp, kp, vp, kmeta, vmeta, ga, xp1, cnp, wo_a, wo_c,
                           row2d(g_ffn2[0]), seq=seq, nsub=ATTN_NSUB)

    to_channel_major = lambda c: jnp.transpose(c[0], (0, 2, 3, 1)).reshape(n_dec, KV_DIM, WINDOW)
    from_channel_major = lambda c: jnp.transpose(
        c.reshape(n_dec, N_KV_HEADS, HEAD_DIM, WINDOW), (0, 3, 1, 2))[None]
    nk, nv, ans = _attn_sample(sinks, qt[:n_dec], kt[:n_dec], vt[:n_dec],
                               to_channel_major(cache_swa_k), to_channel_major(cache_swa_v), ga,
                               bs=SAMPLE_BS)

    xs2, xsn = _merge(xt1[:n_dec], ans, cnt[:n_dec], wo_a, wo_c, row2d(g_ffn2[0]), tm=n_dec,
                      name="merge_tail")
    yp, ys = _ffn(xp2, xpn, w2g, w2u, w2d, tm=FFN_TM, tf=FFN_TF, g_final=row2d(g_final),
                  extra=(xs2, xsn), name="ffn2_prompt")

    kv_shape = (1, batch, WINDOW, N_KV_HEADS, HEAD_DIM)
    new_k_prompt = kp.reshape(batch, seq, KV_DIM)[:, seq - WINDOW:].reshape(kv_shape)
    new_v_prompt = vp.reshape(batch, seq, KV_DIM)[:, seq - WINDOW:].reshape(kv_shape)
    new_conv_sample = jnp.stack([st[:, 1, :], ut[:n_dec]], axis=1)[None]
    return (yp.reshape(batch, seq, D_MODEL), ys.reshape(n_dec, 1, D_MODEL),
            new_k_prompt, new_v_prompt, cstate[None],
            from_channel_major(nk), from_channel_major(nv), new_conv_sample)
```

```python
import functools

import jax
import jax.numpy as jnp
from jax import lax
from jax.experimental import pallas as pl
from jax.experimental.pallas import tpu as pltpu

D_MODEL = 2048
D_FF = 5632
N_META = 16
ATTN_DIM = 1024
CONV_DIM = 1024
HEAD_DIM = 64
N_HEADS = 16
N_KV_HEADS = 4
GQA_GROUP = 4
KV_DIM = 256
WINDOW = 128
IN_DIM = ATTN_DIM + 2 * KV_DIM + 3 * CONV_DIM
EPS = 1e-5
NEG = -1e30

LANES = 128
HALF = LANES // 2
VMEM_LIMIT = 60 * 1024 * 1024
FFN_TM, FFN_TF = 1024, 512
HEAD_TF = 256
FFN_EDGE_ROWS = 256
PROJ_TM = 512
PROJ_ROW_CHUNKS = 2
CAST_TF = 256
ATTN_NSUB = 4
PREP_ROWS = 512
SAMPLE_BS = 16

F32 = jnp.float32
BF16 = jnp.bfloat16


def _rms(x, g):
    return x * lax.rsqrt(jnp.mean(x * x, axis=-1, keepdims=True) + EPS) * g


def _dot(a, b):
    return jnp.dot(a, b, preferred_element_type=F32)


def _dot_t(a, b):
    return lax.dot_general(a, b, (((1,), (1,)), ((), ())), preferred_element_type=F32)


def _resident(shape):
    return pl.BlockSpec(shape, lambda *_: (0,) * len(shape), pipeline_mode=pl.Buffered(1))


def _ffn_kernel(*refs, n_ff, normed_input, has_final, cast_weights, side_chunks, prefetch_x,
                head_given, extra_rows):
    refs = list(refs)
    x_ref, norm_ref, wg_ref, wu_ref, wd_ref = refs[:5]
    refs = refs[5:]
    if has_final:
        gf_ref = refs.pop(0)
    if side_chunks:
        side_src_ref = refs.pop(0)
    if head_given:
        head_hbm = refs.pop(0)
    if extra_rows:
        xe_ref = refs.pop(0)
        if normed_input:
            xne_ref = refs.pop(0)
    out_ref = refs.pop(0)
    if cast_weights:
        wgb_ref, wub_ref, wdb_ref = refs[:3]
        refs = refs[3:]
    if side_chunks:
        side_dst_ref = refs.pop(0)
    if extra_rows:
        oute_ref = refs.pop(0)
    xn_ref = norm_ref if normed_input else refs.pop(0)
    if extra_rows and not normed_input:
        xne_ref = refs.pop(0)
    i = pl.program_id(0)
    j = pl.program_id(1)

    if prefetch_x:
        x_hbm, x_ref, x_sem = x_ref, refs.pop(0), refs.pop(0)
        tm = x_ref.shape[0]

        def x_copy(tile):
            return pltpu.make_async_copy(x_hbm.at[pl.ds(tile * tm, tm), :], x_ref, x_sem)

        @pl.when(j == 0)
        def _():
            pl.when(i == 0)(lambda: x_copy(0).start())
            x_copy(i).wait()

        pl.when((j == 1) & (i + 1 < pl.num_programs(0)))(lambda: x_copy(i + 1).start())

    if head_given:
        head_sem = refs.pop(0)

        @pl.when((i == 0) & (j == 0))
        def _():
            head_copy = pltpu.make_async_copy(head_hbm, out_ref, head_sem)
            head_copy.start()
            head_copy.wait()

    if side_chunks:
        @pl.when(pl.program_id(0) * n_ff + j < side_chunks)
        def _():
            side_dst_ref[...] = side_src_ref[...].astype(BF16)

    def chunk_step(first, last, chunks):
        wg, wu, wd = wg_ref, wu_ref, wd_ref
        if cast_weights:
            wgb_ref[...] = wg_ref[...].astype(BF16)
            wub_ref[...] = wu_ref[...].astype(BF16)
            wdb_ref[...] = wd_ref[...].astype(BF16)
            wg, wu, wd = wgb_ref, wub_ref, wdb_ref
        cm = out_ref.shape[0] // chunks

        def row_group(xr, xnr, outr, rows):
            if first and not normed_input:
                xn = _rms(xr[rows, :], norm_ref[...]).astype(BF16)
                xnr[rows, :] = xn
            else:
                xn = xnr[rows, :]
            gate = _dot(xn, wg[...])
            up = _dot(xn, wu[...])
            h = (gate * jax.nn.sigmoid(gate) * up * 0.5).astype(BF16)
            acc = (xr[rows, :] if first else outr[rows, :]) + _dot(h, wd[...])
            if last and has_final:
                acc = _rms(acc, gf_ref[...])
            outr[rows, :] = acc

        for c in range(chunks):
            row_group(x_ref, xn_ref, out_ref, slice(c * cm, (c + 1) * cm))
        if extra_rows:
            extra_group = lambda: row_group(xe_ref, xne_ref, oute_ref, slice(None))
            pl.when(i == 0)(extra_group) if extra_rows == "tile0" else extra_group()

    row_chunks = max(1, out_ref.shape[0] // FFN_EDGE_ROWS)
    first_chunks = 1 if normed_input else row_chunks
    last_chunks = row_chunks if has_final else 1
    if n_ff == 1:
        chunk_step(True, True, row_chunks)
        return
    active = (i > 0) if head_given else True
    pl.when(active & (j == 0))(lambda: chunk_step(True, False, first_chunks))
    if has_final:
        pl.when(active & (j > 0) & (j < n_ff - 1))(lambda: chunk_step(False, False, 1))
        pl.when(active & (j == n_ff - 1))(lambda: chunk_step(False, True, last_chunks))
    else:
        pl.when(active & (j > 0))(lambda: chunk_step(False, False, 1))


def _ffn(x, norm, wg, wu, wd, *, tm, tf, g_final=None, side_cast=None, head=None, n_rows=None,
         extra=None, name):
    m = n_rows or x.shape[0]
    xe, xne = extra if extra is not None else (None, None)
    assert extra is None or ((xne is not None) == (norm.shape[0] == m) and head is None)
    assert m % tm == 0 and D_FF % tf == 0 and FFN_TF % tf == 0
    n_ff = D_FF // tf
    normed_input = norm.shape[0] == m
    cast_weights = wg.dtype == F32
    assert not cast_weights or m == tm
    assert head is None or (head.shape == (tm, D_MODEL) and m > tm)
    side_src, side_col0 = side_cast if side_cast is not None else (None, 0)
    side_chunks = 0 if side_src is None else (side_src.shape[1] - side_col0) // CAST_TF
    assert side_chunks <= (m // tm) * n_ff and side_col0 % CAST_TF == 0
    row = lambda i, j: (i, 0)
    side_chunk = lambda i, j: jnp.minimum(i * n_ff + j, side_chunks - 1)
    side_in = pl.BlockSpec((D_MODEL, CAST_TF),
                           lambda i, j: (0, side_col0 // CAST_TF + side_chunk(i, j)))
    side_out = pl.BlockSpec((D_MODEL, CAST_TF), lambda i, j: (0, side_chunk(i, j)))
    wj = (lambda i, j: jnp.where(i == 0, 0, j)) if head is not None else (lambda i, j: j)
    per_ff = FFN_TF // tf
    wd_spec = pl.BlockSpec((tf, D_MODEL), lambda i, j: (wj(i, j), 0))
    w_f32_specs = [pl.BlockSpec((D_MODEL, tf), lambda i, j: (0, j))] * 2 + [wd_spec]
    w_bf16_specs = [pl.BlockSpec((None, D_MODEL, tf),
                                 lambda i, j: (wj(i, j) // per_ff, 0, wj(i, j) % per_ff))] * 2
    w_bf16_specs += [wd_spec]
    prefetch_x = m // tm > 1 and n_ff > 1
    in_specs = [pl.BlockSpec(memory_space=pl.ANY) if prefetch_x
                else pl.BlockSpec((tm, D_MODEL), row, pipeline_mode=pl.Buffered(1)),
                pl.BlockSpec((tm, D_MODEL), row) if normed_input else _resident((1, D_MODEL)),
                *(w_f32_specs if cast_weights else w_bf16_specs)]
    args = [x, norm, wg, wu, wd]
    if g_final is not None:
        in_specs.append(_resident((1, D_MODEL)))
        args.append(g_final)
    if side_chunks:
        in_specs.append(side_in)
        args.append(side_src)
    if head is not None:
        in_specs.append(pl.BlockSpec(memory_space=pl.ANY))
        args.append(head)
    if extra is not None:
        in_specs.append(_resident(xe.shape))
        args.append(xe)
        if xne is not None:
            in_specs.append(_resident(xne.shape))
            args.append(xne)
    out_specs = [pl.BlockSpec((tm, D_MODEL), row)]
    out_shape = [jax.ShapeDtypeStruct((m, D_MODEL), F32)]
    if cast_weights:
        out_specs += w_bf16_specs
        out_shape += [jax.ShapeDtypeStruct((D_FF // FFN_TF, D_MODEL, FFN_TF), BF16)] * 2
        out_shape += [jax.ShapeDtypeStruct(wd.shape, BF16)]
    if side_chunks:
        out_specs.append(side_out)
        out_shape.append(jax.ShapeDtypeStruct((D_MODEL, side_chunks * CAST_TF), BF16))
    if extra is not None:
        out_specs.append(pl.BlockSpec(xe.shape, lambda i, j: (0, 0)))
        out_shape.append(jax.ShapeDtypeStruct(xe.shape, F32))
    scratch_shapes = [] if normed_input else [pltpu.VMEM((tm, D_MODEL), BF16)]
    if extra is not None and xne is None:
        scratch_shapes += [pltpu.VMEM(xe.shape, BF16)]
    if prefetch_x:
        scratch_shapes += [pltpu.VMEM((tm, D_MODEL), F32), pltpu.SemaphoreType.DMA(())]
    if head is not None:
        scratch_shapes += [pltpu.SemaphoreType.DMA(())]
    ordered_rows = bool(side_chunks) or prefetch_x
    outs = pl.pallas_call(
        functools.partial(_ffn_kernel, n_ff=n_ff, normed_input=normed_input,
                          has_final=g_final is not None, cast_weights=cast_weights,
                          side_chunks=side_chunks, prefetch_x=prefetch_x,
                          head_given=head is not None,
                          extra_rows=(None if extra is None else "tile0" if m > tm else "all")),
        grid=(m // tm, n_ff),
        in_specs=in_specs,
        out_specs=out_specs,
        out_shape=out_shape,
        scratch_shapes=scratch_shapes,
        compiler_params=pltpu.CompilerParams(
            dimension_semantics=("arbitrary" if ordered_rows else "parallel", "arbitrary"),
            vmem_limit_bytes=VMEM_LIMIT),
        name=name,
    )(*args)
    return outs if len(outs) > 1 else outs[0]


def _merge_kernel(x_ref, a_ref, c_ref, woa_ref, woc_ref, g_ref, x2_ref, xn_ref):
    x = x_ref[...] + (_dot(a_ref[...].astype(BF16), woa_ref[...]) +
                      _dot(c_ref[...].astype(BF16), woc_ref[...]))
    x2_ref[...] = x
    xn_ref[...] = _rms(x, g_ref[...]).astype(BF16)


def _merge(x, a, c, woa, woc, g, *, tm, name):
    m = x.shape[0]
    assert m % tm == 0
    row = lambda i: (i, 0)
    return pl.pallas_call(
        _merge_kernel,
        grid=(m // tm,),
        in_specs=[pl.BlockSpec((tm, D_MODEL), row), pl.BlockSpec((tm, ATTN_DIM), row),
                  pl.BlockSpec((tm, CONV_DIM), row), _resident((ATTN_DIM, D_MODEL)),
                  _resident((CONV_DIM, D_MODEL)), _resident((1, D_MODEL))],
        out_specs=[pl.BlockSpec((tm, D_MODEL), row), pl.BlockSpec((tm, D_MODEL), row)],
        out_shape=[jax.ShapeDtypeStruct((m, D_MODEL), F32),
                   jax.ShapeDtypeStruct((m, D_MODEL), BF16)],
        compiler_params=pltpu.CompilerParams(
            dimension_semantics=("parallel",), vmem_limit_bytes=VMEM_LIMIT),
        name=name,
    )(x, a, c, woa, woc, g)


K0, V0, B0, C0, H0, R_END = 0, 256, 512, 1536, 2560, 3584


def _project_conv(x_ref, g_ref, wr_ref):
    xn = _rms(x_ref[...], g_ref[...]).astype(BF16)
    gate_b = _dot(xn, wr_ref[:, B0:C0])
    u = _dot(xn, wr_ref[:, C0:H0]) * _dot(xn, wr_ref[:, H0:R_END])
    return xn, gate_b, u


def _project_qkv(xn, wq_ref, wr_ref):
    q = _dot(xn, wq_ref[...]) * (HEAD_DIM ** -0.5)
    return q, _dot(xn, wr_ref[:, K0:V0]), _dot(xn, wr_ref[:, V0:B0])


def _proj_prompt_kernel(x_ref, g_ref, wq_ref, wr_ref, wc_ref, gc_ref, umeta_ref,
                        fg_ref, fu_ref, fd_ref,
                        q_ref, k_ref, v_ref, cn_ref, cst_ref, bg_ref, bu_ref, bd_ref,
                        us_ref, *, tm, tiles_per_seq):
    t = pl.program_id(0) % tiles_per_seq

    @pl.when(t == 0)
    def _():
        us_ref[0:8, :] = umeta_ref[...]

    @pl.when(t != 0)
    def _():
        us_ref[0:8, :] = us_ref[tm:tm + 8, :]

    hm = tm // PROJ_ROW_CHUNKS
    casts = ((fg_ref, bg_ref), (fu_ref, bu_ref), (fd_ref, bd_ref))
    for h in range(PROJ_ROW_CHUNKS):
        rows = slice(h * hm, (h + 1) * hm)
        for src, dst in casts[h::PROJ_ROW_CHUNKS]:
            dst[...] = src[...].astype(BF16)
        xn, gate_b, u = _project_conv(x_ref.at[rows], g_ref, wr_ref)
        us_ref[8 + h * hm:8 + (h + 1) * hm, :] = u
        y = (wc_ref[0:1, :] * us_ref[6 + h * hm:6 + (h + 1) * hm, :]
             + wc_ref[1:2, :] * us_ref[7 + h * hm:7 + (h + 1) * hm, :] + wc_ref[2:3, :] * u)
        cn_ref[rows, :] = _rms(gate_b * y, gc_ref[...]).astype(BF16)
        q, k, v = _project_qkv(xn, wq_ref, wr_ref)
        q_ref[rows, :] = q.astype(BF16)
        k_ref[rows, :] = k
        v_ref[rows, :] = v
    cst_ref[0] = us_ref[tm + 6:tm + 8, :]


def _proj_prompt(x, g, wq, wr, wc, gc, umeta, ffn_w, *, tm, seq):
    m = x.shape[0]
    tiles_per_seq = seq // tm
    n_steps = m // tm
    n_cast = D_FF // CAST_TF
    assert D_FF % CAST_TF == 0 and n_cast <= n_steps
    row = lambda i: (i, 0)
    chunk = lambda i: jnp.minimum(i, n_cast - 1)
    cast_cols = pl.BlockSpec((D_MODEL, CAST_TF), lambda i: (0, chunk(i)))
    cast_rows = pl.BlockSpec((CAST_TF, D_MODEL), lambda i: (chunk(i), 0))
    per_ff = FFN_TF // CAST_TF
    assert FFN_TF % CAST_TF == 0
    cast_cols_out = pl.BlockSpec((None, D_MODEL, CAST_TF),
                                 lambda i: (chunk(i) // per_ff, 0, chunk(i) % per_ff))
    chunked = jax.ShapeDtypeStruct((D_FF // FFN_TF, D_MODEL, FFN_TF), BF16)
    return pl.pallas_call(
        functools.partial(_proj_prompt_kernel, tm=tm, tiles_per_seq=tiles_per_seq),
        grid=(n_steps,),
        in_specs=[pl.BlockSpec((tm, D_MODEL), row), _resident((1, D_MODEL)),
                  _resident((D_MODEL, ATTN_DIM)), _resident((D_MODEL, R_END)),
                  _resident((3, CONV_DIM)), _resident((1, CONV_DIM)), _resident((8, CONV_DIM)),
                  cast_cols, cast_cols, cast_rows],
        out_specs=[pl.BlockSpec((tm, ATTN_DIM), row), pl.BlockSpec((tm, KV_DIM), row),
                   pl.BlockSpec((tm, KV_DIM), row), pl.BlockSpec((tm, CONV_DIM), row),
                   pl.BlockSpec((1, 2, CONV_DIM), lambda i: (i // tiles_per_seq, 0, 0)),
                   cast_cols_out, cast_cols_out, cast_rows],
        out_shape=[jax.ShapeDtypeStruct((m, ATTN_DIM), BF16),
                   jax.ShapeDtypeStruct((m, KV_DIM), F32),
                   jax.ShapeDtypeStruct((m, KV_DIM), F32),
                   jax.ShapeDtypeStruct((m, CONV_DIM), BF16),
                   jax.ShapeDtypeStruct((m // seq, 2, CONV_DIM), F32),
                   chunked, chunked, jax.ShapeDtypeStruct(ffn_w[2].shape, BF16)],
        scratch_shapes=[pltpu.VMEM((tm + 8, CONV_DIM), F32)],
        compiler_params=pltpu.CompilerParams(
            dimension_semantics=("arbitrary",), vmem_limit_bytes=VMEM_LIMIT),
        name="proj_prompt",
    )(x, g, wq, wr, wc, gc, umeta, *ffn_w)


def _proj_tail_kernel(x_ref, g_ref, wq_ref, wr_ref, wc_ref, gc_ref, h0_ref, h1_ref,
                      q_ref, k_ref, v_ref, u_ref, cn_ref):
    xn, gate_b, u = _project_conv(x_ref, g_ref, wr_ref)
    q, k, v = _project_qkv(xn, wq_ref, wr_ref)
    q_ref[...] = q
    k_ref[...] = k
    v_ref[...] = v
    u_ref[...] = u
    y = wc_ref[0:1, :] * h0_ref[...] + wc_ref[1:2, :] * h1_ref[...] + wc_ref[2:3, :] * u
    cn_ref[...] = _rms(gate_b * y, gc_ref[...]).astype(BF16)


def _proj_tail(x, g, wq, wr, wc, gc, h0, h1):
    m = x.shape[0]
    full = lambda shape: pl.BlockSpec(shape, lambda i: (0,) * len(shape))
    return pl.pallas_call(
        _proj_tail_kernel,
        grid=(1,),
        in_specs=[full((m, D_MODEL)), full((1, D_MODEL)), _resident((D_MODEL, ATTN_DIM)),
                  _resident((D_MODEL, R_END)),
                  full((3, CONV_DIM)), full((1, CONV_DIM)), full((m, CONV_DIM)),
                  full((m, CONV_DIM))],
        out_specs=[full((m, ATTN_DIM)), full((m, KV_DIM)), full((m, KV_DIM)),
                   full((m, CONV_DIM)), full((m, CONV_DIM))],
        out_shape=[jax.ShapeDtypeStruct((m, ATTN_DIM), F32),
                   jax.ShapeDtypeStruct((m, KV_DIM), F32),
                   jax.ShapeDtypeStruct((m, KV_DIM), F32),
                   jax.ShapeDtypeStruct((m, CONV_DIM), F32),
                   jax.ShapeDtypeStruct((m, CONV_DIM), BF16)],
        compiler_params=pltpu.CompilerParams(
            dimension_semantics=("arbitrary",), vmem_limit_bytes=VMEM_LIMIT),
        name="proj_tail",
    )(x, g, wq, wr, wc, gc, h0, h1)


class _BandedAttention:
    def __init__(self, first, sink_ref, q_ref, kprev, kcur, vprev, vcur, ga_ref, o_ref, nsub):
        self.sink_ref, self.q_ref, self.ga_ref, self.o_ref = sink_ref, q_ref, ga_ref, o_ref
        self.n_pairs = N_KV_HEADS // 2
        self.units = [(sb, pair) for sb in range(nsub) for pair in range(self.n_pairs)]
        kall = jnp.concatenate([kprev, kcur], axis=0)
        vall = jnp.concatenate([vprev, vcur], axis=0)
        c = lax.broadcasted_iota(jnp.int32, (2 * WINDOW, WINDOW), 0)
        r = lax.broadcasted_iota(jnp.int32, (2 * WINDOW, WINDOW), 1)
        self.band = (c >= r) & (c <= r + WINDOW)
        self.band_first = self.band & ((c >= WINDOW - N_META) | jnp.logical_not(first))
        low = lax.broadcasted_iota(jnp.int32, kall.shape[:1] + (LANES,), 1) < HALF
        self.k_half, self.vt_half = [], []
        for pair in range(self.n_pairs):
            kcol = kall[:, pair * LANES:(pair + 1) * LANES]
            vcol = vall[:, pair * LANES:(pair + 1) * LANES]
            self.k_half.append((jnp.where(low, kcol, 0.0).astype(BF16),
                                jnp.where(low, 0.0, kcol).astype(BF16)))
            self.vt_half.append((jnp.where(low, vcol, 0.0).T.astype(BF16),
                                 jnp.where(low, 0.0, vcol).T.astype(BF16)))
        self.outs = {}

    def scores(self, unit):
        sb, pair = unit
        rows = slice(sb * WINDOW, (sb + 1) * WINDOW)
        keys = slice(sb * WINDOW, (sb + 2) * WINDOW)
        kst = jnp.concatenate([self.k_half[pair][0][keys], self.k_half[pair][1][keys]], axis=0)
        qst = jnp.concatenate(
            [self.q_ref[rows, (grp * self.n_pairs + pair) * LANES:
                        (grp * self.n_pairs + pair + 1) * LANES]
             for grp in range(GQA_GROUP)], axis=0)
        return _dot_t(kst, qst)

    def softmax(self, unit, st_all):
        sb, pair = unit
        valid = self.band_first if sb == 0 else self.band
        pt_rows = []
        for half in range(2):
            pt_cols = []
            for grp in range(GQA_GROUP):
                sink = self.sink_ref[(2 * pair + half) * GQA_GROUP + grp]
                st = st_all[half * 2 * WINDOW:(half + 1) * 2 * WINDOW,
                            grp * WINDOW:(grp + 1) * WINDOW]
                st = jnp.where(valid, st, NEG)
                mx = jnp.maximum(jnp.max(st, axis=0, keepdims=True), sink)
                e = jnp.exp(st - mx)
                den = jnp.sum(e, axis=0, keepdims=True) + jnp.exp(sink - mx)
                pt_cols.append((e * (1.0 / den)).astype(BF16))
            pt_rows.append(jnp.concatenate(pt_cols, axis=1))
        return jnp.concatenate(pt_rows, axis=0)

    def values(self, unit, pt):
        sb, pair = unit
        keys = slice(sb * WINDOW, (sb + 2) * WINDOW)
        vst = jnp.concatenate([self.vt_half[pair][0][:, keys], self.vt_half[pair][1][:, keys]],
                              axis=1)
        ot_all = _dot(vst, pt)
        for grp in range(GQA_GROUP):
            self.outs[(sb, grp * self.n_pairs + pair)] = ot_all[:, grp * WINDOW:(grp + 1) * WINDOW]
        if pair == self.n_pairs - 1:
            self._normalise(sb)

    def _normalise(self, sb):
        rows = slice(sb * WINDOW, (sb + 1) * WINDOW)
        cols = range(ATTN_DIM // LANES)
        ssq = sum(jnp.sum(self.outs[(sb, col)] ** 2, axis=0, keepdims=True) for col in cols)
        inv = lax.rsqrt(ssq * (1.0 / ATTN_DIM) + EPS)
        for col in cols:
            sl = slice(col * LANES, (col + 1) * LANES)
            self.o_ref[rows, sl] = ((self.outs.pop((sb, col)) * inv).T
                                    * self.ga_ref[:, sl]).astype(BF16)


def _attn_merge_kernel(sink_ref, q_ref, kp_ref, kc_ref, vp_ref, vc_ref, km_ref, vm_ref, ga_ref,
                       x_ref, c_ref, woa_ref, woc_ref, g_ref, x2_ref, xn_ref,
                       an_new_ref, an_old_ref, *, nsub, tiles_per_seq, n_tiles):
    t = pl.program_id(0)

    @pl.when(t == 0)
    def _():
        an_new_ref[...] = jnp.zeros(an_new_ref.shape, an_new_ref.dtype)

    an_old_ref[...] = an_new_ref[...]

    first = (jnp.minimum(t, n_tiles - 1) % tiles_per_seq) == 0
    attn = _BandedAttention(first, sink_ref, q_ref,
                            jnp.where(first, km_ref[...], kp_ref[...]), kc_ref[...],
                            jnp.where(first, vm_ref[...], vp_ref[...]), vc_ref[...],
                            ga_ref, an_new_ref, nsub)
    units = attn.units
    n_chunks = len(units)
    cw = D_MODEL // n_chunks
    ssq = jnp.zeros((x_ref.shape[0], 1), F32)
    st_next = attn.scores(units[0])
    pt_prev = None
    for u, unit in enumerate(units):
        cols = slice(u * cw, (u + 1) * cw)
        xc = x_ref[:, cols] + (_dot(an_old_ref[...], woa_ref[:, cols]) +
                               _dot(c_ref[...], woc_ref[:, cols]))
        x2_ref[:, cols] = xc
        ssq = ssq + jnp.sum(xc * xc, axis=-1, keepdims=True)
        st_all, st_next = st_next, (attn.scores(units[u + 1]) if u + 1 < n_chunks else None)
        if pt_prev is not None:
            attn.values(units[u - 1], pt_prev)
        pt_prev = attn.softmax(unit, st_all)
    attn.values(units[-1], pt_prev)
    inv = lax.rsqrt(ssq * (1.0 / D_MODEL) + EPS)
    for u in range(n_chunks):
        cols = slice(u * cw, (u + 1) * cw)
        xn_ref[:, cols] = (x2_ref[:, cols] * inv * g_ref[:, cols]).astype(BF16)


def _attn_merge(sinks, q, k, v, kmeta, vmeta, ga, x, c, woa, woc, g, *, seq, nsub):
    m = x.shape[0]
    tm = nsub * WINDOW
    n_tiles = m // tm
    tiles_per_seq = seq // tm
    cur = lambda t: (jnp.minimum(t, n_tiles - 1), 0)
    prev = lambda t: (jnp.maximum(jnp.minimum(t, n_tiles - 1) * nsub - 1, 0), 0)
    lag = lambda t: (jnp.maximum(t - 1, 0), 0)
    return pl.pallas_call(
        functools.partial(_attn_merge_kernel, nsub=nsub, tiles_per_seq=tiles_per_seq,
                          n_tiles=n_tiles),
        grid=(n_tiles + 1,),
        in_specs=[pl.BlockSpec(memory_space=pltpu.SMEM),
                  pl.BlockSpec((tm, ATTN_DIM), cur),
                  pl.BlockSpec((WINDOW, KV_DIM), prev), pl.BlockSpec((tm, KV_DIM), cur),
                  pl.BlockSpec((WINDOW, KV_DIM), prev), pl.BlockSpec((tm, KV_DIM), cur),
                  _resident((WINDOW, KV_DIM)), _resident((WINDOW, KV_DIM)),
                  _resident((1, ATTN_DIM)),
                  pl.BlockSpec((tm, D_MODEL), lag), pl.BlockSpec((tm, CONV_DIM), lag),
                  _resident((ATTN_DIM, D_MODEL)), _resident((CONV_DIM, D_MODEL)),
                  _resident((1, D_MODEL))],
        out_specs=[pl.BlockSpec((tm, D_MODEL), lag), pl.BlockSpec((tm, D_MODEL), lag)],
        out_shape=[jax.ShapeDtypeStruct((m, D_MODEL), F32),
                   jax.ShapeDtypeStruct((m, D_MODEL), BF16)],
        scratch_shapes=[pltpu.VMEM((tm, ATTN_DIM), BF16), pltpu.VMEM((tm, ATTN_DIM), BF16)],
        compiler_params=pltpu.CompilerParams(
            dimension_semantics=("arbitrary",), vmem_limit_bytes=VMEM_LIMIT),
        name="attn_merge_prompt",
    )(sinks, q, k, k, v, v, kmeta, vmeta, ga, x, c, woa, woc, g)


def _attn_sample_kernel(sink_ref, q_ref, kn_ref, vn_ref, ck_ref, cv_ref, ga_ref,
                        ok_ref, ov_ref, an_ref, *, bs):
    n_pairs = N_KV_HEADS // 2
    heads_per_col = 2 * GQA_GROUP
    last_key = lax.broadcasted_iota(jnp.int32, (KV_DIM, WINDOW), 1) == WINDOW - 1
    pad = jnp.zeros((WINDOW - bs, KV_DIM), F32)
    knt = jnp.concatenate([kn_ref[...], pad], axis=0).T
    vnt = jnp.concatenate([vn_ref[...], pad], axis=0).T
    for s in range(bs):
        ok_ref[s] = jnp.where(last_key, knt[:, s:s + 1], pltpu.roll(ck_ref[s], WINDOW - 1, 1))
        ov_ref[s] = jnp.where(last_key, vnt[:, s:s + 1], pltpu.roll(cv_ref[s], WINDOW - 1, 1))

    n_rows = heads_per_col * bs
    row = lax.broadcasted_iota(jnp.int32, (n_rows, bs * WINDOW), 0)
    lane = lax.broadcasted_iota(jnp.int32, (n_rows, bs * WINDOW), 1)
    log2 = lambda n: n.bit_length() - 1
    assert bs == 1 << log2(bs) and WINDOW == 1 << log2(WINDOW)
    same_seq = (row & (bs - 1)) == (lane >> log2(WINDOW))
    head_row = lax.broadcasted_iota(jnp.int32, (n_rows, 1), 0) >> log2(bs)
    low = lax.broadcasted_iota(jnp.int32, (bs, LANES), 1) < HALF

    cols = [None] * (ATTN_DIM // LANES)
    for pair in range(n_pairs):
        sl = slice(pair * LANES, (pair + 1) * LANES)
        pieces = []
        sk = jnp.zeros((n_rows, 1), F32)
        for grp in range(GQA_GROUP):
            col = grp * n_pairs + pair
            qcol = q_ref[:, col * LANES:(col + 1) * LANES]
            pieces += [jnp.where(low, qcol, 0.0), jnp.where(low, 0.0, qcol)]
            for half in range(2):
                sink = sink_ref[(2 * pair + half) * GQA_GROUP + grp]
                sk = jnp.where(head_row == 2 * grp + half, sink, sk)
        lq = jnp.concatenate(pieces, axis=0).astype(BF16)
        kstack = jnp.concatenate([ck_ref[s, sl, :] for s in range(bs)], axis=1).astype(BF16)
        vstack = jnp.concatenate([cv_ref[s, sl, :] for s in range(bs)], axis=1).astype(BF16)
        s = jnp.where(same_seq, _dot(lq, kstack), NEG)
        k_new = jnp.concatenate([kn_ref[:, sl]] * heads_per_col, axis=0).astype(BF16)
        v_new = jnp.concatenate([vn_ref[:, sl]] * heads_per_col, axis=0).astype(BF16)
        s_new = jnp.sum(lq.astype(F32) * k_new.astype(F32), axis=-1, keepdims=True)
        mx = jnp.maximum(jnp.maximum(jnp.max(s, axis=-1, keepdims=True), s_new), sk)
        e = jnp.exp(s - mx)
        e_new = jnp.exp(s_new - mx)
        rden = 1.0 / (jnp.sum(e, axis=-1, keepdims=True) + e_new + jnp.exp(sk - mx))
        p = (e * rden).astype(BF16)
        p_new = (e_new * rden).astype(BF16).astype(F32)
        o = _dot_t(p, vstack) + p_new * v_new.astype(F32)
        for grp in range(GQA_GROUP):
            lo = o[(2 * grp) * bs:(2 * grp + 1) * bs]
            hi = o[(2 * grp + 1) * bs:(2 * grp + 2) * bs]
            cols[grp * n_pairs + pair] = jnp.where(low, lo, hi)
    an_ref[...] = _rms(jnp.concatenate(cols, axis=1), ga_ref[...])


def _attn_sample(sinks, q, kn, vn, ck, cv, ga, *, bs):
    nseq = q.shape[0]
    row = lambda i: (i, 0)
    blk3 = pl.BlockSpec((bs, KV_DIM, WINDOW), lambda i: (i, 0, 0))
    return pl.pallas_call(
        functools.partial(_attn_sample_kernel, bs=bs),
        grid=(nseq // bs,),
        in_specs=[pl.BlockSpec(memory_space=pltpu.SMEM),
                  pl.BlockSpec((bs, ATTN_DIM), row), pl.BlockSpec((bs, KV_DIM), row),
                  pl.BlockSpec((bs, KV_DIM), row), blk3, blk3,
                  pl.BlockSpec((1, ATTN_DIM), lambda i: (0, 0))],
        out_specs=[blk3, blk3, pl.BlockSpec((bs, ATTN_DIM), row)],
        out_shape=[jax.ShapeDtypeStruct(ck.shape, F32), jax.ShapeDtypeStruct(cv.shape, F32),
                   jax.ShapeDtypeStruct((nseq, ATTN_DIM), F32)],
        compiler_params=pltpu.CompilerParams(
            dimension_semantics=("parallel",), vmem_limit_bytes=VMEM_LIMIT),
        name="attn_sample",
    )(sinks, q, kn, vn, ck, cv, ga)


def _prep_wq_kernel(w_ref, wq_ref):
    low = lax.broadcasted_iota(jnp.int32, (w_ref.shape[0], LANES), 1) < HALF
    for grp in range(GQA_GROUP):
        def piece(kv, want_low):
            col = kv * (GQA_GROUP // 2) + grp // 2
            blk = w_ref[:, col * LANES:(col + 1) * LANES]
            return blk if (grp % 2 == 0) == want_low else pltpu.roll(blk, HALF, 1)
        for pair in range(N_KV_HEADS // 2):
            col = grp * (N_KV_HEADS // 2) + pair
            wq_ref[:, col * LANES:(col + 1) * LANES] = jnp.where(
                low, piece(2 * pair, True), piece(2 * pair + 1, False)).astype(BF16)


def _prep_wq(w_in2d, *, tr):
    rows = w_in2d.shape[0]
    return pl.pallas_call(
        _prep_wq_kernel,
        grid=(rows // tr,),
        in_specs=[pl.BlockSpec((tr, ATTN_DIM), lambda i: (i, 0))],
        out_specs=pl.BlockSpec((tr, ATTN_DIM), lambda i: (i, 0)),
        out_shape=jax.ShapeDtypeStruct((rows, ATTN_DIM), BF16),
        compiler_params=pltpu.CompilerParams(
            dimension_semantics=("parallel",), vmem_limit_bytes=VMEM_LIMIT),
        name="prep_wq",
    )(w_in2d)


def _group_major(w, axis):
    shape = w.shape
    w = w.reshape(shape[:axis] + (N_KV_HEADS, GQA_GROUP, HEAD_DIM) + shape[axis + 1:])
    return jnp.swapaxes(w, axis, axis + 1).reshape(shape)


def kernel(x_prompt, x_sample, cache_swa_k, cache_swa_v, state_conv, meta_tokens, g_ffn1, w1_gate, w1_up, w1_down, g_mix, w_in, attn_sinks, w_conv, g_attn_out, g_conv_out, w_out, g_ffn2, w2_gate, w2_up, w2_down, g_final):
    assert g_ffn1.shape[0] == 1, "one layer"
    batch, seq, _ = x_prompt.shape
    n_dec = x_sample.shape[0]
    assert x_sample.shape[1] == 1 and cache_swa_k.shape[2] == WINDOW

    row2d = lambda a: a.reshape(1, -1).astype(F32)
    wq = _prep_wq(w_in[0], tr=PREP_ROWS)
    wo_a = _group_major(w_out[0, :ATTN_DIM], 0).astype(BF16)
    wo_c = w_out[0, ATTN_DIM:].astype(BF16)
    ga = row2d(_group_major(g_attn_out[0], 0))
    gc = row2d(g_conv_out[0])
    sinks = attn_sinks[0].astype(F32)
    wc = w_conv[0].astype(F32)

    xp = x_prompt.reshape(batch * seq, D_MODEL)
    xt = jnp.concatenate([x_sample[:, 0, :], meta_tokens.astype(x_prompt.dtype)], axis=0)
    n_tail = xt.shape[0]

    xh1, w1g, w1u, w1d, xt1 = _ffn(xp, row2d(g_ffn1[0]), w1_gate[0], w1_up[0], w1_down[0],
                                   tm=FFN_TM, tf=HEAD_TF, n_rows=FFN_TM, extra=(xt, None),
                                   name="ffn1_head")
    xp1, wr = _ffn(xp, row2d(g_ffn1[0]), w1g, w1u, w1d, tm=FFN_TM, tf=FFN_TF,
                   side_cast=(w_in[0], ATTN_DIM), head=xh1, name="ffn1_prompt")

    st = state_conv[0]
    zpad = jnp.zeros((N_META, CONV_DIM), F32)
    h0 = jnp.concatenate([st[:, 0, :], zpad], axis=0)
    h1 = jnp.concatenate([st[:, 1, :], zpad], axis=0)
    qt, kt, vt, ut, cnt = _proj_tail(xt1, row2d(g_mix[0]), wq, wr, wc, gc, h0, h1)
    kmeta = jnp.pad(kt[n_dec:], ((WINDOW - N_META, 0), (0, 0)))
    vmeta = jnp.pad(vt[n_dec:], ((WINDOW - N_META, 0), (0, 0)))
    umeta = ut[n_tail - 8:]

    qp, kp, vp, cnp, cstate, w2g, w2u, w2d = _proj_prompt(
        xp1, row2d(g_mix[0]), wq, wr, wc, gc, umeta, (w2_gate[0], w2_up[0], w2_down[0]),
        tm=PROJ_TM, seq=seq)
    xp2, xpn = _attn_merge(sinks, qp, kp, vp, kmeta, vmeta, ga, xp1, cnp, wo_a, wo_c,
                           row2d(g_ffn2[0]), seq=seq, nsub=ATTN_NSUB)

    to_channel_major = lambda c: jnp.transpose(c[0], (0, 2, 3, 1)).reshape(n_dec, KV_DIM, WINDOW)
    from_channel_major = lambda c: jnp.transpose(
        c.reshape(n_dec, N_KV_HEADS, HEAD_DIM, WINDOW), (0, 3, 1, 2))[None]
    nk, nv, ans = _attn_sample(sinks, qt[:n_dec], kt[:n_dec], vt[:n_dec],
                               to_channel_major(cache_swa_k), to_channel_major(cache_swa_v), ga,
                               bs=SAMPLE_BS)

    xs2, xsn = _merge(xt1[:n_dec], ans, cnt[:n_dec], wo_a, wo_c, row2d(g_ffn2[0]), tm=n_dec,
                      name="merge_tail")
    yp, ys = _ffn(xp2, xpn, w2g, w2u, w2d, tm=FFN_TM, tf=FFN_TF, g_final=row2d(g_final),
                  extra=(xs2, xsn), name="ffn2_prompt")

    kv_shape = (1, batch, WINDOW, N_KV_HEADS, HEAD_DIM)
    new_k_prompt = kp.reshape(batch, seq, KV_DIM)[:, seq - WINDOW:].reshape(kv_shape)
    new_v_prompt = vp.reshape(batch, seq, KV_DIM)[:, seq - WINDOW:].reshape(kv_shape)
    new_conv_sample = jnp.stack([st[:, 1, :], ut[:n_dec]], axis=1)[None]
    return (yp.reshape(batch, seq, D_MODEL), ys.reshape(n_dec, 1, D_MODEL),
            new_k_prompt, new_v_prompt, cstate[None],
            from_channel_major(nk), from_channel_major(nv), new_conv_sample)
```

```python
import functools

import jax
import jax.numpy as jnp
from jax import lax
from jax.experimental import pallas as pl
from jax.experimental.pallas import tpu as pltpu

D_MODEL = 2048
D_FF = 5632
N_META = 16
ATTN_DIM = 1024
CONV_DIM = 1024
HEAD_DIM = 64
N_HEADS = 16
N_KV_HEADS = 4
GQA_GROUP = 4
KV_DIM = 256
WINDOW = 128
IN_DIM = ATTN_DIM + 2 * KV_DIM + 3 * CONV_DIM
EPS = 1e-5
NEG = -1e30

LANES = 128
HALF = LANES // 2
VMEM_LIMIT = 60 * 1024 * 1024
FFN_TM, FFN_TF = 1024, 512
HEAD_TF = 256
FFN_EDGE_ROWS = 256
PROJ_TM = 512
PROJ_ROW_CHUNKS = 2
CAST_TF = 256
ATTN_NSUB = 4
PREP_ROWS = 512
SAMPLE_BS = 16

F32 = jnp.float32
BF16 = jnp.bfloat16


def _rms(x, g):
    return x * lax.rsqrt(jnp.mean(x * x, axis=-1, keepdims=True) + EPS) * g


def _dot(a, b):
    return jnp.dot(a, b, preferred_element_type=F32)


def _dot_t(a, b):
    return lax.dot_general(a, b, (((1,), (1,)), ((), ())), preferred_element_type=F32)


def _resident(shape):
    return pl.BlockSpec(shape, lambda *_: (0,) * len(shape), pipeline_mode=pl.Buffered(1))


def _ffn_kernel(*refs, n_ff, normed_input, has_final, cast_weights, side_chunks, prefetch_x,
                head_given, extra_rows):
    refs = list(refs)
    x_ref, norm_ref, wg_ref, wu_ref, wd_ref = refs[:5]
    refs = refs[5:]
    if has_final:
        gf_ref = refs.pop(0)
    if side_chunks:
        side_src_ref = refs.pop(0)
    if head_given:
        head_hbm = refs.pop(0)
    if extra_rows:
        xe_ref = refs.pop(0)
        if normed_input:
            xne_ref = refs.pop(0)
    out_ref = refs.pop(0)
    if cast_weights:
        wgb_ref, wub_ref, wdb_ref = refs[:3]
        refs = refs[3:]
    if side_chunks:
        side_dst_ref = refs.pop(0)
    if extra_rows:
        oute_ref = refs.pop(0)
    xn_ref = norm_ref if normed_input else refs.pop(0)
    if extra_rows and not normed_input:
        xne_ref = refs.pop(0)
    i = pl.program_id(0)
    j = pl.program_id(1)

    if prefetch_x:
        x_hbm, x_ref, x_sem = x_ref, refs.pop(0), refs.pop(0)
        tm = x_ref.shape[0]

        def x_copy(tile):
            return pltpu.make_async_copy(x_hbm.at[pl.ds(tile * tm, tm), :], x_ref, x_sem)

        @pl.when(j == 0)
        def _():
            pl.when(i == 0)(lambda: x_copy(0).start())
            x_copy(i).wait()

        pl.when((j == 1) & (i + 1 < pl.num_programs(0)))(lambda: x_copy(i + 1).start())

    if head_given:
        head_sem = refs.pop(0)

        @pl.when((i == 0) & (j == 0))
        def _():
            head_copy = pltpu.make_async_copy(head_hbm, out_ref, head_sem)
            head_copy.start()
            head_copy.wait()

    if side_chunks:
        @pl.when(pl.program_id(0) * n_ff + j < side_chunks)
        def _():
            side_dst_ref[...] = side_src_ref[...].astype(BF16)

    def chunk_step(first, last, chunks):
        wg, wu, wd = wg_ref, wu_ref, wd_ref
        if cast_weights:
            wgb_ref[...] = wg_ref[...].astype(BF16)
            wub_ref[...] = wu_ref[...].astype(BF16)
            wdb_ref[...] = wd_ref[...].astype(BF16)
            wg, wu, wd = wgb_ref, wub_ref, wdb_ref
        cm = out_ref.shape[0] // chunks

        def row_group(xr, xnr, outr, rows):
            if first and not normed_input:
                xn = _rms(xr[rows, :], norm_ref[...]).astype(BF16)
                xnr[rows, :] = xn
            else:
                xn = xnr[rows, :]
            gate = _dot(xn, wg[...])
            up = _dot(xn, wu[...])
            h = (gate * jax.nn.sigmoid(gate) * up * 0.5).astype(BF16)
            acc = (xr[rows, :] if first else outr[rows, :]) + _dot(h, wd[...])
            if last and has_final:
                acc = _rms(acc, gf_ref[...])
            outr[rows, :] = acc

        for c in range(chunks):
            row_group(x_ref, xn_ref, out_ref, slice(c * cm, (c + 1) * cm))
        if extra_rows:
            extra_group = lambda: row_group(xe_ref, xne_ref, oute_ref, slice(None))
            pl.when(i == 0)(extra_group) if extra_rows == "tile0" else extra_group()

    row_chunks = max(1, out_ref.shape[0] // FFN_EDGE_ROWS)
    first_chunks = 1 if normed_input else row_chunks
    last_chunks = row_chunks if has_final else 1
    if n_ff == 1:
        chunk_step(True, True, row_chunks)
        return
    active = (i > 0) if head_given else True
    pl.when(active & (j == 0))(lambda: chunk_step(True, False, first_chunks))
    if has_final:
        pl.when(active & (j > 0) & (j < n_ff - 1))(lambda: chunk_step(False, False, 1))
        pl.when(active & (j == n_ff - 1))(lambda: chunk_step(False, True, last_chunks))
    else:
        pl.when(active & (j > 0))(lambda: chunk_step(False, False, 1))


def _ffn(x, norm, wg, wu, wd, *, tm, tf, g_final=None, side_cast=None, head=None, n_rows=None,
         extra=None, name):
    m = n_rows or x.shape[0]
    xe, xne = extra if extra is not None else (None, None)
    assert extra is None or ((xne is not None) == (norm.shape[0] == m) and head is None)
    assert m % tm == 0 and D_FF % tf == 0 and FFN_TF % tf == 0
    n_ff = D_FF // tf
    normed_input = norm.shape[0] == m
    cast_weights = wg.dtype == F32
    assert not cast_weights or m == tm
    assert head is None or (head.shape == (tm, D_MODEL) and m > tm)
    side_src, side_col0 = side_cast if side_cast is not None else (None, 0)
    side_chunks = 0 if side_src is None else (side_src.shape[1] - side_col0) // CAST_TF
    assert side_chunks <= (m // tm) * n_ff and side_col0 % CAST_TF == 0
    row = lambda i, j: (i, 0)
    side_chunk = lambda i, j: jnp.minimum(i * n_ff + j, side_chunks - 1)
    side_in = pl.BlockSpec((D_MODEL, CAST_TF),
                           lambda i, j: (0, side_col0 // CAST_TF + side_chunk(i, j)))
    side_out = pl.BlockSpec((D_MODEL, CAST_TF), lambda i, j: (0, side_chunk(i, j)))
    wj = (lambda i, j: jnp.where(i == 0, 0, j)) if head is not None else (lambda i, j: j)
    per_ff = FFN_TF // tf
    wd_spec = pl.BlockSpec((tf, D_MODEL), lambda i, j: (wj(i, j), 0))
    w_f32_specs = [pl.BlockSpec((D_MODEL, tf), lambda i, j: (0, j))] * 2 + [wd_spec]
    w_bf16_specs = [pl.BlockSpec((None, D_MODEL, tf),
                                 lambda i, j: (wj(i, j) // per_ff, 0, wj(i, j) % per_ff))] * 2
    w_bf16_specs += [wd_spec]
    prefetch_x = m // tm > 1 and n_ff > 1
    in_specs = [pl.BlockSpec(memory_space=pl.ANY) if prefetch_x
                else pl.BlockSpec((tm, D_MODEL), row, pipeline_mode=pl.Buffered(1)),
                pl.BlockSpec((tm, D_MODEL), row) if normed_input else _resident((1, D_MODEL)),
                *(w_f32_specs if cast_weights else w_bf16_specs)]
    args = [x, norm, wg, wu, wd]
    if g_final is not None:
        in_specs.append(_resident((1, D_MODEL)))
        args.append(g_final)
    if side_chunks:
        in_specs.append(side_in)
        args.append(side_src)
    if head is not None:
        in_specs.append(pl.BlockSpec(memory_space=pl.ANY))
        args.append(head)
    if extra is not None:
        in_specs.append(_resident(xe.shape))
        args.append(xe)
        if xne is not None:
            in_specs.append(_resident(xne.shape))
            args.append(xne)
    out_specs = [pl.BlockSpec((tm, D_MODEL), row)]
    out_shape = [jax.ShapeDtypeStruct((m, D_MODEL), F32)]
    if cast_weights:
        out_specs += w_bf16_specs
        out_shape += [jax.ShapeDtypeStruct((D_FF // FFN_TF, D_MODEL, FFN_TF), BF16)] * 2
        out_shape += [jax.ShapeDtypeStruct(wd.shape, BF16)]
    if side_chunks:
        out_specs.append(side_out)
        out_shape.append(jax.ShapeDtypeStruct((D_MODEL, side_chunks * CAST_TF), BF16))
    if extra is not None:
        out_specs.append(pl.BlockSpec(xe.shape, lambda i, j: (0, 0)))
        out_shape.append(jax.ShapeDtypeStruct(xe.shape, F32))
    scratch_shapes = [] if normed_input else [pltpu.VMEM((tm, D_MODEL), BF16)]
    if extra is not None and xne is None:
        scratch_shapes += [pltpu.VMEM(xe.shape, BF16)]
    if prefetch_x:
        scratch_shapes += [pltpu.VMEM((tm, D_MODEL), F32), pltpu.SemaphoreType.DMA(())]
    if head is not None:
        scratch_shapes += [pltpu.SemaphoreType.DMA(())]
    ordered_rows = bool(side_chunks) or prefetch_x
    outs = pl.pallas_call(
        functools.partial(_ffn_kernel, n_ff=n_ff, normed_input=normed_input,
                          has_final=g_final is not None, cast_weights=cast_weights,
                          side_chunks=side_chunks, prefetch_x=prefetch_x,
                          head_given=head is not None,
                          extra_rows=(None if extra is None else "tile0" if m > tm else "all")),
        grid=(m // tm, n_ff),
        in_specs=in_specs,
        out_specs=out_specs,
        out_shape=out_shape,
        scratch_shapes=scratch_shapes,
        compiler_params=pltpu.CompilerParams(
            dimension_semantics=("arbitrary" if ordered_rows else "parallel", "arbitrary"),
            vmem_limit_bytes=VMEM_LIMIT),
        name=name,
    )(*args)
    return outs if len(outs) > 1 else outs[0]


def _merge_kernel(x_ref, a_ref, c_ref, woa_ref, woc_ref, g_ref, x2_ref, xn_ref):
    x = x_ref[...] + (_dot(a_ref[...].astype(BF16), woa_ref[...]) +
                      _dot(c_ref[...].astype(BF16), woc_ref[...]))
    x2_ref[...] = x
    xn_ref[...] = _rms(x, g_ref[...]).astype(BF16)


def _merge(x, a, c, woa, woc, g, *, tm, name):
    m = x.shape[0]
    assert m % tm == 0
    row = lambda i: (i, 0)
    return pl.pallas_call(
        _merge_kernel,
        grid=(m // tm,),
        in_specs=[pl.BlockSpec((tm, D_MODEL), row), pl.BlockSpec((tm, ATTN_DIM), row),
                  pl.BlockSpec((tm, CONV_DIM), row), _resident((ATTN_DIM, D_MODEL)),
                  _resident((CONV_DIM, D_MODEL)), _resident((1, D_MODEL))],
        out_specs=[pl.BlockSpec((tm, D_MODEL), row), pl.BlockSpec((tm, D_MODEL), row)],
        out_shape=[jax.ShapeDtypeStruct((m, D_MODEL), F32),
                   jax.ShapeDtypeStruct((m, D_MODEL), BF16)],
        compiler_params=pltpu.CompilerParams(
            dimension_semantics=("parallel",), vmem_limit_bytes=VMEM_LIMIT),
        name=name,
    )(x, a, c, woa, woc, g)


K0, V0, B0, C0, H0, R_END = 0, 256, 512, 1536, 2560, 3584


def _project_conv(x_ref, g_ref, wr_ref):
    xn = _rms(x_ref[...], g_ref[...]).astype(BF16)
    gate_b = _dot(xn, wr_ref[:, B0:C0])
    u = _dot(xn, wr_ref[:, C0:H0]) * _dot(xn, wr_ref[:, H0:R_END])
    return xn, gate_b, u


def _project_qkv(xn, wq_ref, wr_ref):
    q = _dot(xn, wq_ref[...]) * (HEAD_DIM ** -0.5)
    return q, _dot(xn, wr_ref[:, K0:V0]), _dot(xn, wr_ref[:, V0:B0])


def _proj_prompt_kernel(x_ref, g_ref, wq_ref, wr_ref, wc_ref, gc_ref, umeta_ref,
                        fg_ref, fu_ref, fd_ref,
                        q_ref, k_ref, v_ref, cn_ref, cst_ref, bg_ref, bu_ref, bd_ref,
                        us_ref, *, tm, tiles_per_seq):
    t = pl.program_id(0) % tiles_per_seq

    @pl.when(t == 0)
    def _():
        us_ref[0:8, :] = umeta_ref[...]

    @pl.when(t != 0)
    def _():
        us_ref[0:8, :] = us_ref[tm:tm + 8, :]

    hm = tm // PROJ_ROW_CHUNKS
    casts = ((fg_ref, bg_ref), (fu_ref, bu_ref), (fd_ref, bd_ref))
    for h in range(PROJ_ROW_CHUNKS):
        rows = slice(h * hm, (h + 1) * hm)
        for src, dst in casts[h::PROJ_ROW_CHUNKS]:
            dst[...] = src[...].astype(BF16)
        xn, gate_b, u = _project_conv(x_ref.at[rows], g_ref, wr_ref)
        us_ref[8 + h * hm:8 + (h + 1) * hm, :] = u
        y = (wc_ref[0:1, :] * us_ref[6 + h * hm:6 + (h + 1) * hm, :]
             + wc_ref[1:2, :] * us_ref[7 + h * hm:7 + (h + 1) * hm, :] + wc_ref[2:3, :] * u)
        cn_ref[rows, :] = _rms(gate_b * y, gc_ref[...]).astype(BF16)
        q, k, v = _project_qkv(xn, wq_ref, wr_ref)
        q_ref[rows, :] = q.astype(BF16)
        k_ref[rows, :] = k
        v_ref[rows, :] = v
    cst_ref[0] = us_ref[tm + 6:tm + 8, :]


def _proj_prompt(x, g, wq, wr, wc, gc, umeta, ffn_w, *, tm, seq):
    m = x.shape[0]
    tiles_per_seq = seq // tm
    n_steps = m // tm
    n_cast = D_FF // CAST_TF
    assert D_FF % CAST_TF == 0 and n_cast <= n_steps
    row = lambda i: (i, 0)
    chunk = lambda i: jnp.minimum(i, n_cast - 1)
    cast_cols = pl.BlockSpec((D_MODEL, CAST_TF), lambda i: (0, chunk(i)))
    cast_rows = pl.BlockSpec((CAST_TF, D_MODEL), lambda i: (chunk(i), 0))
    per_ff = FFN_TF // CAST_TF
    assert FFN_TF % CAST_TF == 0
    cast_cols_out = pl.BlockSpec((None, D_MODEL, CAST_TF),
                                 lambda i: (chunk(i) // per_ff, 0, chunk(i) % per_ff))
    chunked = jax.ShapeDtypeStruct((D_FF // FFN_TF, D_MODEL, FFN_TF), BF16)
    return pl.pallas_call(
        functools.partial(_proj_prompt_kernel, tm=tm, tiles_per_seq=tiles_per_seq),
        grid=(n_steps,),
        in_specs=[pl.BlockSpec((tm, D_MODEL), row), _resident((1, D_MODEL)),
                  _resident((D_MODEL, ATTN_DIM)), _resident((D_MODEL, R_END)),
                  _resident((3, CONV_DIM)), _resident((1, CONV_DIM)), _resident((8, CONV_DIM)),
                  cast_cols, cast_cols, cast_rows],
        out_specs=[pl.BlockSpec((tm, ATTN_DIM), row), pl.BlockSpec((tm, KV_DIM), row),
                   pl.BlockSpec((tm, KV_DIM), row), pl.BlockSpec((tm, CONV_DIM), row),
                   pl.BlockSpec((1, 2, CONV_DIM), lambda i: (i // tiles_per_seq, 0, 0)),
                   cast_cols_out, cast_cols_out, cast_rows],
        out_shape=[jax.ShapeDtypeStruct((m, ATTN_DIM), BF16),
                   jax.ShapeDtypeStruct((m, KV_DIM), F32),
                   jax.ShapeDtypeStruct((m, KV_DIM), F32),
                   jax.ShapeDtypeStruct((m, CONV_DIM), BF16),
                   jax.ShapeDtypeStruct((m // seq, 2, CONV_DIM), F32),
                   chunked, chunked, jax.ShapeDtypeStruct(ffn_w[2].shape, BF16)],
        scratch_shapes=[pltpu.VMEM((tm + 8, CONV_DIM), F32)],
        compiler_params=pltpu.CompilerParams(
            dimension_semantics=("arbitrary",), vmem_limit_bytes=VMEM_LIMIT),
        name="proj_prompt",
    )(x, g, wq, wr, wc, gc, umeta, *ffn_w)


def _proj_tail_kernel(x_ref, g_ref, wq_ref, wr_ref, wc_ref, gc_ref, h0_ref, h1_ref,
                      q_ref, k_ref, v_ref, u_ref, cn_ref):
    xn, gate_b, u = _project_conv(x_ref, g_ref, wr_ref)
    q, k, v = _project_qkv(xn, wq_ref, wr_ref)
    q_ref[...] = q
    k_ref[...] = k
    v_ref[...] = v
    u_ref[...] = u
    y = wc_ref[0:1, :] * h0_ref[...] + wc_ref[1:2, :] * h1_ref[...] + wc_ref[2:3, :] * u
    cn_ref[...] = _rms(gate_b * y, gc_ref[...]).astype(BF16)


def _proj_tail(x, g, wq, wr, wc, gc, h0, h1):
    m = x.shape[0]
    full = lambda shape: pl.BlockSpec(shape, lambda i: (0,) * len(shape))
    return pl.pallas_call(
        _proj_tail_kernel,
        grid=(1,),
        in_specs=[full((m, D_MODEL)), full((1, D_MODEL)), _resident((D_MODEL, ATTN_DIM)),
                  _resident((D_MODEL, R_END)),
                  full((3, CONV_DIM)), full((1, CONV_DIM)), full((m, CONV_DIM)),
                  full((m, CONV_DIM))],
        out_specs=[full((m, ATTN_DIM)), full((m, KV_DIM)), full((m, KV_DIM)),
                   full((m, CONV_DIM)), full((m, CONV_DIM))],
        out_shape=[jax.ShapeDtypeStruct((m, ATTN_DIM), F32),
                   jax.ShapeDtypeStruct((m, KV_DIM), F32),
                   jax.ShapeDtypeStruct((m, KV_DIM), F32),
                   jax.ShapeDtypeStruct((m, CONV_DIM), F32),
                   jax.ShapeDtypeStruct((m, CONV_DIM), BF16)],
        compiler_params=pltpu.CompilerParams(
            dimension_semantics=("arbitrary",), vmem_limit_bytes=VMEM_LIMIT),
        name="proj_tail",
    )(x, g, wq, wr, wc, gc, h0, h1)


class _BandedAttention:
    def __init__(self, first, sink_ref, q_ref, kprev, kcur, vprev, vcur, ga_ref, o_ref, nsub):
        self.sink_ref, self.q_ref, self.ga_ref, self.o_ref = sink_ref, q_ref, ga_ref, o_ref
        self.n_pairs = N_KV_HEADS // 2
        self.units = [(sb, pair) for sb in range(nsub) for pair in range(self.n_pairs)]
        kall = jnp.concatenate([kprev, kcur], axis=0)
        vall = jnp.concatenate([vprev, vcur], axis=0)
        c = lax.broadcasted_iota(jnp.int32, (2 * WINDOW, WINDOW), 0)
        r = lax.broadcasted_iota(jnp.int32, (2 * WINDOW, WINDOW), 1)
        self.band = (c >= r) & (c <= r + WINDOW)
        self.band_first = self.band & ((c >= WINDOW - N_META) | jnp.logical_not(first))
        low = lax.broadcasted_iota(jnp.int32, kall.shape[:1] + (LANES,), 1) < HALF
        self.k_half, self.vt_half = [], []
        for pair in range(self.n_pairs):
            kcol = kall[:, pair * LANES:(pair + 1) * LANES]
            vcol = vall[:, pair * LANES:(pair + 1) * LANES]
            self.k_half.append((jnp.where(low, kcol, 0.0).astype(BF16),
                                jnp.where(low, 0.0, kcol).astype(BF16)))
            self.vt_half.append((jnp.where(low, vcol, 0.0).T.astype(BF16),
                                 jnp.where(low, 0.0, vcol).T.astype(BF16)))
        self.outs = {}

    def scores(self, unit):
        sb, pair = unit
        rows = slice(sb * WINDOW, (sb + 1) * WINDOW)
        keys = slice(sb * WINDOW, (sb + 2) * WINDOW)
        kst = jnp.concatenate([self.k_half[pair][0][keys], self.k_half[pair][1][keys]], axis=0)
        qst = jnp.concatenate(
            [self.q_ref[rows, (grp * self.n_pairs + pair) * LANES:
                        (grp * self.n_pairs + pair + 1) * LANES]
             for grp in range(GQA_GROUP)], axis=0)
        return _dot_t(kst, qst)

    def softmax(self, unit, st_all):
        sb, pair = unit
        valid = self.band_first if sb == 0 else self.band
        pt_rows = []
        for half in range(2):
            pt_cols = []
            for grp in range(GQA_GROUP):
                sink = self.sink_ref[(2 * pair + half) * GQA_GROUP + grp]
                st = st_all[half * 2 * WINDOW:(half + 1) * 2 * WINDOW,
                            grp * WINDOW:(grp + 1) * WINDOW]
                st = jnp.where(valid, st, NEG)
                mx = jnp.maximum(jnp.max(st, axis=0, keepdims=True), sink)
                e = jnp.exp(st - mx)
                den = jnp.sum(e, axis=0, keepdims=True) + jnp.exp(sink - mx)
                pt_cols.append((e * (1.0 / den)).astype(BF16))
            pt_rows.append(jnp.concatenate(pt_cols, axis=1))
        return jnp.concatenate(pt_rows, axis=0)

    def values(self, unit, pt):
        sb, pair = unit
        keys = slice(sb * WINDOW, (sb + 2) * WINDOW)
        vst = jnp.concatenate([self.vt_half[pair][0][:, keys], self.vt_half[pair][1][:, keys]],
                              axis=1)
        ot_all = _dot(vst, pt)
        for grp in range(GQA_GROUP):
            self.outs[(sb, grp * self.n_pairs + pair)] = ot_all[:, grp * WINDOW:(grp + 1) * WINDOW]
        if pair == self.n_pairs - 1:
            self._normalise(sb)

    def _normalise(self, sb):
        rows = slice(sb * WINDOW, (sb + 1) * WINDOW)
        cols = range(ATTN_DIM // LANES)
        ssq = sum(jnp.sum(self.outs[(sb, col)] ** 2, axis=0, keepdims=True) for col in cols)
        inv = lax.rsqrt(ssq * (1.0 / ATTN_DIM) + EPS)
        for col in cols:
            sl = slice(col * LANES, (col + 1) * LANES)
            self.o_ref[rows, sl] = ((self.outs.pop((sb, col)) * inv).T
                                    * self.ga_ref[:, sl]).astype(BF16)


def _attn_merge_kernel(sink_ref, q_ref, kp_ref, kc_ref, vp_ref, vc_ref, km_ref, vm_ref, ga_ref,
                       x_ref, c_ref, woa_ref, woc_ref, x2_ref,
                       an_new_ref, an_old_ref, *, nsub, tiles_per_seq, n_tiles):
    t = pl.program_id(0)

    @pl.when(t == 0)
    def _():
        an_new_ref[...] = jnp.zeros(an_new_ref.shape, an_new_ref.dtype)

    an_old_ref[...] = an_new_ref[...]

    first = (jnp.minimum(t, n_tiles - 1) % tiles_per_seq) == 0
    attn = _BandedAttention(first, sink_ref, q_ref,
                            jnp.where(first, km_ref[...], kp_ref[...]), kc_ref[...],
                            jnp.where(first, vm_ref[...], vp_ref[...]), vc_ref[...],
                            ga_ref, an_new_ref, nsub)
    units = attn.units
    n_chunks = len(units)
    cw = D_MODEL // n_chunks
    st_next = attn.scores(units[0])
    pt_prev = None
    for u, unit in enumerate(units):
        cols = slice(u * cw, (u + 1) * cw)
        x2_ref[:, cols] = x_ref[:, cols] + (_dot(an_old_ref[...], woa_ref[:, cols]) +
                                            _dot(c_ref[...], woc_ref[:, cols]))
        st_all, st_next = st_next, (attn.scores(units[u + 1]) if u + 1 < n_chunks else None)
        if pt_prev is not None:
            attn.values(units[u - 1], pt_prev)
        pt_prev = attn.softmax(unit, st_all)
    attn.values(units[-1], pt_prev)


def _attn_merge(sinks, q, k, v, kmeta, vmeta, ga, x, c, woa, woc, *, seq, nsub):
    m = x.shape[0]
    tm = nsub * WINDOW
    n_tiles = m // tm
    tiles_per_seq = seq // tm
    cur = lambda t: (jnp.minimum(t, n_tiles - 1), 0)
    prev = lambda t: (jnp.maximum(jnp.minimum(t, n_tiles - 1) * nsub - 1, 0), 0)
    lag = lambda t: (jnp.maximum(t - 1, 0), 0)
    return pl.pallas_call(
        functools.partial(_attn_merge_kernel, nsub=nsub, tiles_per_seq=tiles_per_seq,
                          n_tiles=n_tiles),
        grid=(n_tiles + 1,),
        in_specs=[pl.BlockSpec(memory_space=pltpu.SMEM),
                  pl.BlockSpec((tm, ATTN_DIM), cur),
                  pl.BlockSpec((WINDOW, KV_DIM), prev), pl.BlockSpec((tm, KV_DIM), cur),
                  pl.BlockSpec((WINDOW, KV_DIM), prev), pl.BlockSpec((tm, KV_DIM), cur),
                  _resident((WINDOW, KV_DIM)), _resident((WINDOW, KV_DIM)),
                  _resident((1, ATTN_DIM)),
                  pl.BlockSpec((tm, D_MODEL), lag), pl.BlockSpec((tm, CONV_DIM), lag),
                  _resident((ATTN_DIM, D_MODEL)), _resident((CONV_DIM, D_MODEL))],
        out_specs=pl.BlockSpec((tm, D_MODEL), lag),
        out_shape=jax.ShapeDtypeStruct((m, D_MODEL), F32),
        scratch_shapes=[pltpu.VMEM((tm, ATTN_DIM), BF16), pltpu.VMEM((tm, ATTN_DIM), BF16)],
        compiler_params=pltpu.CompilerParams(
            dimension_semantics=("arbitrary",), vmem_limit_bytes=VMEM_LIMIT),
        name="attn_merge_prompt",
    )(sinks, q, k, k, v, v, kmeta, vmeta, ga, x, c, woa, woc)


def _attn_sample_kernel(sink_ref, q_ref, kn_ref, vn_ref, ck_ref, cv_ref, ga_ref,
                        ok_ref, ov_ref, an_ref, *, bs):
    n_pairs = N_KV_HEADS // 2
    heads_per_col = 2 * GQA_GROUP
    last_key = lax.broadcasted_iota(jnp.int32, (KV_DIM, WINDOW), 1) == WINDOW - 1
    pad = jnp.zeros((WINDOW - bs, KV_DIM), F32)
    knt = jnp.concatenate([kn_ref[...], pad], axis=0).T
    vnt = jnp.concatenate([vn_ref[...], pad], axis=0).T
    for s in range(bs):
        ok_ref[s] = jnp.where(last_key, knt[:, s:s + 1], pltpu.roll(ck_ref[s], WINDOW - 1, 1))
        ov_ref[s] = jnp.where(last_key, vnt[:, s:s + 1], pltpu.roll(cv_ref[s], WINDOW - 1, 1))

    n_rows = heads_per_col * bs
    row = lax.broadcasted_iota(jnp.int32, (n_rows, bs * WINDOW), 0)
    lane = lax.broadcasted_iota(jnp.int32, (n_rows, bs * WINDOW), 1)
    log2 = lambda n: n.bit_length() - 1
    assert bs == 1 << log2(bs) and WINDOW == 1 << log2(WINDOW)
    same_seq = (row & (bs - 1)) == (lane >> log2(WINDOW))
    head_row = lax.broadcasted_iota(jnp.int32, (n_rows, 1), 0) >> log2(bs)
    low = lax.broadcasted_iota(jnp.int32, (bs, LANES), 1) < HALF

    cols = [None] * (ATTN_DIM // LANES)
    for pair in range(n_pairs):
        sl = slice(pair * LANES, (pair + 1) * LANES)
        pieces = []
        sk = jnp.zeros((n_rows, 1), F32)
        for grp in range(GQA_GROUP):
            col = grp * n_pairs + pair
            qcol = q_ref[:, col * LANES:(col + 1) * LANES]
            pieces += [jnp.where(low, qcol, 0.0), jnp.where(low, 0.0, qcol)]
            for half in range(2):
                sink = sink_ref[(2 * pair + half) * GQA_GROUP + grp]
                sk = jnp.where(head_row == 2 * grp + half, sink, sk)
        lq = jnp.concatenate(pieces, axis=0).astype(BF16)
        kstack = jnp.concatenate([ck_ref[s, sl, :] for s in range(bs)], axis=1).astype(BF16)
        vstack = jnp.concatenate([cv_ref[s, sl, :] for s in range(bs)], axis=1).astype(BF16)
        s = jnp.where(same_seq, _dot(lq, kstack), NEG)
        k_new = jnp.concatenate([kn_ref[:, sl]] * heads_per_col, axis=0).astype(BF16)
        v_new = jnp.concatenate([vn_ref[:, sl]] * heads_per_col, axis=0).astype(BF16)
        s_new = jnp.sum(lq.astype(F32) * k_new.astype(F32), axis=-1, keepdims=True)
        mx = jnp.maximum(jnp.maximum(jnp.max(s, axis=-1, keepdims=True), s_new), sk)
        e = jnp.exp(s - mx)
        e_new = jnp.exp(s_new - mx)
        rden = 1.0 / (jnp.sum(e, axis=-1, keepdims=True) + e_new + jnp.exp(sk - mx))
        p = (e * rden).astype(BF16)
        p_new = (e_new * rden).astype(BF16).astype(F32)
        o = _dot_t(p, vstack) + p_new * v_new.astype(F32)
        for grp in range(GQA_GROUP):
            lo = o[(2 * grp) * bs:(2 * grp + 1) * bs]
            hi = o[(2 * grp + 1) * bs:(2 * grp + 2) * bs]
            cols[grp * n_pairs + pair] = jnp.where(low, lo, hi)
    an_ref[...] = _rms(jnp.concatenate(cols, axis=1), ga_ref[...])


def _attn_sample(sinks, q, kn, vn, ck, cv, ga, *, bs):
    nseq = q.shape[0]
    row = lambda i: (i, 0)
    blk3 = pl.BlockSpec((bs, KV_DIM, WINDOW), lambda i: (i, 0, 0))
    return pl.pallas_call(
        functools.partial(_attn_sample_kernel, bs=bs),
        grid=(nseq // bs,),
        in_specs=[pl.BlockSpec(memory_space=pltpu.SMEM),
                  pl.BlockSpec((bs, ATTN_DIM), row), pl.BlockSpec((bs, KV_DIM), row),
                  pl.BlockSpec((bs, KV_DIM), row), blk3, blk3,
                  pl.BlockSpec((1, ATTN_DIM), lambda i: (0, 0))],
        out_specs=[blk3, blk3, pl.BlockSpec((bs, ATTN_DIM), row)],
        out_shape=[jax.ShapeDtypeStruct(ck.shape, F32), jax.ShapeDtypeStruct(cv.shape, F32),
                   jax.ShapeDtypeStruct((nseq, ATTN_DIM), F32)],
        compiler_params=pltpu.CompilerParams(
            dimension_semantics=("parallel",), vmem_limit_bytes=VMEM_LIMIT),
        name="attn_sample",
    )(sinks, q, kn, vn, ck, cv, ga)


def _prep_wq_kernel(w_ref, wq_ref):
    low = lax.broadcasted_iota(jnp.int32, (w_ref.shape[0], LANES), 1) < HALF
    for grp in range(GQA_GROUP):
        def piece(kv, want_low):
            col = kv * (GQA_GROUP // 2) + grp // 2
            blk = w_ref[:, col * LANES:(col + 1) * LANES]
            return blk if (grp % 2 == 0) == want_low else pltpu.roll(blk, HALF, 1)
        for pair in range(N_KV_HEADS // 2):
            col = grp * (N_KV_HEADS // 2) + pair
            wq_ref[:, col * LANES:(col + 1) * LANES] = jnp.where(
                low, piece(2 * pair, True), piece(2 * pair + 1, False)).astype(BF16)


def _prep_wq(w_in2d, *, tr):
    rows = w_in2d.shape[0]
    return pl.pallas_call(
        _prep_wq_kernel,
        grid=(rows // tr,),
        in_specs=[pl.BlockSpec((tr, ATTN_DIM), lambda i: (i, 0))],
        out_specs=pl.BlockSpec((tr, ATTN_DIM), lambda i: (i, 0)),
        out_shape=jax.ShapeDtypeStruct((rows, ATTN_DIM), BF16),
        compiler_params=pltpu.CompilerParams(
            dimension_semantics=("parallel",), vmem_limit_bytes=VMEM_LIMIT),
        name="prep_wq",
    )(w_in2d)


def _group_major(w, axis):
    shape = w.shape
    w = w.reshape(shape[:axis] + (N_KV_HEADS, GQA_GROUP, HEAD_DIM) + shape[axis + 1:])
    return jnp.swapaxes(w, axis, axis + 1).reshape(shape)


def kernel(x_prompt, x_sample, cache_swa_k, cache_swa_v, state_conv, meta_tokens, g_ffn1, w1_gate, w1_up, w1_down, g_mix, w_in, attn_sinks, w_conv, g_attn_out, g_conv_out, w_out, g_ffn2, w2_gate, w2_up, w2_down, g_final):
    assert g_ffn1.shape[0] == 1, "one layer"
    batch, seq, _ = x_prompt.shape
    n_dec = x_sample.shape[0]
    assert x_sample.shape[1] == 1 and cache_swa_k.shape[2] == WINDOW

    row2d = lambda a: a.reshape(1, -1).astype(F32)
    wq = _prep_wq(w_in[0], tr=PREP_ROWS)
    wo_a = _group_major(w_out[0, :ATTN_DIM], 0).astype(BF16)
    wo_c = w_out[0, ATTN_DIM:].astype(BF16)
    ga = row2d(_group_major(g_attn_out[0], 0))
    gc = row2d(g_conv_out[0])
    sinks = attn_sinks[0].astype(F32)
    wc = w_conv[0].astype(F32)

    xp = x_prompt.reshape(batch * seq, D_MODEL)
    xt = jnp.concatenate([x_sample[:, 0, :], meta_tokens.astype(x_prompt.dtype)], axis=0)
    n_tail = xt.shape[0]

    xh1, w1g, w1u, w1d, xt1 = _ffn(xp, row2d(g_ffn1[0]), w1_gate[0], w1_up[0], w1_down[0],
                                   tm=FFN_TM, tf=HEAD_TF, n_rows=FFN_TM, extra=(xt, None),
                                   name="ffn1_head")
    xp1, wr = _ffn(xp, row2d(g_ffn1[0]), w1g, w1u, w1d, tm=FFN_TM, tf=FFN_TF,
                   side_cast=(w_in[0], ATTN_DIM), head=xh1, name="ffn1_prompt")

    st = state_conv[0]
    zpad = jnp.zeros((N_META, CONV_DIM), F32)
    h0 = jnp.concatenate([st[:, 0, :], zpad], axis=0)
    h1 = jnp.concatenate([st[:, 1, :], zpad], axis=0)
    qt, kt, vt, ut, cnt = _proj_tail(xt1, row2d(g_mix[0]), wq, wr, wc, gc, h0, h1)
    kmeta = jnp.pad(kt[n_dec:], ((WINDOW - N_META, 0), (0, 0)))
    vmeta = jnp.pad(vt[n_dec:], ((WINDOW - N_META, 0), (0, 0)))
    umeta = ut[n_tail - 8:]

    qp, kp, vp, cnp, cstate, w2g, w2u, w2d = _proj_prompt(
        xp1, row2d(g_mix[0]), wq, wr, wc, gc, umeta, (w2_gate[0], w2_up[0], w2_down[0]),
        tm=PROJ_TM, seq=seq)
    xp2 = _attn_merge(sinks, qp, kp, vp, kmeta, vmeta, ga, xp1, cnp, wo_a, wo_c,
                      seq=seq, nsub=ATTN_NSUB)

    to_channel_major = lambda c: jnp.transpose(c[0], (0, 2, 3, 1)).reshape(n_dec, KV_DIM, WINDOW)
    from_channel_major = lambda c: jnp.transpose(
        c.reshape(n_dec, N_KV_HEADS, HEAD_DIM, WINDOW), (0, 3, 1, 2))[None]
    nk, nv, ans = _attn_sample(sinks, qt[:n_dec], kt[:n_dec], vt[:n_dec],
                               to_channel_major(cache_swa_k), to_channel_major(cache_swa_v), ga,
                               bs=SAMPLE_BS)

    xs2, xsn = _merge(xt1[:n_dec], ans, cnt[:n_dec], wo_a, wo_c, row2d(g_ffn2[0]), tm=n_dec,
                      name="merge_tail")
    yp, ys = _ffn(xp2, row2d(g_ffn2[0]), w2g, w2u, w2d, tm=FFN_TM, tf=FFN_TF,
                  g_final=row2d(g_final), extra=(xs2, None), name="ffn2_prompt")

    kv_shape = (1, batch, WINDOW, N_KV_HEADS, HEAD_DIM)
    new_k_prompt = kp.reshape(batch, seq, KV_DIM)[:, seq - WINDOW:].reshape(kv_shape)
    new_v_prompt = vp.reshape(batch, seq, KV_DIM)[:, seq - WINDOW:].reshape(kv_shape)
    new_conv_sample = jnp.stack([st[:, 1, :], ut[:n_dec]], axis=1)[None]
    return (yp.reshape(batch, seq, D_MODEL), ys.reshape(n_dec, 1, D_MODEL),
            new_k_prompt, new_v_prompt, cstate[None],
            from_channel_major(nk), from_channel_major(nv), new_conv_sample)
```
